```python
import math
import jax, jax.numpy as jnp
from jax import lax
import numpy as np

D_MODEL = 1024
BATCH = 8
SEQ = 8192
DEPTH = 2

CHUNK = 64
Q_BLOCK = 128
HEAD_DIM = 64
ROPE_THETA = 10000.0
EPS = 1e-6
N_MEM = 256
MEM_HEADS = 4
MEM_WIDTH = MEM_HEADS * HEAD_DIM
MIX_WIDTH = D_MODEL
SEQ_WIDTH = MIX_WIDTH - MEM_WIDTH
DIFF_HEADS = SEQ_WIDTH // (2 * HEAD_DIM)
DSA_HEADS = SEQ_WIDTH // HEAD_DIM
IDX_HEADS = 8
IDX_DIM = 64
DSA_TOPK_MAX = 256
D_FF = 4 * D_MODEL
N_DIFF_LAYERS = (DEPTH + 1) // 2
N_DSA_LAYERS = DEPTH // 2

DIFF_SIZES = (2 * DIFF_HEADS * HEAD_DIM,
              2 * DIFF_HEADS * HEAD_DIM,
              DIFF_HEADS * 2 * HEAD_DIM,
              MEM_WIDTH)
DSA_SIZES = (DSA_HEADS * HEAD_DIM,
             HEAD_DIM,
             HEAD_DIM,
             IDX_HEADS * IDX_DIM,
             IDX_DIM,
             IDX_HEADS,
             MEM_WIDTH)
DIFF_IN = sum(DIFF_SIZES)
DSA_IN = sum(DSA_SIZES)

kernel_name = "hybrid_diffattn_dsa_memory_trunk"


def rms_norm(x, g):
    xf = x.astype(jnp.float32)
    y = xf * lax.rsqrt(jnp.mean(xf * xf, axis=-1, keepdims=True) + EPS)
    return (y * g.astype(jnp.float32)).astype(x.dtype)


def split_cols(a, sizes):
    idx = np.cumsum(sizes)[:-1].tolist()
    return jnp.split(a, idx, axis=-1)


def rope_tables(positions, dim):
    inv = ROPE_THETA ** (-jnp.arange(0, dim, 2, dtype=jnp.float32) / dim)
    ang = positions.astype(jnp.float32)[..., None] * inv
    return jnp.cos(ang), jnp.sin(ang)


def apply_rope(x, cos, sin):
    xf = x.astype(jnp.float32)
    x1, x2 = jnp.split(xf, 2, axis=-1)
    out = jnp.concatenate([x1 * cos - x2 * sin, x2 * cos + x1 * sin], axis=-1)
    return out.astype(x.dtype)


def to_blocks(a):
    b, s = a.shape[:2]
    return jnp.moveaxis(a.reshape(b, s // Q_BLOCK, Q_BLOCK, *a.shape[2:]), 1, 0)


def from_blocks(a):
    nb, b, q = a.shape[:3]
    return jnp.moveaxis(a, 0, 1).reshape(b, nb * q, *a.shape[3:])


def memory_attention(qm, mem_n, w_kv):
    b, s, _ = qm.shape
    km, vm = jnp.split(mem_n @ w_kv, 2, axis=-1)
    qm = qm.reshape(b, s, MEM_HEADS, HEAD_DIM)
    km = km.reshape(b, N_MEM, MEM_HEADS, HEAD_DIM)
    vm = vm.reshape(b, N_MEM, MEM_HEADS, HEAD_DIM)
    sc = jnp.einsum('bshd,bmhd->bhsm', qm, km).astype(jnp.float32) * (HEAD_DIM ** -0.5)
    p = jax.nn.softmax(sc, axis=-1).astype(vm.dtype)
    o = jnp.einsum('bhsm,bmhd->bshd', p, vm)
    return o.reshape(b, s, MEM_WIDTH)


def diff_attention(h, w_in, lq1, lk1, lq2, lk2, g_subln, cos, sin, chunk_id, layer_idx):
    b, s, _ = h.shape
    q, k, v, qm = split_cols(h @ w_in, DIFF_SIZES)
    q = apply_rope(q.reshape(b, s, 2 * DIFF_HEADS, HEAD_DIM), cos[:, :, None], sin[:, :, None])
    k = apply_rope(k.reshape(b, s, 2 * DIFF_HEADS, HEAD_DIM), cos[:, :, None], sin[:, :, None])
    v = v.reshape(b, s, DIFF_HEADS, 2 * HEAD_DIM)
    lam_init = 0.8 - 0.6 * math.exp(-0.3 * layer_idx)
    lam = (jnp.exp(jnp.sum(lq1.astype(jnp.float32) * lk1.astype(jnp.float32)))
           - jnp.exp(jnp.sum(lq2.astype(jnp.float32) * lk2.astype(jnp.float32))) + lam_init)
    scale = HEAD_DIM ** -0.5

    def block(args):
        qb, cqb = args
        sc = jnp.einsum('bqhd,bkhd->bhqk', qb, k).astype(jnp.float32) * scale
        mask = (cqb[:, :, None] >= chunk_id[:, None, :])[:, None]
        p = jax.nn.softmax(jnp.where(mask, sc, -jnp.inf), axis=-1)
        p = p.reshape(b, DIFF_HEADS, 2, qb.shape[1], s)
        a = p[:, :, 0] - lam * p[:, :, 1]
        return jnp.einsum('bhqk,bkhd->bqhd', a.astype(v.dtype), v)

    o = from_blocks(lax.map(block, (to_blocks(q), to_blocks(chunk_id))))
    o = rms_norm(o, g_subln) * (1.0 - lam_init)
    return o.reshape(b, s, SEQ_WIDTH), qm


def dsa_attention(h, w_in, cos, sin, chunk_id):
    b, s, _ = h.shape
    q, k, v, iq, ik, iw, qm = split_cols(h @ w_in, DSA_SIZES)
    q = apply_rope(q.reshape(b, s, DSA_HEADS, HEAD_DIM), cos[:, :, None], sin[:, :, None])
    k = apply_rope(k, cos, sin)
    iq = apply_rope(iq.reshape(b, s, IDX_HEADS, IDX_DIM), cos[:, :, None], sin[:, :, None])
    ik = apply_rope(ik, cos, sin)
    iw = iw * (IDX_HEADS ** -0.5 * IDX_DIM ** -0.5)
    top_k = min(DSA_TOPK_MAX, s // 4)
    scale = HEAD_DIM ** -0.5
    gather = jax.vmap(lambda tb, ib: tb[ib])

    def block(args):
        qb, iqb, iwb, cqb = args
        logits = jnp.einsum('bqhd,bkd->bqhk', iqb, ik).astype(jnp.float32)
        score = jnp.einsum('bqhk,bqh->bqk', jax.nn.relu(logits), iwb.astype(jnp.float32))
        admissible = cqb[:, :, None] >= chunk_id[:, None, :]
        score = jnp.where(admissible, score, -jnp.inf)
        top_val, top_idx = lax.top_k(score, top_k)
        valid = top_val > -jnp.inf
        kg = gather(k, top_idx)
        vg = gather(v, top_idx)
        sc = jnp.einsum('bqhd,bqkd->bhqk', qb, kg).astype(jnp.float32) * scale
        p = jax.nn.softmax(jnp.where(valid[:, None], sc, -jnp.inf), axis=-1)
        return jnp.einsum('bhqk,bqkd->bqhd', p.astype(vg.dtype), vg)

    o = from_blocks(lax.map(block, (to_blocks(q), to_blocks(iq), to_blocks(iw), to_blocks(chunk_id))))
    return o.reshape(b, s, SEQ_WIDTH), qm


def setup_inputs(seed: int = 0) -> dict:
    key = jax.random.key(seed)
    ks = jax.random.split(key, 24)

    def w(k, shape, fan_in):
        return jax.random.normal(k, shape, jnp.float32) * (fan_in ** -0.5)

    def gain(k, shape):
        return 1.0 + 0.05 * jax.random.normal(k, shape, jnp.float32)

    x = jax.random.normal(ks[0], (BATCH, SEQ, D_MODEL), jnp.float32)
    mem = jax.random.normal(ks[1], (BATCH, N_MEM, D_MODEL), jnp.float32)
    offsets = jax.random.randint(ks[2], (BATCH, 1), 0, 16) * CHUNK
    positions = (offsets + jnp.arange(SEQ)[None, :]).astype(jnp.int32)
    return {
        "x": x,
        "mem": mem,
        "positions": positions,
        "g_pre_mix": gain(ks[3], (DEPTH, D_MODEL)),
        "g_post_mix": gain(ks[4], (DEPTH, D_MODEL)),
        "g_mem": gain(ks[5], (DEPTH, D_MODEL)),
        "w_mem_kv": w(ks[6], (DEPTH, D_MODEL, 2 * MEM_WIDTH), D_MODEL),
        "w_out": w(ks[7], (DEPTH, MIX_WIDTH, D_MODEL), MIX_WIDTH),
        "g_pre_mlp": gain(ks[8], (DEPTH, D_MODEL)),
        "g_post_mlp": gain(ks[9], (DEPTH, D_MODEL)),
        "w_mlp_up": w(ks[10], (DEPTH, D_MODEL, D_FF), D_MODEL),
        "w_mlp_down": w(ks[11], (DEPTH, D_FF, D_MODEL), D_FF),
        "w_in_diff": w(ks[12], (N_DIFF_LAYERS, D_MODEL, DIFF_IN), D_MODEL),
        "lambda_q1": 0.1 * jax.random.normal(ks[13], (N_DIFF_LAYERS, HEAD_DIM), jnp.float32),
        "lambda_k1": 0.1 * jax.random.normal(ks[14], (N_DIFF_LAYERS, HEAD_DIM), jnp.float32),
        "lambda_q2": 0.1 * jax.random.normal(ks[15], (N_DIFF_LAYERS, HEAD_DIM), jnp.float32),
        "lambda_k2": 0.1 * jax.random.normal(ks[16], (N_DIFF_LAYERS, HEAD_DIM), jnp.float32),
        "g_diff_subln": gain(ks[17], (N_DIFF_LAYERS, 2 * HEAD_DIM)),
        "w_in_dsa": w(ks[18], (N_DSA_LAYERS, D_MODEL, DSA_IN), D_MODEL),
    }


def reference(x, mem, positions, g_pre_mix, g_post_mix, g_mem, w_mem_kv, w_out,
              g_pre_mlp, g_post_mlp, w_mlp_up, w_mlp_down, w_in_diff,
              lambda_q1, lambda_k1, lambda_q2, lambda_k2, g_diff_subln, w_in_dsa):
    cos, sin = rope_tables(positions, HEAD_DIM)
    chunk_id = positions // CHUNK
    for i in range(DEPTH):
        j = i // 2
        h = rms_norm(x, g_pre_mix[i])
        if i % 2 == 0:
            o_seq, qm = diff_attention(h, w_in_diff[j], lambda_q1[j], lambda_k1[j],
                                       lambda_q2[j], lambda_k2[j], g_diff_subln[j],
                                       cos, sin, chunk_id, i)
        else:
            o_seq, qm = dsa_attention(h, w_in_dsa[j], cos, sin, chunk_id)
        o_mem = memory_attention(qm, rms_norm(mem, g_mem[i]), w_mem_kv[i])
        y = jnp.concatenate([o_seq, o_mem], axis=-1) @ w_out[i]
        x = x + rms_norm(y, g_post_mix[i])
        h = rms_norm(x, g_pre_mlp[i])
        y = jnp.square(jax.nn.relu(h @ w_mlp_up[i])) @ w_mlp_down[i]
        x = x + rms_norm(y, g_post_mlp[i])
    return x
```

```python
import functools
import math

import jax
import jax.numpy as jnp
from jax import lax
from jax.experimental import pallas as pl
from jax.experimental.pallas import tpu as pltpu

D_MODEL = 1024
CHUNK = 64
HEAD_DIM = 64
LANES = 128
ROPE_THETA = 10000.0
EPS = 1e-6
MEM_HEADS = 4
MEM_WIDTH = MEM_HEADS * HEAD_DIM
SEQ_WIDTH = D_MODEL - MEM_WIDTH
DIFF_HEADS = SEQ_WIDTH // (2 * HEAD_DIM)
DSA_HEADS = SEQ_WIDTH // HEAD_DIM
IDX_HEADS = 8
IDX_DIM = 64
DSA_TOPK_MAX = 256
D_FF = 4 * D_MODEL

NEG = -1e30
INT_MIN = -(2 ** 31)
MASKED_KEY = -2139095041
VMEM_LIMIT = 56 * 1024 * 1024

F32 = jnp.float32
BF16 = jnp.bfloat16
I32 = jnp.int32


def _dot(a, b):
    return jnp.dot(a, b, preferred_element_type=F32)


def _dot_nt(a, b):
    return lax.dot_general(a, b, (((1,), (1,)), ((), ())), preferred_element_type=F32)


def _rms(x, g):
    return x * lax.rsqrt(jnp.mean(x * x, axis=-1, keepdims=True) + EPS) * g


def _params(sem):
    return pltpu.CompilerParams(dimension_semantics=sem, vmem_limit_bytes=VMEM_LIMIT)


def _inproj_kernel(*refs, rope_outs, plain_outs):
    n_r, n_p = len(rope_outs), len(plain_outs)
    x_ref, g_ref = refs[0], refs[1]
    pos = 2
    if n_r:
        cos_ref, sin_ref, wr_ref = refs[2], refs[3], refs[4]
        pos = 5
    if n_p:
        wp_ref = refs[pos]
        pos += 1
    out_refs = refs[pos:]
    x = x_ref[...]
    h = _rms(x, g_ref[...]).astype(BF16)
    oi = 0
    if n_r:
        cos = cos_ref[...]
        sin = sin_ref[...]
        lane = lax.broadcasted_iota(I32, cos.shape, 1)
        low = (lane % HEAD_DIM) < (HEAD_DIM // 2)
        c0 = 0
        for width, scale in rope_outs:
            o_ref = out_refs[oi]
            oi += 1
            for s in range(width // LANES):
                y = _dot(h, wr_ref[:, c0 + s * LANES:c0 + (s + 1) * LANES])
                swapped = jnp.where(low, pltpu.roll(y, LANES - HEAD_DIM // 2, 1),
                                    pltpu.roll(y, HEAD_DIM // 2, 1))
                r = y * cos + swapped * sin
                if scale != 1.0:
                    r = r * scale
                o_ref[:, s * LANES:(s + 1) * LANES] = r.astype(o_ref.dtype)
            c0 += width
    c0 = 0
    for width, scale in plain_outs:
        o_ref = out_refs[oi]
        oi += 1
        y = _dot(h, wp_ref[:, c0:c0 + width])
        if scale != 1.0:
            y = y * scale
        o_ref[...] = y.astype(o_ref.dtype)
        c0 += width


def _inproj(x2d, g, cos, sin, w_rope, w_plain, rope_outs, plain_outs, tm):
    m = x2d.shape[0]
    row = lambda i: (i, 0)
    fixed = lambda i: (0, 0)
    in_specs = [pl.BlockSpec((tm, D_MODEL), row), pl.BlockSpec((1, D_MODEL), fixed)]
    args = [x2d, g.reshape(1, D_MODEL)]
    if rope_outs:
        in_specs += [pl.BlockSpec((tm, LANES), row), pl.BlockSpec((tm, LANES), row),
                     pl.BlockSpec(w_rope.shape, fixed)]
        args += [cos, sin, w_rope]
    if plain_outs:
        in_specs += [pl.BlockSpec(w_plain.shape, fixed)]
        args += [w_plain]
    outs = tuple(rope_outs) + tuple(plain_outs)
    out_shape = [jax.ShapeDtypeStruct((m, w), dt) for w, _, dt in outs]
    out_specs = [pl.BlockSpec((tm, w), row) for w, _, _ in outs]
    kern = functools.partial(_inproj_kernel,
                             rope_outs=tuple((w, s) for w, s, _ in rope_outs),
                             plain_outs=tuple((w, s) for w, s, _ in plain_outs))
    return pl.pallas_call(
        kern, grid=(m // tm,), in_specs=in_specs, out_specs=out_specs, out_shape=out_shape,
        compiler_params=_params(("parallel",)), name="inproj")(*args)


def _diff_attn_kernel(status_ref, nhi_ref, q_ref, k_ref, v_ref, cq_ref, ck_ref, lam_ref, gsub_ref,
                      o_ref, m_s, l_s, acc_s, *, tq, tk, nq, nk, lam_init):
    b = pl.program_id(0)
    i = pl.program_id(2)
    q = q_ref[...]
    lane = lax.broadcasted_iota(I32, q.shape, 1)
    zero = jnp.zeros_like(q)
    qs = (jnp.where(lane < HEAD_DIM, q, zero), jnp.where(lane >= HEAD_DIM, q, zero))
    cq = cq_ref[0]
    m_s[...] = jnp.full(m_s.shape, NEG, F32)
    l_s[...] = jnp.zeros(l_s.shape, F32)
    acc_s[...] = jnp.zeros(acc_s.shape, F32)
    base = (b * nq + i) * nk

    def tile(t, masked):
        off = pl.multiple_of(t * tk, tk)
        kt = k_ref[pl.ds(off, tk), :]
        vt = v_ref[pl.ds(off, tk), :]
        if masked:
            ck = ck_ref[0, :, pl.ds(off, tk)]
            bias = jnp.where(cq >= ck, 0.0, NEG)
        for j in range(2):
            s = _dot_nt(qs[j], kt)
            if masked:
                s = s + bias
            m_old = m_s[j]
            m_new = jnp.maximum(m_old, jnp.max(s, axis=1, keepdims=True))
            p = jnp.exp(s - m_new)
            alpha = jnp.exp(m_old - m_new)
            l_s[j] = alpha * l_s[j] + jnp.sum(p, axis=1, keepdims=True)
            acc_s[j] = alpha * acc_s[j] + _dot(p.astype(BF16), vt)
            m_s[j] = m_new

    def body(t, carry):
        st = status_ref[base + t]

        @pl.when(st == 1)
        def _():
            tile(t, False)

        @pl.when(st == 2)
        def _():
            tile(t, True)

        return carry

    lax.fori_loop(0, nhi_ref[b * nq + i], body, 0)

    lam_p = lam_ref[...]
    lam = (jnp.exp(jnp.sum(lam_p[0:1] * lam_p[1:2], axis=1, keepdims=True))
           - jnp.exp(jnp.sum(lam_p[2:3] * lam_p[3:4], axis=1, keepdims=True)) + lam_init)
    o = acc_s[0] / l_s[0] - lam * (acc_s[1] / l_s[1])
    o = _rms(o, gsub_ref[...]) * (1.0 - lam_init)
    o_ref[...] = o.astype(o_ref.dtype)


def _diff_attention(q, k, v, cq3, ck3, status, nhi, lam_p, gsub, b, s, tq, tk, lam_init):
    nq, nk = s // tq, s // tk
    kern = functools.partial(_diff_attn_kernel, tq=tq, tk=tk, nq=nq, nk=nk, lam_init=lam_init)
    grid_spec = pltpu.PrefetchScalarGridSpec(
        num_scalar_prefetch=2,
        grid=(b, DIFF_HEADS, nq),
        in_specs=[
            pl.BlockSpec((tq, LANES), lambda bb, h, i, *_: (bb * nq + i, h)),
            pl.BlockSpec((s, LANES), lambda bb, h, i, *_: (bb, h)),
            pl.BlockSpec((s, LANES), lambda bb, h, i, *_: (bb, h)),
            pl.BlockSpec((1, tq, 1), lambda bb, h, i, *_: (bb * nq + i, 0, 0)),
            pl.BlockSpec((1, 1, s), lambda bb, h, i, *_: (bb, 0, 0)),
            pl.BlockSpec((4, HEAD_DIM), lambda bb, h, i, *_: (0, 0)),
            pl.BlockSpec((1, LANES), lambda bb, h, i, *_: (0, 0)),
        ],
        out_specs=pl.BlockSpec((tq, LANES), lambda bb, h, i, *_: (bb * nq + i, h)),
        scratch_shapes=[pltpu.VMEM((2, tq, 1), F32), pltpu.VMEM((2, tq, 1), F32),
                        pltpu.VMEM((2, tq, LANES), F32)],
    )
    return pl.pallas_call(
        kern, grid_spec=grid_spec,
        out_shape=jax.ShapeDtypeStruct((b * s, SEQ_WIDTH), BF16),
        compiler_params=_params(("parallel", "parallel", "arbitrary")), name="diff_attn",
    )(status, nhi, q, k, v, cq3, ck3, lam_p, gsub)


def _dsa_kernel(status_ref, nhi_ref, q_ref, iq_ref, iw_ref, kk_ref, vv_ref, ikk_ref, cq_ref, ck_ref,
                o_ref, keys_s, qst_s, iqst_s, iwb_s, p_s, m_s, l_s, acc_s, thr_s, jst_s,
                *, tq, tk, nq, nk, s_len, top_k):
    b = pl.program_id(0)
    i = pl.program_id(1)
    base = (b * nq + i) * nk
    n_hi = nhi_ref[b * nq + i]
    cq = cq_ref[0]
    lane = lax.broadcasted_iota(I32, (tq, LANES), 1)
    first = lane < HEAD_DIM

    for h in range(DSA_HEADS):
        slab = q_ref[:, (h // 2) * LANES:(h // 2 + 1) * LANES]
        keep = first if h % 2 == 0 else jnp.logical_not(first)
        qst_s[h * tq:(h + 1) * tq, :] = jnp.where(keep, slab, jnp.zeros_like(slab))
    for h in range(IDX_HEADS):
        slab = iq_ref[:, (h // 2) * LANES:(h // 2 + 1) * LANES]
        keep = first if h % 2 == 0 else jnp.logical_not(first)
        iqst_s[h * tq:(h + 1) * tq, :] = jnp.where(keep, slab, jnp.zeros_like(slab))
        iwb_s[h * tq:(h + 1) * tq, :] = jnp.broadcast_to(iw_ref[:, h:h + 1], (tq, LANES))

    def score_body(t, carry):
        st = status_ref[base + t]
        off = pl.multiple_of(t * tk, tk)

        @pl.when(st == 0)
        def _():
            keys_s[:, pl.ds(off, tk)] = jnp.full((tq, tk), MASKED_KEY, I32)

        def scores(masked):
            ikt = ikk_ref[pl.ds(off, tk), :]
            logits = _dot_nt(iqst_s[...], ikt)
            iwb = iwb_s[...]
            iwt = jnp.concatenate([iwb] * (tk // LANES), axis=1)
            r = jnp.maximum(logits, 0.0) * iwt
            sc = r[0:tq]
            for h in range(1, IDX_HEADS):
                sc = sc + r[h * tq:(h + 1) * tq]
            bits = lax.bitcast_convert_type(sc, I32)
            key = jnp.where(bits < 0, bits ^ jnp.int32(0x7FFFFFFF), bits)
            if masked:
                ck = ck_ref[0, :, pl.ds(off, tk)]
                key = jnp.where(cq >= ck, key, jnp.int32(MASKED_KEY))
            keys_s[:, pl.ds(off, tk)] = key

        @pl.when(st == 1)
        def _():
            scores(False)

        @pl.when(st == 2)
        def _():
            scores(True)

        return carry

    lax.fori_loop(0, n_hi, score_body, 0)

    def count_tiles(pred):
        def body(t, acc):
            off = pl.multiple_of(t * tk, tk)
            for c in range(tk // LANES):
                slab = keys_s[:, pl.ds(off + c * LANES, LANES)]
                acc = acc + jnp.where(pred(slab, off + c * LANES), 1, 0).astype(I32)
            return acc
        acc = lax.fori_loop(0, n_hi, body, jnp.zeros((tq, LANES), I32))
        return jnp.sum(acc, axis=1, keepdims=True)

    def bit_body(j, thr):
        cand = thr + lax.shift_left(jnp.int32(1), 31 - j)
        cnt = count_tiles(lambda slab, col0: slab >= cand)
        return jnp.where(cnt >= top_k, cand, thr)

    thr = lax.fori_loop(0, 32, bit_body, jnp.full((tq, 1), INT_MIN, I32))
    cnt_gt = count_tiles(lambda slab, col0: slab > thr)
    cnt_eq = count_tiles(lambda slab, col0: slab == thr)
    need = top_k - cnt_gt
    real = thr > MASKED_KEY
    excess = jnp.logical_and(real, cnt_eq > need)
    thr_s[...] = thr
    jst_s[...] = jnp.where(real, jnp.int32(s_len), jnp.int32(-1))

    @pl.when(jnp.max(excess.astype(I32)) > 0)
    def _():
        def idx_body(j, jcur):
            cand = jcur + lax.shift_left(jnp.int32(1), (s_len.bit_length() - 2) - j)

            def pred(slab, col0):
                idx = col0 + lane
                return jnp.logical_and(slab == thr, idx < cand)
            cnt = count_tiles(pred)
            return jnp.where(cnt < need, cand, jcur)
        jbest = lax.fori_loop(0, s_len.bit_length() - 1, idx_body, jnp.zeros((tq, 1), I32))
        jst_s[...] = jnp.where(excess, jbest, jst_s[...])

    m_s[...] = jnp.full(m_s.shape, NEG, F32)
    l_s[...] = jnp.zeros(l_s.shape, F32)
    acc_s[...] = jnp.zeros(acc_s.shape, F32)
    lane_k = lax.broadcasted_iota(I32, (tq, tk), 1)

    def attn_body(t, carry):
        st = status_ref[base + t]

        @pl.when(st != 0)
        def _():
            off = pl.multiple_of(t * tk, tk)
            key = keys_s[:, pl.ds(off, tk)]
            th = thr_s[...]
            sel = jnp.logical_or(key > th,
                                 jnp.logical_and(key == th, (lane_k + off) <= jst_s[...]))
            bias = jnp.where(sel, 0.0, NEG)
            kt = kk_ref[pl.ds(off, tk), :]
            vt = vv_ref[pl.ds(off, tk), :]
            s_all = _dot_nt(qst_s[...], kt)
            for h in range(DSA_HEADS):
                rows = slice(h * tq, (h + 1) * tq)
                s = s_all[rows] + bias
                m_old = m_s[rows]
                m_new = jnp.maximum(m_old, jnp.max(s, axis=1, keepdims=True))
                p = jnp.exp(s - m_new)
                alpha = jnp.exp(m_old - m_new)
                l_s[rows] = alpha * l_s[rows] + jnp.sum(p, axis=1, keepdims=True)
                acc_s[rows] = alpha * acc_s[rows]
                p_s[rows] = p.astype(BF16)
                m_s[rows] = m_new
            acc_s[...] = acc_s[...] + _dot(p_s[...], vt)

        return carry

    lax.fori_loop(0, n_hi, attn_body, 0)

    for j in range(DSA_HEADS // 2):
        ra = slice(2 * j * tq, (2 * j + 1) * tq)
        rb = slice((2 * j + 1) * tq, (2 * j + 2) * tq)
        oa = acc_s[ra] / l_s[ra]
        ob = acc_s[rb] / l_s[rb]
        o_ref[:, j * LANES:(j + 1) * LANES] = jnp.where(first, oa, ob).astype(o_ref.dtype)


def _dsa_attention(q, iq, iw, kk, vv, ikk, cq3, ck3, status, nhi, b, s, tq, tk, top_k):
    nq, nk = s // tq, s // tk
    kern = functools.partial(_dsa_kernel, tq=tq, tk=tk, nq=nq, nk=nk, s_len=s, top_k=top_k)
    rowblk = lambda bb, i, *_: (bb * nq + i, 0)
    perb = lambda bb, i, *_: (bb, 0)
    grid_spec = pltpu.PrefetchScalarGridSpec(
        num_scalar_prefetch=2,
        grid=(b, nq),
        in_specs=[
            pl.BlockSpec((tq, SEQ_WIDTH), rowblk),
            pl.BlockSpec((tq, IDX_HEADS * IDX_DIM), rowblk),
            pl.BlockSpec((tq, LANES), rowblk),
            pl.BlockSpec((s, LANES), perb),
            pl.BlockSpec((s, LANES), perb),
            pl.BlockSpec((s, LANES), perb),
            pl.BlockSpec((1, tq, 1), lambda bb, i, *_: (bb * nq + i, 0, 0)),
            pl.BlockSpec((1, 1, s), lambda bb, i, *_: (bb, 0, 0)),
        ],
        out_specs=pl.BlockSpec((tq, SEQ_WIDTH), rowblk),
        scratch_shapes=[
            pltpu.VMEM((tq, s), I32),
            pltpu.VMEM((DSA_HEADS * tq, LANES), BF16),
            pltpu.VMEM((IDX_HEADS * tq, LANES), BF16),
            pltpu.VMEM((IDX_HEADS * tq, LANES), F32),
            pltpu.VMEM((DSA_HEADS * tq, tk), BF16),
            pltpu.VMEM((DSA_HEADS * tq, 1), F32),
            pltpu.VMEM((DSA_HEADS * tq, 1), F32),
            pltpu.VMEM((DSA_HEADS * tq, LANES), F32),
            pltpu.VMEM((tq, 1), I32),
            pltpu.VMEM((tq, 1), I32),
        ],
    )
    return pl.pallas_call(
        kern, grid_spec=grid_spec,
        out_shape=jax.ShapeDtypeStruct((b * s, SEQ_WIDTH), BF16),
        compiler_params=_params(("parallel", "arbitrary")), name="dsa_attn",
    )(status, nhi, q, iq, iw, kk, vv, ikk, cq3, ck3)


def _mix_out_kernel(x_ref, oseq_ref, qm_ref, km_ref, vm_ref, wo_ref, g_ref, o_ref):
    qm = qm_ref[...]
    km = km_ref[...]
    vm = vm_ref[...]
    lane = lax.broadcasted_iota(I32, qm.shape, 1)
    o_mem = jnp.zeros(qm.shape, F32)
    for h in range(MEM_HEADS):
        mine = (lane // HEAD_DIM) == h
        s = _dot_nt(jnp.where(mine, qm, jnp.zeros_like(qm)), km)
        p = jnp.exp(s - jnp.max(s, axis=1, keepdims=True))
        p = p / jnp.sum(p, axis=1, keepdims=True)
        o_mem = o_mem + jnp.where(mine, _dot(p.astype(BF16), vm), 0.0)
    y = _dot(oseq_ref[...], wo_ref[0:SEQ_WIDTH, :]) + _dot(o_mem.astype(BF16), wo_ref[SEQ_WIDTH:, :])
    o_ref[...] = x_ref[...] + _rms(y, g_ref[...])


def _mix_out(x2d, oseq, qm, kv, w_out, g, b, s, n_mem, tm):
    nt = s // tm
    return pl.pallas_call(
        _mix_out_kernel, grid=(b, nt),
        in_specs=[
            pl.BlockSpec((tm, D_MODEL), lambda bb, i: (bb * nt + i, 0)),
            pl.BlockSpec((tm, SEQ_WIDTH), lambda bb, i: (bb * nt + i, 0)),
            pl.BlockSpec((tm, MEM_WIDTH), lambda bb, i: (bb * nt + i, 0)),
            pl.BlockSpec((n_mem, MEM_WIDTH), lambda bb, i: (bb, 0)),
            pl.BlockSpec((n_mem, MEM_WIDTH), lambda bb, i: (bb, 1)),
            pl.BlockSpec((D_MODEL, D_MODEL), lambda bb, i: (0, 0)),
            pl.BlockSpec((1, D_MODEL), lambda bb, i: (0, 0)),
        ],
        out_specs=pl.BlockSpec((tm, D_MODEL), lambda bb, i: (bb * nt + i, 0)),
        out_shape=jax.ShapeDtypeStruct(x2d.shape, F32),
        compiler_params=_params(("parallel", "parallel")), name="mix_out",
    )(x2d, oseq, qm, kv, kv, w_out, g.reshape(1, D_MODEL))


def _mlp_kernel(x_ref, gpre_ref, wup_ref, wdn_ref, gpost_ref, o_ref, h_s, acc_s):
    j = pl.program_id(1)

    @pl.when(j == 0)
    def _():
        h_s[...] = _rms(x_ref[...], gpre_ref[...]).astype(BF16)
        acc_s[...] = jnp.zeros(acc_s.shape, F32)

    u = jnp.maximum(_dot(h_s[...], wup_ref[...]), 0.0)
    acc_s[...] += _dot((u * u).astype(BF16), wdn_ref[...])

    @pl.when(j == pl.num_programs(1) - 1)
    def _():
        o_ref[...] = x_ref[...] + _rms(acc_s[...], gpost_ref[...])


def _mlp(x2d, g_pre, w_up, w_dn, g_post, tm, tf):
    m = x2d.shape[0]
    return pl.pallas_call(
        _mlp_kernel, grid=(m // tm, D_FF // tf),
        in_specs=[
            pl.BlockSpec((tm, D_MODEL), lambda i, j: (i, 0)),
            pl.BlockSpec((1, D_MODEL), lambda i, j: (0, 0)),
            pl.BlockSpec((D_MODEL, tf), lambda i, j: (0, j)),
            pl.BlockSpec((tf, D_MODEL), lambda i, j: (j, 0)),
            pl.BlockSpec((1, D_MODEL), lambda i, j: (0, 0)),
        ],
        out_specs=pl.BlockSpec((tm, D_MODEL), lambda i, j: (i, 0)),
        out_shape=jax.ShapeDtypeStruct(x2d.shape, F32),
        scratch_shapes=[pltpu.VMEM((tm, D_MODEL), BF16), pltpu.VMEM((tm, D_MODEL), F32)],
        compiler_params=_params(("parallel", "arbitrary")), name="mlp",
    )(x2d, g_pre.reshape(1, D_MODEL), w_up, w_dn, g_post.reshape(1, D_MODEL))


def _rope_tables(positions):
    half = HEAD_DIM // 2
    inv = ROPE_THETA ** (-jnp.arange(0, HEAD_DIM, 2, dtype=F32) / HEAD_DIM)
    ang = positions.astype(F32).reshape(-1, 1) * inv
    cos, sin = jnp.cos(ang), jnp.sin(ang)
    reps = LANES // HEAD_DIM
    cos_t = jnp.tile(cos, (1, 2 * reps))
    sin_t = jnp.tile(jnp.concatenate([-sin, sin], axis=1), (1, reps))
    assert cos_t.shape[1] == LANES and half * 2 == HEAD_DIM
    return cos_t, sin_t


def _visibility(chunk_id, tq, tk):
    b, s = chunk_id.shape
    cq = chunk_id.reshape(b, s // tq, tq)
    ck = chunk_id.reshape(b, s // tk, tk)
    qmin, qmax = cq.min(-1)[:, :, None], cq.max(-1)[:, :, None]
    kmin, kmax = ck.min(-1)[:, None, :], ck.max(-1)[:, None, :]
    status = jnp.where(kmin > qmax, 0, jnp.where(kmax <= qmin, 1, 2)).astype(I32)
    tiles = jnp.arange(s // tk, dtype=I32)[None, None, :]
    nhi = jnp.max(jnp.where(status != 0, tiles + 1, 0), axis=-1).astype(I32)
    return status.reshape(-1), nhi.reshape(-1)


def _dup(w):
    return jnp.concatenate([w, w], axis=1)


def kernel(x, mem, positions, g_pre_mix, g_post_mix, g_mem, w_mem_kv, w_out, g_pre_mlp, g_post_mlp,
           w_mlp_up, w_mlp_down, w_in_diff, lambda_q1, lambda_k1, lambda_q2, lambda_k2, g_diff_subln,
           w_in_dsa):
    b, s, d = x.shape
    n_mem = mem.shape[1]
    depth = g_pre_mix.shape[0]
    m = b * s
    scale = HEAD_DIM ** -0.5
    tm_proj = min(512, s)
    tq_diff = tk_diff = min(256, s)
    tq_dsa, tk_dsa = min(128, s), min(256, s)
    tm_mix = min(512, s)
    tm_mlp, tf_mlp = min(1024, m), 1024
    top_k = min(DSA_TOPK_MAX, s // 4)
    assert tk_dsa >= top_k, "the threshold search needs one key tile to hold top_k candidates"

    cos_t, sin_t = _rope_tables(positions)
    chunk_id = positions // CHUNK
    cq3 = chunk_id.reshape(-1, 1)
    ck3 = chunk_id.reshape(b, 1, s)
    vis_diff = _visibility(chunk_id, tq_diff, tk_diff)
    vis_dsa = _visibility(chunk_id, tq_dsa, tk_dsa)

    x2d = x.reshape(m, d)
    mem2d = mem.reshape(b * n_mem, d)
    for i in range(depth):
        j = i // 2
        kv = _inproj(mem2d, g_mem[i], None, None, None, w_mem_kv[i].astype(BF16), (),
                     ((2 * MEM_WIDTH, 1.0, BF16),), min(256, b * n_mem))[0]
        if i % 2 == 0:
            w = w_in_diff[j].astype(BF16)
            nqk = 2 * DIFF_HEADS * HEAD_DIM
            q, k, v, qm = _inproj(
                x2d, g_pre_mix[i], cos_t, sin_t, w[:, :2 * nqk], w[:, 2 * nqk:],
                ((nqk, scale, BF16), (nqk, 1.0, BF16)),
                ((SEQ_WIDTH, 1.0, BF16), (MEM_WIDTH, scale, BF16)), tm_proj)
            lam_p = jnp.stack([lambda_q1[j], lambda_k1[j], lambda_q2[j], lambda_k2[j]]).astype(F32)
            lam_init = 0.8 - 0.6 * math.exp(-0.3 * i)
            o_seq = _diff_attention(
                q, k, v, cq3.reshape(m // tq_diff, tq_diff, 1), ck3, vis_diff[0], vis_diff[1],
                lam_p, g_diff_subln[j].reshape(1, LANES), b, s, tq_diff, tk_diff, lam_init)
        else:
            w = w_in_dsa[j]
            o0 = 0
            parts = []
            for width in (DSA_HEADS * HEAD_DIM, HEAD_DIM, HEAD_DIM, IDX_HEADS * IDX_DIM, IDX_DIM,
                          IDX_HEADS, MEM_WIDTH):
                parts.append(w[:, o0:o0 + width])
                o0 += width
            wq, wk, wv, wiq, wik, wiw, wqm = parts
            wiw = jnp.pad(wiw, ((0, 0), (0, LANES - IDX_HEADS)))
            w_rope = jnp.concatenate([wq, wiq, _dup(wk), _dup(wik)], axis=1).astype(BF16)
            w_plain = jnp.concatenate([_dup(wv), wiw, wqm], axis=1).astype(BF16)
            iw_scale = IDX_HEADS ** -0.5 * IDX_DIM ** -0.5
            q, iq, kk, ikk, vv, iw, qm = _inproj(
                x2d, g_pre_mix[i], cos_t, sin_t, w_rope, w_plain,
                ((DSA_HEADS * HEAD_DIM, scale, BF16), (IDX_HEADS * IDX_DIM, 1.0, BF16),
                 (LANES, 1.0, BF16), (LANES, 1.0, BF16)),
                ((LANES, 1.0, BF16), (LANES, iw_scale, F32), (MEM_WIDTH, scale, BF16)), tm_proj)
            o_seq = _dsa_attention(
                q, iq, iw, kk, vv, ikk, cq3.reshape(m // tq_dsa, tq_dsa, 1), ck3,
                vis_dsa[0], vis_dsa[1], b, s, tq_dsa, tk_dsa, top_k)
        x2d = _mix_out(x2d, o_seq, qm, kv, w_out[i].astype(BF16), g_post_mix[i], b, s, n_mem, tm_mix)
        x2d = _mlp(x2d, g_pre_mlp[i], w_mlp_up[i].astype(BF16), w_mlp_down[i].astype(BF16),
                   g_post_mlp[i], tm_mlp, tf_mlp)
    return x2d.reshape(b, s, d)
```

```python
import functools
import math

import jax
import jax.numpy as jnp
from jax import lax
from jax.experimental import pallas as pl
from jax.experimental.pallas import tpu as pltpu

D_MODEL = 1024
CHUNK = 64
HEAD_DIM = 64
LANES = 128
ROPE_THETA = 10000.0
EPS = 1e-6
MEM_HEADS = 4
MEM_WIDTH = MEM_HEADS * HEAD_DIM
SEQ_WIDTH = D_MODEL - MEM_WIDTH
DIFF_HEADS = SEQ_WIDTH // (2 * HEAD_DIM)
DSA_HEADS = SEQ_WIDTH // HEAD_DIM
IDX_HEADS = 8
IDX_DIM = 64
DSA_TOPK_MAX = 256
D_FF = 4 * D_MODEL

NEG = -1e30
INT_MIN = -(2 ** 31)
MASKED_KEY = -2139095041
VMEM_LIMIT = 56 * 1024 * 1024

F32 = jnp.float32
BF16 = jnp.bfloat16
I32 = jnp.int32


def _dot(a, b):
    return jnp.dot(a, b, preferred_element_type=F32)


def _dot_nt(a, b):
    return lax.dot_general(a, b, (((1,), (1,)), ((), ())), preferred_element_type=F32)


def _rms(x, g):
    return x * lax.rsqrt(jnp.mean(x * x, axis=-1, keepdims=True) + EPS) * g


def _params(sem):
    return pltpu.CompilerParams(dimension_semantics=sem, vmem_limit_bytes=VMEM_LIMIT)


def _inproj_kernel(*refs, rope_outs, plain_outs):
    n_r, n_p = len(rope_outs), len(plain_outs)
    x_ref, g_ref = refs[0], refs[1]
    pos = 2
    if n_r:
        cos_ref, sin_ref, wr_ref = refs[2], refs[3], refs[4]
        pos = 5
    if n_p:
        wp_ref = refs[pos]
        pos += 1
    out_refs = refs[pos:]
    x = x_ref[...]
    h = _rms(x, g_ref[...]).astype(BF16)
    oi = 0
    if n_r:
        cos = cos_ref[...]
        sin = sin_ref[...]
        lane = lax.broadcasted_iota(I32, cos.shape, 1)
        low = (lane % HEAD_DIM) < (HEAD_DIM // 2)
        c0 = 0
        for width, scale in rope_outs:
            o_ref = out_refs[oi]
            oi += 1
            for s in range(width // LANES):
                y = _dot(h, wr_ref[:, c0 + s * LANES:c0 + (s + 1) * LANES])
                swapped = jnp.where(low, pltpu.roll(y, LANES - HEAD_DIM // 2, 1),
                                    pltpu.roll(y, HEAD_DIM // 2, 1))
                r = y * cos + swapped * sin
                if scale != 1.0:
                    r = r * scale
                o_ref[:, s * LANES:(s + 1) * LANES] = r.astype(o_ref.dtype)
            c0 += width
    c0 = 0
    for width, scale in plain_outs:
        o_ref = out_refs[oi]
        oi += 1
        y = _dot(h, wp_ref[:, c0:c0 + width])
        if scale != 1.0:
            y = y * scale
        o_ref[...] = y.astype(o_ref.dtype)
        c0 += width


def _inproj(x2d, g, cos, sin, w_rope, w_plain, rope_outs, plain_outs, tm):
    m = x2d.shape[0]
    row = lambda i: (i, 0)
    fixed = lambda i: (0, 0)
    in_specs = [pl.BlockSpec((tm, D_MODEL), row), pl.BlockSpec((1, D_MODEL), fixed)]
    args = [x2d, g.reshape(1, D_MODEL)]
    if rope_outs:
        in_specs += [pl.BlockSpec((tm, LANES), row), pl.BlockSpec((tm, LANES), row),
                     pl.BlockSpec(w_rope.shape, fixed)]
        args += [cos, sin, w_rope]
    if plain_outs:
        in_specs += [pl.BlockSpec(w_plain.shape, fixed)]
        args += [w_plain]
    outs = tuple(rope_outs) + tuple(plain_outs)
    out_shape = [jax.ShapeDtypeStruct((m, w), dt) for w, _, dt in outs]
    out_specs = [pl.BlockSpec((tm, w), row) for w, _, _ in outs]
    kern = functools.partial(_inproj_kernel,
                             rope_outs=tuple((w, s) for w, s, _ in rope_outs),
                             plain_outs=tuple((w, s) for w, s, _ in plain_outs))
    return pl.pallas_call(
        kern, grid=(m // tm,), in_specs=in_specs, out_specs=out_specs, out_shape=out_shape,
        compiler_params=_params(("parallel",)), name="inproj")(*args)


def _diff_attn_kernel(status_ref, nhi_ref, q_ref, k_ref, v_ref, cq_ref, ck_ref, lam_ref, gsub_ref,
                      o_ref, m_s, l_s, acc_s, *, tq, tk, nq, nk, lam_init):
    b = pl.program_id(0)
    i = pl.program_id(2)
    q = q_ref[...]
    lane = lax.broadcasted_iota(I32, q.shape, 1)
    zero = jnp.zeros_like(q)
    qs = (jnp.where(lane < HEAD_DIM, q, zero), jnp.where(lane >= HEAD_DIM, q, zero))
    cq = cq_ref[0]
    m_s[...] = jnp.full(m_s.shape, NEG, F32)
    l_s[...] = jnp.zeros(l_s.shape, F32)
    acc_s[...] = jnp.zeros(acc_s.shape, F32)
    base = (b * nq + i) * nk

    def tile(t, masked):
        off = pl.multiple_of(t * tk, tk)
        kt = k_ref[pl.ds(off, tk), :]
        vt = v_ref[pl.ds(off, tk), :]
        s = [_dot_nt(qs[0], kt), _dot_nt(qs[1], kt)]
        if masked:
            ck = ck_ref[0, :, pl.ds(off, tk)]
            bias = jnp.where(cq >= ck, 0.0, NEG)
            s = [s[0] + bias, s[1] + bias]
        for j in range(2):
            m_old = m_s[j]
            m_new = jnp.maximum(m_old, jnp.max(s[j], axis=1, keepdims=True))
            p = jnp.exp2(s[j] - pltpu.repeat(m_new, tk // LANES, axis=1))
            alpha = jnp.exp2(m_old - m_new)
            l_s[j] = alpha * l_s[j] + jnp.sum(p, axis=1, keepdims=True)
            acc_s[j] = alpha * acc_s[j] + _dot(p.astype(BF16), vt)
            m_s[j] = m_new

    def body(t, carry):
        st = status_ref[base + t]

        @pl.when(st == 1)
        def _():
            tile(t, False)

        @pl.when(st == 2)
        def _():
            tile(t, True)

        return carry

    lax.fori_loop(0, nhi_ref[b * nq + i], body, 0)

    lam_p = lam_ref[...]
    lam = (jnp.exp(jnp.sum(lam_p[0:1] * lam_p[1:2], axis=1, keepdims=True))
           - jnp.exp(jnp.sum(lam_p[2:3] * lam_p[3:4], axis=1, keepdims=True)) + lam_init)
    o = acc_s[0] / l_s[0] - lam * (acc_s[1] / l_s[1])
    o = _rms(o, gsub_ref[...]) * (1.0 - lam_init)
    o_ref[...] = o.astype(o_ref.dtype)


def _diff_attention(q, k, v, cq3, ck3, status, nhi, lam_p, gsub, b, s, tq, tk, lam_init):
    nq, nk = s // tq, s // tk
    kern = functools.partial(_diff_attn_kernel, tq=tq, tk=tk, nq=nq, nk=nk, lam_init=lam_init)
    grid_spec = pltpu.PrefetchScalarGridSpec(
        num_scalar_prefetch=2,
        grid=(b, DIFF_HEADS, nq),
        in_specs=[
            pl.BlockSpec((tq, LANES), lambda bb, h, i, *_: (bb * nq + i, h)),
            pl.BlockSpec((s, LANES), lambda bb, h, i, *_: (bb, h)),
            pl.BlockSpec((s, LANES), lambda bb, h, i, *_: (bb, h)),
            pl.BlockSpec((1, tq, 1), lambda bb, h, i, *_: (bb * nq + i, 0, 0)),
            pl.BlockSpec((1, 1, s), lambda bb, h, i, *_: (bb, 0, 0)),
            pl.BlockSpec((4, HEAD_DIM), lambda bb, h, i, *_: (0, 0)),
            pl.BlockSpec((1, LANES), lambda bb, h, i, *_: (0, 0)),
        ],
        out_specs=pl.BlockSpec((tq, LANES), lambda bb, h, i, *_: (bb * nq + i, h)),
        scratch_shapes=[pltpu.VMEM((2, tq, LANES), F32), pltpu.VMEM((2, tq, LANES), F32),
                        pltpu.VMEM((2, tq, LANES), F32)],
    )
    return pl.pallas_call(
        kern, grid_spec=grid_spec,
        out_shape=jax.ShapeDtypeStruct((b * s, SEQ_WIDTH), BF16),
        compiler_params=_params(("parallel", "parallel", "arbitrary")), name="diff_attn",
    )(status, nhi, q, k, v, cq3, ck3, lam_p, gsub)


def _dsa_kernel(status_ref, nhi_ref, q_ref, iq_ref, iw_ref, kk_ref, vv_ref, ikk_ref, cq_ref, ck_ref,
                o_ref, keys_s, qst_s, iqst_s, iwb_s, m_s, acc_s, thr_s, jst_s,
                *, tq, tk, nq, nk, s_len, top_k):
    b = pl.program_id(0)
    i = pl.program_id(1)
    base = (b * nq + i) * nk
    n_hi = nhi_ref[b * nq + i]
    cq = cq_ref[0]
    lane = lax.broadcasted_iota(I32, (tq, LANES), 1)
    first = lane < HEAD_DIM

    for h in range(DSA_HEADS):
        slab = q_ref[:, (h // 2) * LANES:(h // 2 + 1) * LANES]
        keep = first if h % 2 == 0 else jnp.logical_not(first)
        qst_s[h * tq:(h + 1) * tq, :] = jnp.where(keep, slab, jnp.zeros_like(slab))
    for h in range(IDX_HEADS):
        slab = iq_ref[:, (h // 2) * LANES:(h // 2 + 1) * LANES]
        keep = first if h % 2 == 0 else jnp.logical_not(first)
        iqst_s[h * tq:(h + 1) * tq, :] = jnp.where(keep, slab, jnp.zeros_like(slab))
        iwb_s[h * tq:(h + 1) * tq, :] = jnp.broadcast_to(iw_ref[:, h:h + 1], (tq, LANES))

    def score_body(t, carry):
        st = status_ref[base + t]
        off = pl.multiple_of(t * tk, tk)

        @pl.when(st == 0)
        def _():
            keys_s[:, pl.ds(off, tk)] = jnp.full((tq, tk), MASKED_KEY, I32)

        def scores(masked):
            ikt = ikk_ref[pl.ds(off, tk), :]
            logits = _dot_nt(iqst_s[...], ikt)
            iwb = iwb_s[...]
            iwt = jnp.concatenate([iwb] * (tk // LANES), axis=1)
            r = jnp.maximum(logits, 0.0) * iwt
            sc = r[0:tq]
            for h in range(1, IDX_HEADS):
                sc = sc + r[h * tq:(h + 1) * tq]
            bits = lax.bitcast_convert_type(sc, I32)
            key = jnp.where(bits < 0, bits ^ jnp.int32(0x7FFFFFFF), bits)
            if masked:
                ck = ck_ref[0, :, pl.ds(off, tk)]
                key = jnp.where(cq >= ck, key, jnp.int32(MASKED_KEY))
            keys_s[:, pl.ds(off, tk)] = key

        @pl.when(st == 1)
        def _():
            scores(False)

        @pl.when(st == 2)
        def _():
            scores(True)

        return carry

    lax.fori_loop(0, n_hi, score_body, 0)

    def count_tiles(pred):
        def body(t, acc):
            off = pl.multiple_of(t * tk, tk)
            for c in range(tk // LANES):
                slab = keys_s[:, pl.ds(off + c * LANES, LANES)]
                acc = acc + jnp.where(pred(slab, off + c * LANES), 1, 0).astype(I32)
            return acc
        acc = lax.fori_loop(0, n_hi, body, jnp.zeros((tq, LANES), I32))
        return jnp.broadcast_to(jnp.sum(acc, axis=1, keepdims=True), (tq, LANES))

    def bit_body(j, thr):
        cand = thr + lax.shift_left(jnp.int32(1), 31 - j)
        cnt = count_tiles(lambda slab, col0: slab >= cand)
        return jnp.where(cnt >= top_k, cand, thr)

    thr = lax.fori_loop(0, 32, bit_body, jnp.full((tq, LANES), INT_MIN, I32))
    cnt_gt = count_tiles(lambda slab, col0: slab > thr)
    cnt_eq = count_tiles(lambda slab, col0: slab == thr)
    need = top_k - cnt_gt
    real = thr > MASKED_KEY
    excess = jnp.logical_and(real, cnt_eq > need)
    thr_s[...] = thr
    jst_s[...] = jnp.where(real, jnp.int32(s_len), jnp.int32(-1))

    @pl.when(jnp.max(excess.astype(I32)) > 0)
    def _():
        def idx_body(j, jcur):
            cand = jcur + lax.shift_left(jnp.int32(1), (s_len.bit_length() - 2) - j)

            def pred(slab, col0):
                idx = col0 + lane
                return jnp.logical_and(slab == thr, idx < cand)
            cnt = count_tiles(pred)
            return jnp.where(cnt < need, cand, jcur)
        jbest = lax.fori_loop(0, s_len.bit_length() - 1, idx_body, jnp.zeros((tq, LANES), I32))
        jst_s[...] = jnp.where(excess, jbest, jst_s[...])

    m_s[...] = jnp.full(m_s.shape, NEG, F32)
    acc_s[...] = jnp.zeros(acc_s.shape, F32)
    lane_k = lax.broadcasted_iota(I32, (tq, tk), 1)
    lane_v = lax.broadcasted_iota(I32, (tk, LANES), 1)

    def attn_body(t, carry):
        st = status_ref[base + t]

        @pl.when(st != 0)
        def _():
            off = pl.multiple_of(t * tk, tk)
            key = keys_s[:, pl.ds(off, tk)]
            th = pltpu.repeat(thr_s[...], tk // LANES, axis=1)
            jst = pltpu.repeat(jst_s[...], tk // LANES, axis=1)
            sel = jnp.logical_or(key > th, jnp.logical_and(key == th, (lane_k + off) <= jst))
            bias = jnp.where(sel, 0.0, NEG)
            kt = kk_ref[pl.ds(off, tk), :]
            vt = vv_ref[pl.ds(off, tk), :]
            vt = jnp.where(lane_v < HEAD_DIM, vt, jnp.ones_like(vt))
            s = _dot_nt(qst_s[...], kt)
            s = (s.reshape(DSA_HEADS, tq, tk) + bias[None]).reshape(DSA_HEADS * tq, tk)
            m_old = m_s[...]
            m_new = jnp.maximum(m_old, jnp.max(s, axis=1, keepdims=True))
            p = jnp.exp2(s - pltpu.repeat(m_new, tk // LANES, axis=1))
            alpha = jnp.exp2(m_old - m_new)
            acc_s[...] = alpha * acc_s[...] + _dot(p.astype(BF16), vt)
            m_s[...] = m_new

        return carry

    lax.fori_loop(0, n_hi, attn_body, 0)

    for j in range(DSA_HEADS // 2):
        acc_a = acc_s[2 * j * tq:(2 * j + 1) * tq]
        acc_b = acc_s[(2 * j + 1) * tq:(2 * j + 2) * tq]
        swap_a = pltpu.roll(acc_a, HEAD_DIM, 1)
        swap_b = pltpu.roll(acc_b, HEAD_DIM, 1)
        o_ref[:, j * LANES:(j + 1) * LANES] = jnp.where(first, acc_a / swap_a,
                                                        swap_b / acc_b).astype(o_ref.dtype)


def _dsa_attention(q, iq, iw, kk, vv, ikk, cq3, ck3, status, nhi, b, s, tq, tk, top_k):
    nq, nk = s // tq, s // tk
    kern = functools.partial(_dsa_kernel, tq=tq, tk=tk, nq=nq, nk=nk, s_len=s, top_k=top_k)
    rowblk = lambda bb, i, *_: (bb * nq + i, 0)
    perb = lambda bb, i, *_: (bb, 0)
    grid_spec = pltpu.PrefetchScalarGridSpec(
        num_scalar_prefetch=2,
        grid=(b, nq),
        in_specs=[
            pl.BlockSpec((tq, SEQ_WIDTH), rowblk),
            pl.BlockSpec((tq, IDX_HEADS * IDX_DIM), rowblk),
            pl.BlockSpec((tq, LANES), rowblk),
            pl.BlockSpec((s, LANES), perb),
            pl.BlockSpec((s, LANES), perb),
            pl.BlockSpec((s, LANES), perb),
            pl.BlockSpec((1, tq, 1), lambda bb, i, *_: (bb * nq + i, 0, 0)),
            pl.BlockSpec((1, 1, s), lambda bb, i, *_: (bb, 0, 0)),
        ],
        out_specs=pl.BlockSpec((tq, SEQ_WIDTH), rowblk),
        scratch_shapes=[
            pltpu.VMEM((tq, s), I32),
            pltpu.VMEM((DSA_HEADS * tq, LANES), BF16),
            pltpu.VMEM((IDX_HEADS * tq, LANES), BF16),
            pltpu.VMEM((IDX_HEADS * tq, LANES), F32),
            pltpu.VMEM((DSA_HEADS * tq, LANES), F32),
            pltpu.VMEM((DSA_HEADS * tq, LANES), F32),
            pltpu.VMEM((tq, LANES), I32),
            pltpu.VMEM((tq, LANES), I32),
        ],
    )
    return pl.pallas_call(
        kern, grid_spec=grid_spec,
        out_shape=jax.ShapeDtypeStruct((b * s, SEQ_WIDTH), BF16),
        compiler_params=_params(("parallel", "arbitrary")), name="dsa_attn",
    )(status, nhi, q, iq, iw, kk, vv, ikk, cq3, ck3)


def _mix_out_kernel(x_ref, oseq_ref, qm_ref, km_ref, vm_ref, wo_ref, g_ref, o_ref):
    qm = qm_ref[...]
    km = km_ref[...]
    vm = vm_ref[...]
    lane = lax.broadcasted_iota(I32, qm.shape, 1)
    o_mem = jnp.zeros(qm.shape, F32)
    for h in range(MEM_HEADS):
        mine = (lane // HEAD_DIM) == h
        s = _dot_nt(jnp.where(mine, qm, jnp.zeros_like(qm)), km)
        p = jnp.exp(s - jnp.max(s, axis=1, keepdims=True))
        p = p / jnp.sum(p, axis=1, keepdims=True)
        o_mem = o_mem + jnp.where(mine, _dot(p.astype(BF16), vm), 0.0)
    y = _dot(oseq_ref[...], wo_ref[0:SEQ_WIDTH, :]) + _dot(o_mem.astype(BF16), wo_ref[SEQ_WIDTH:, :])
    o_ref[...] = x_ref[...] + _rms(y, g_ref[...])


def _mix_out(x2d, oseq, qm, kv, w_out, g, b, s, n_mem, tm):
    nt = s // tm
    return pl.pallas_call(
        _mix_out_kernel, grid=(b, nt),
        in_specs=[
            pl.BlockSpec((tm, D_MODEL), lambda bb, i: (bb * nt + i, 0)),
            pl.BlockSpec((tm, SEQ_WIDTH), lambda bb, i: (bb * nt + i, 0)),
            pl.BlockSpec((tm, MEM_WIDTH), lambda bb, i: (bb * nt + i, 0)),
            pl.BlockSpec((n_mem, MEM_WIDTH), lambda bb, i: (bb, 0)),
            pl.BlockSpec((n_mem, MEM_WIDTH), lambda bb, i: (bb, 1)),
            pl.BlockSpec((D_MODEL, D_MODEL), lambda bb, i: (0, 0)),
            pl.BlockSpec((1, D_MODEL), lambda bb, i: (0, 0)),
        ],
        out_specs=pl.BlockSpec((tm, D_MODEL), lambda bb, i: (bb * nt + i, 0)),
        out_shape=jax.ShapeDtypeStruct(x2d.shape, F32),
        compiler_params=_params(("parallel", "parallel")), name="mix_out",
    )(x2d, oseq, qm, kv, kv, w_out, g.reshape(1, D_MODEL))


def _mlp_kernel(x_ref, gpre_ref, wup_ref, wdn_ref, gpost_ref, o_ref, h_s, acc_s):
    j = pl.program_id(1)

    @pl.when(j == 0)
    def _():
        h_s[...] = _rms(x_ref[...], gpre_ref[...]).astype(BF16)
        acc_s[...] = jnp.zeros(acc_s.shape, F32)

    u = jnp.maximum(_dot(h_s[...], wup_ref[...]), 0.0)
    acc_s[...] += _dot((u * u).astype(BF16), wdn_ref[...])

    @pl.when(j == pl.num_programs(1) - 1)
    def _():
        o_ref[...] = x_ref[...] + _rms(acc_s[...], gpost_ref[...])


def _mlp(x2d, g_pre, w_up, w_dn, g_post, tm, tf):
    m = x2d.shape[0]
    return pl.pallas_call(
        _mlp_kernel, grid=(m // tm, D_FF // tf),
        in_specs=[
            pl.BlockSpec((tm, D_MODEL), lambda i, j: (i, 0)),
            pl.BlockSpec((1, D_MODEL), lambda i, j: (0, 0)),
            pl.BlockSpec((D_MODEL, tf), lambda i, j: (0, j)),
            pl.BlockSpec((tf, D_MODEL), lambda i, j: (j, 0)),
            pl.BlockSpec((1, D_MODEL), lambda i, j: (0, 0)),
        ],
        out_specs=pl.BlockSpec((tm, D_MODEL), lambda i, j: (i, 0)),
        out_shape=jax.ShapeDtypeStruct(x2d.shape, F32),
        scratch_shapes=[pltpu.VMEM((tm, D_MODEL), BF16), pltpu.VMEM((tm, D_MODEL), F32)],
        compiler_params=_params(("parallel", "arbitrary")), name="mlp",
    )(x2d, g_pre.reshape(1, D_MODEL), w_up, w_dn, g_post.reshape(1, D_MODEL))


def _rope_tables(positions):
    half = HEAD_DIM // 2
    inv = ROPE_THETA ** (-jnp.arange(0, HEAD_DIM, 2, dtype=F32) / HEAD_DIM)
    ang = positions.astype(F32).reshape(-1, 1) * inv
    cos, sin = jnp.cos(ang), jnp.sin(ang)
    reps = LANES // HEAD_DIM
    cos_t = jnp.tile(cos, (1, 2 * reps))
    sin_t = jnp.tile(jnp.concatenate([-sin, sin], axis=1), (1, reps))
    assert cos_t.shape[1] == LANES and half * 2 == HEAD_DIM
    return cos_t, sin_t


def _visibility(chunk_id, tq, tk):
    b, s = chunk_id.shape
    cq = chunk_id.reshape(b, s // tq, tq)
    ck = chunk_id.reshape(b, s // tk, tk)
    qmin, qmax = cq.min(-1)[:, :, None], cq.max(-1)[:, :, None]
    kmin, kmax = ck.min(-1)[:, None, :], ck.max(-1)[:, None, :]
    status = jnp.where(kmin > qmax, 0, jnp.where(kmax <= qmin, 1, 2)).astype(I32)
    tiles = jnp.arange(s // tk, dtype=I32)[None, None, :]
    nhi = jnp.max(jnp.where(status != 0, tiles + 1, 0), axis=-1).astype(I32)
    return status.reshape(-1), nhi.reshape(-1)


def _dup(w):
    return jnp.concatenate([w, w], axis=1)


def kernel(x, mem, positions, g_pre_mix, g_post_mix, g_mem, w_mem_kv, w_out, g_pre_mlp, g_post_mlp,
           w_mlp_up, w_mlp_down, w_in_diff, lambda_q1, lambda_k1, lambda_q2, lambda_k2, g_diff_subln,
           w_in_dsa):
    b, s, d = x.shape
    n_mem = mem.shape[1]
    depth = g_pre_mix.shape[0]
    m = b * s
    scale = HEAD_DIM ** -0.5
    scale2 = scale * math.log2(math.e)
    tm_proj = min(512, s)
    tq_diff = tk_diff = min(512, s)
    tq_dsa, tk_dsa = min(128, s), min(256, s)
    tm_mix = min(512, s)
    tm_mlp, tf_mlp = min(1024, m), 1024
    top_k = min(DSA_TOPK_MAX, s // 4)
    assert tk_dsa >= top_k, "the threshold search needs one key tile to hold top_k candidates"

    cos_t, sin_t = _rope_tables(positions)
    chunk_id = positions // CHUNK
    cq3 = chunk_id.reshape(-1, 1)
    ck3 = chunk_id.reshape(b, 1, s)
    vis_diff = _visibility(chunk_id, tq_diff, tk_diff)
    vis_dsa = _visibility(chunk_id, tq_dsa, tk_dsa)

    x2d = x.reshape(m, d)
    mem2d = mem.reshape(b * n_mem, d)
    for i in range(depth):
        j = i // 2
        kv = _inproj(mem2d, g_mem[i], None, None, None, w_mem_kv[i].astype(BF16), (),
                     ((2 * MEM_WIDTH, 1.0, BF16),), min(256, b * n_mem))[0]
        if i % 2 == 0:
            w = w_in_diff[j].astype(BF16)
            nqk = 2 * DIFF_HEADS * HEAD_DIM
            q, k, v, qm = _inproj(
                x2d, g_pre_mix[i], cos_t, sin_t, w[:, :2 * nqk], w[:, 2 * nqk:],
                ((nqk, scale2, BF16), (nqk, 1.0, BF16)),
                ((SEQ_WIDTH, 1.0, BF16), (MEM_WIDTH, scale, BF16)), tm_proj)
            lam_p = jnp.stack([lambda_q1[j], lambda_k1[j], lambda_q2[j], lambda_k2[j]]).astype(F32)
            lam_init = 0.8 - 0.6 * math.exp(-0.3 * i)
            o_seq = _diff_attention(
                q, k, v, cq3.reshape(m // tq_diff, tq_diff, 1), ck3, vis_diff[0], vis_diff[1],
                lam_p, g_diff_subln[j].reshape(1, LANES), b, s, tq_diff, tk_diff, lam_init)
        else:
            w = w_in_dsa[j]
            o0 = 0
            parts = []
            for width in (DSA_HEADS * HEAD_DIM, HEAD_DIM, HEAD_DIM, IDX_HEADS * IDX_DIM, IDX_DIM,
                          IDX_HEADS, MEM_WIDTH):
                parts.append(w[:, o0:o0 + width])
                o0 += width
            wq, wk, wv, wiq, wik, wiw, wqm = parts
            wiw = jnp.pad(wiw, ((0, 0), (0, LANES - IDX_HEADS)))
            w_rope = jnp.concatenate([wq, wiq, _dup(wk), _dup(wik)], axis=1).astype(BF16)
            w_plain = jnp.concatenate([_dup(wv), wiw, wqm], axis=1).astype(BF16)
            iw_scale = IDX_HEADS ** -0.5 * IDX_DIM ** -0.5
            q, iq, kk, ikk, vv, iw, qm = _inproj(
                x2d, g_pre_mix[i], cos_t, sin_t, w_rope, w_plain,
                ((DSA_HEADS * HEAD_DIM, scale2, BF16), (IDX_HEADS * IDX_DIM, 1.0, BF16),
                 (LANES, 1.0, BF16), (LANES, 1.0, BF16)),
                ((LANES, 1.0, BF16), (LANES, iw_scale, F32), (MEM_WIDTH, scale, BF16)), tm_proj)
            o_seq = _dsa_attention(
                q, iq, iw, kk, vv, ikk, cq3.reshape(m // tq_dsa, tq_dsa, 1), ck3,
                vis_dsa[0], vis_dsa[1], b, s, tq_dsa, tk_dsa, top_k)
        x2d = _mix_out(x2d, o_seq, qm, kv, w_out[i].astype(BF16), g_post_mix[i], b, s, n_mem, tm_mix)
        x2d = _mlp(x2d, g_pre_mlp[i], w_mlp_up[i].astype(BF16), w_mlp_down[i].astype(BF16),
                   g_post_mlp[i], tm_mlp, tf_mlp)
    return x2d.reshape(b, s, d)
```

```python
import functools
import math

import jax
import jax.numpy as jnp
from jax import lax
from jax.experimental import pallas as pl
from jax.experimental.pallas import tpu as pltpu

D_MODEL = 1024
CHUNK = 64
HEAD_DIM = 64
LANES = 128
ROPE_THETA = 10000.0
EPS = 1e-6
MEM_HEADS = 4
MEM_WIDTH = MEM_HEADS * HEAD_DIM
SEQ_WIDTH = D_MODEL - MEM_WIDTH
DIFF_HEADS = SEQ_WIDTH // (2 * HEAD_DIM)
DSA_HEADS = SEQ_WIDTH // HEAD_DIM
IDX_HEADS = 8
IDX_DIM = 64
DSA_TOPK_MAX = 256
D_FF = 4 * D_MODEL

NEG = -1e30
INT_MIN = -(2 ** 31)
MASKED_KEY = -2139095041
VMEM_LIMIT = 56 * 1024 * 1024

F32 = jnp.float32
BF16 = jnp.bfloat16
I32 = jnp.int32


def _dot(a, b):
    return jnp.dot(a, b, preferred_element_type=F32)


def _dot_nt(a, b):
    return lax.dot_general(a, b, (((1,), (1,)), ((), ())), preferred_element_type=F32)


def _lane_tile(x, width):
    return jnp.concatenate([x] * (width // LANES), axis=1)


def _rms(x, g):
    return x * lax.rsqrt(jnp.mean(x * x, axis=-1, keepdims=True) + EPS) * g


def _params(sem):
    return pltpu.CompilerParams(dimension_semantics=sem, vmem_limit_bytes=VMEM_LIMIT)


def _inproj_kernel(*refs, rope_outs, plain_outs):
    n_r, n_p = len(rope_outs), len(plain_outs)
    x_ref, g_ref = refs[0], refs[1]
    pos = 2
    if n_r:
        cos_ref, sin_ref, wr_ref = refs[2], refs[3], refs[4]
        pos = 5
    if n_p:
        wp_ref = refs[pos]
        pos += 1
    out_refs = refs[pos:]
    x = x_ref[...]
    h = _rms(x, g_ref[...]).astype(BF16)
    oi = 0
    if n_r:
        cos = cos_ref[...]
        sin = sin_ref[...]
        lane = lax.broadcasted_iota(I32, cos.shape, 1)
        low = (lane % HEAD_DIM) < (HEAD_DIM // 2)
        c0 = 0
        for width, scale in rope_outs:
            o_ref = out_refs[oi]
            oi += 1
            for s in range(width // LANES):
                y = _dot(h, wr_ref[:, c0 + s * LANES:c0 + (s + 1) * LANES])
                swapped = jnp.where(low, pltpu.roll(y, LANES - HEAD_DIM // 2, 1),
                                    pltpu.roll(y, HEAD_DIM // 2, 1))
                r = y * cos + swapped * sin
                if scale != 1.0:
                    r = r * scale
                o_ref[:, s * LANES:(s + 1) * LANES] = r.astype(o_ref.dtype)
            c0 += width
    c0 = 0
    for width, scale in plain_outs:
        o_ref = out_refs[oi]
        oi += 1
        y = _dot(h, wp_ref[:, c0:c0 + width])
        if scale != 1.0:
            y = y * scale
        o_ref[...] = y.astype(o_ref.dtype)
        c0 += width


def _inproj(x2d, g, cos, sin, w_rope, w_plain, rope_outs, plain_outs, tm):
    m = x2d.shape[0]
    row = lambda i: (i, 0)
    fixed = lambda i: (0, 0)
    in_specs = [pl.BlockSpec((tm, D_MODEL), row), pl.BlockSpec((1, D_MODEL), fixed)]
    args = [x2d, g.reshape(1, D_MODEL)]
    if rope_outs:
        in_specs += [pl.BlockSpec((tm, LANES), row), pl.BlockSpec((tm, LANES), row),
                     pl.BlockSpec(w_rope.shape, fixed)]
        args += [cos, sin, w_rope]
    if plain_outs:
        in_specs += [pl.BlockSpec(w_plain.shape, fixed)]
        args += [w_plain]
    outs = tuple(rope_outs) + tuple(plain_outs)
    out_shape = [jax.ShapeDtypeStruct((m, w), dt) for w, _, dt in outs]
    out_specs = [pl.BlockSpec((tm, w), row) for w, _, _ in outs]
    kern = functools.partial(_inproj_kernel,
                             rope_outs=tuple((w, s) for w, s, _ in rope_outs),
                             plain_outs=tuple((w, s) for w, s, _ in plain_outs))
    return pl.pallas_call(
        kern, grid=(m // tm,), in_specs=in_specs, out_specs=out_specs, out_shape=out_shape,
        compiler_params=_params(("parallel",)), name="inproj")(*args)


def _diff_attn_kernel(status_ref, nhi_ref, q_ref, k_ref, v_ref, cq_ref, ck_ref, lam_ref, gsub_ref,
                      o_ref, m_s, l_s, acc_s, *, tq, tk, nq, nk, lam_init):
    b = pl.program_id(0)
    i = pl.program_id(2)
    q = q_ref[...]
    lane = lax.broadcasted_iota(I32, q.shape, 1)
    zero = jnp.zeros_like(q)
    qs = (jnp.where(lane < HEAD_DIM, q, zero), jnp.where(lane >= HEAD_DIM, q, zero))
    cq = cq_ref[0]
    m_s[...] = jnp.full(m_s.shape, NEG, F32)
    l_s[...] = jnp.zeros(l_s.shape, F32)
    acc_s[...] = jnp.zeros(acc_s.shape, F32)
    base = (b * nq + i) * nk

    def tile(t, masked):
        off = pl.multiple_of(t * tk, tk)
        kt = k_ref[pl.ds(off, tk), :]
        vt = v_ref[pl.ds(off, tk), :]
        s = [_dot_nt(qs[0], kt), _dot_nt(qs[1], kt)]
        if masked:
            ck = ck_ref[0, :, pl.ds(off, tk)]
            bias = jnp.where(cq >= ck, 0.0, NEG)
            s = [s[0] + bias, s[1] + bias]
        for j in range(2):
            m_old = m_s[j]
            m_new = jnp.maximum(m_old, jnp.max(s[j], axis=1, keepdims=True))
            p = jnp.exp2(s[j] - _lane_tile(m_new, tk))
            alpha = jnp.exp2(m_old - m_new)
            l_s[j] = alpha * l_s[j] + jnp.sum(p, axis=1, keepdims=True)
            acc_s[j] = alpha * acc_s[j] + _dot(p.astype(BF16), vt)
            m_s[j] = m_new

    def body(t, carry):
        st = status_ref[base + t]

        @pl.when(st == 1)
        def _():
            tile(t, False)

        @pl.when(st == 2)
        def _():
            tile(t, True)

        return carry

    lax.fori_loop(0, nhi_ref[b * nq + i], body, 0)

    lam_p = lam_ref[...]
    lam = (jnp.exp(jnp.sum(lam_p[0:1] * lam_p[1:2], axis=1, keepdims=True))
           - jnp.exp(jnp.sum(lam_p[2:3] * lam_p[3:4], axis=1, keepdims=True)) + lam_init)
    o = acc_s[0] / l_s[0] - lam * (acc_s[1] / l_s[1])
    o = _rms(o, gsub_ref[...]) * (1.0 - lam_init)
    o_ref[...] = o.astype(o_ref.dtype)


def _diff_attention(q, k, v, cq3, ck3, status, nhi, lam_p, gsub, b, s, tq, tk, lam_init):
    nq, nk = s // tq, s // tk
    kern = functools.partial(_diff_attn_kernel, tq=tq, tk=tk, nq=nq, nk=nk, lam_init=lam_init)
    grid_spec = pltpu.PrefetchScalarGridSpec(
        num_scalar_prefetch=2,
        grid=(b, DIFF_HEADS, nq),
        in_specs=[
            pl.BlockSpec((tq, LANES), lambda bb, h, i, *_: (bb * nq + i, h)),
            pl.BlockSpec((s, LANES), lambda bb, h, i, *_: (bb, h)),
            pl.BlockSpec((s, LANES), lambda bb, h, i, *_: (bb, h)),
            pl.BlockSpec((1, tq, 1), lambda bb, h, i, *_: (bb * nq + i, 0, 0)),
            pl.BlockSpec((1, 1, s), lambda bb, h, i, *_: (bb, 0, 0)),
            pl.BlockSpec((4, HEAD_DIM), lambda bb, h, i, *_: (0, 0)),
            pl.BlockSpec((1, LANES), lambda bb, h, i, *_: (0, 0)),
        ],
        out_specs=pl.BlockSpec((tq, LANES), lambda bb, h, i, *_: (bb * nq + i, h)),
        scratch_shapes=[pltpu.VMEM((2, tq, LANES), F32), pltpu.VMEM((2, tq, LANES), F32),
                        pltpu.VMEM((2, tq, LANES), F32)],
    )
    return pl.pallas_call(
        kern, grid_spec=grid_spec,
        out_shape=jax.ShapeDtypeStruct((b * s, SEQ_WIDTH), BF16),
        compiler_params=_params(("parallel", "parallel", "arbitrary")), name="diff_attn",
    )(status, nhi, q, k, v, cq3, ck3, lam_p, gsub)


def _dsa_kernel(status_ref, nhi_ref, q_ref, iq_ref, iw_ref, kk_ref, vv_ref, ikk_ref, cq_ref, ck_ref,
                o_ref, keys_s, qst_s, iqst_s, iwb_s, m_s, acc_s, thr_s, jst_s,
                *, tq, tk, nq, nk, s_len, top_k):
    b = pl.program_id(0)
    i = pl.program_id(1)
    base = (b * nq + i) * nk
    n_hi = nhi_ref[b * nq + i]
    cq = cq_ref[0]
    lane = lax.broadcasted_iota(I32, (tq, LANES), 1)
    first = lane < HEAD_DIM

    for h in range(DSA_HEADS):
        slab = q_ref[:, (h // 2) * LANES:(h // 2 + 1) * LANES]
        keep = first if h % 2 == 0 else jnp.logical_not(first)
        qst_s[h * tq:(h + 1) * tq, :] = jnp.where(keep, slab, jnp.zeros_like(slab))
    for h in range(IDX_HEADS):
        slab = iq_ref[:, (h // 2) * LANES:(h // 2 + 1) * LANES]
        keep = first if h % 2 == 0 else jnp.logical_not(first)
        iqst_s[h * tq:(h + 1) * tq, :] = jnp.where(keep, slab, jnp.zeros_like(slab))
        iwb_s[h * tq:(h + 1) * tq, :] = jnp.broadcast_to(iw_ref[:, h:h + 1], (tq, LANES))

    def score_body(t, carry):
        st = status_ref[base + t]
        off = pl.multiple_of(t * tk, tk)

        @pl.when(st == 0)
        def _():
            keys_s[:, pl.ds(off, tk)] = jnp.full((tq, tk), MASKED_KEY, I32)

        def scores(masked):
            ikt = ikk_ref[pl.ds(off, tk), :]
            logits = _dot_nt(iqst_s[...], ikt)
            iwb = iwb_s[...]
            iwt = jnp.concatenate([iwb] * (tk // LANES), axis=1)
            r = jnp.maximum(logits, 0.0) * iwt
            sc = r[0:tq]
            for h in range(1, IDX_HEADS):
                sc = sc + r[h * tq:(h + 1) * tq]
            bits = lax.bitcast_convert_type(sc, I32)
            key = jnp.where(bits < 0, bits ^ jnp.int32(0x7FFFFFFF), bits)
            if masked:
                ck = ck_ref[0, :, pl.ds(off, tk)]
                key = jnp.where(cq >= ck, key, jnp.int32(MASKED_KEY))
            keys_s[:, pl.ds(off, tk)] = key

        @pl.when(st == 1)
        def _():
            scores(False)

        @pl.when(st == 2)
        def _():
            scores(True)

        return carry

    lax.fori_loop(0, n_hi, score_body, 0)

    rg = min(tq, LANES)
    lane_rg = lax.broadcasted_iota(I32, (rg, LANES), 1)

    def count_tiles(pred, *operands):
        accs = []
        for r in range(tq // rg):
            rows = slice(r * rg, (r + 1) * rg)
            ops = [o[rows] for o in operands]

            def body(t, acc, rows=rows, ops=ops):
                off = pl.multiple_of(t * tk, tk)
                for c in range(tk // LANES):
                    slab = keys_s[rows, pl.ds(off + c * LANES, LANES)]
                    acc = acc + jnp.where(pred(slab, off + c * LANES, *ops), 1, 0).astype(I32)
                return acc
            accs.append(lax.fori_loop(0, n_hi, body, jnp.zeros((rg, LANES), I32)))
        acc = accs[0] if len(accs) == 1 else jnp.concatenate(accs, axis=0)
        return jnp.broadcast_to(jnp.sum(acc, axis=1, keepdims=True), (tq, LANES))

    def bit_body(j, thr):
        cand = thr + lax.shift_left(jnp.int32(1), 31 - j)
        cnt = count_tiles(lambda slab, col0, c: slab >= c, cand)
        return jnp.where(cnt >= top_k, cand, thr)

    thr = lax.fori_loop(0, 32, bit_body, jnp.full((tq, LANES), INT_MIN, I32))
    cnt_gt = count_tiles(lambda slab, col0, th: slab > th, thr)
    cnt_eq = count_tiles(lambda slab, col0, th: slab == th, thr)
    need = top_k - cnt_gt
    real = thr > MASKED_KEY
    excess = jnp.logical_and(real, cnt_eq > need)
    thr_s[...] = thr
    jst_s[...] = jnp.where(real, jnp.int32(s_len), jnp.int32(-1))

    @pl.when(jnp.max(excess.astype(I32)) > 0)
    def _():
        def idx_body(j, jcur):
            cand = jcur + lax.shift_left(jnp.int32(1), (s_len.bit_length() - 2) - j)
            cnt = count_tiles(
                lambda slab, col0, th, c: jnp.logical_and(slab == th, (col0 + lane_rg) < c), thr, cand)
            return jnp.where(cnt < need, cand, jcur)
        jbest = lax.fori_loop(0, s_len.bit_length() - 1, idx_body, jnp.zeros((tq, LANES), I32))
        jst_s[...] = jnp.where(excess, jbest, jst_s[...])

    m_s[...] = jnp.full(m_s.shape, NEG, F32)
    acc_s[...] = jnp.zeros(acc_s.shape, F32)
    lane_k = lax.broadcasted_iota(I32, (tq, tk), 1)
    lane_v = lax.broadcasted_iota(I32, (tk, LANES), 1)

    def attn_body(t, carry):
        st = status_ref[base + t]

        @pl.when(st != 0)
        def _():
            off = pl.multiple_of(t * tk, tk)
            key = keys_s[:, pl.ds(off, tk)]
            th = _lane_tile(thr_s[...], tk)
            jst = _lane_tile(jst_s[...], tk)
            sel = jnp.logical_or(key > th, jnp.logical_and(key == th, (lane_k + off) <= jst))
            bias = jnp.where(sel, 0.0, NEG)
            kt = kk_ref[pl.ds(off, tk), :]
            vt = vv_ref[pl.ds(off, tk), :]
            vt = jnp.where(lane_v < HEAD_DIM, vt, jnp.ones_like(vt))
            s = _dot_nt(qst_s[...], kt)
            s = (s.reshape(DSA_HEADS, tq, tk) + bias[None]).reshape(DSA_HEADS * tq, tk)
            m_old = m_s[...]
            m_new = jnp.maximum(m_old, jnp.max(s, axis=1, keepdims=True))
            p = jnp.exp2(s - _lane_tile(m_new, tk))
            alpha = jnp.exp2(m_old - m_new)
            acc_s[...] = alpha * acc_s[...] + _dot(p.astype(BF16), vt)
            m_s[...] = m_new

        return carry

    lax.fori_loop(0, n_hi, attn_body, 0)

    for j in range(DSA_HEADS // 2):
        acc_a = acc_s[2 * j * tq:(2 * j + 1) * tq]
        acc_b = acc_s[(2 * j + 1) * tq:(2 * j + 2) * tq]
        swap_a = pltpu.roll(acc_a, HEAD_DIM, 1)
        swap_b = pltpu.roll(acc_b, HEAD_DIM, 1)
        o_ref[:, j * LANES:(j + 1) * LANES] = jnp.where(first, acc_a / swap_a,
                                                        swap_b / acc_b).astype(o_ref.dtype)


def _dsa_attention(q, iq, iw, kk, vv, ikk, cq3, ck3, status, nhi, b, s, tq, tk, top_k):
    nq, nk = s // tq, s // tk
    kern = functools.partial(_dsa_kernel, tq=tq, tk=tk, nq=nq, nk=nk, s_len=s, top_k=top_k)
    rowblk = lambda bb, i, *_: (bb * nq + i, 0)
    perb = lambda bb, i, *_: (bb, 0)
    grid_spec = pltpu.PrefetchScalarGridSpec(
        num_scalar_prefetch=2,
        grid=(b, nq),
        in_specs=[
            pl.BlockSpec((tq, SEQ_WIDTH), rowblk),
            pl.BlockSpec((tq, IDX_HEADS * IDX_DIM), rowblk),
            pl.BlockSpec((tq, LANES), rowblk),
            pl.BlockSpec((s, LANES), perb),
            pl.BlockSpec((s, LANES), perb),
            pl.BlockSpec((s, LANES), perb),
            pl.BlockSpec((1, tq, 1), lambda bb, i, *_: (bb * nq + i, 0, 0)),
            pl.BlockSpec((1, 1, s), lambda bb, i, *_: (bb, 0, 0)),
        ],
        out_specs=pl.BlockSpec((tq, SEQ_WIDTH), rowblk),
        scratch_shapes=[
            pltpu.VMEM((tq, s), I32),
            pltpu.VMEM((DSA_HEADS * tq, LANES), BF16),
            pltpu.VMEM((IDX_HEADS * tq, LANES), BF16),
            pltpu.VMEM((IDX_HEADS * tq, LANES), F32),
            pltpu.VMEM((DSA_HEADS * tq, LANES), F32),
            pltpu.VMEM((DSA_HEADS * tq, LANES), F32),
            pltpu.VMEM((tq, LANES), I32),
            pltpu.VMEM((tq, LANES), I32),
        ],
    )
    return pl.pallas_call(
        kern, grid_spec=grid_spec,
        out_shape=jax.ShapeDtypeStruct((b * s, SEQ_WIDTH), BF16),
        compiler_params=_params(("parallel", "arbitrary")), name="dsa_attn",
    )(status, nhi, q, iq, iw, kk, vv, ikk, cq3, ck3)


def _mix_out_kernel(x_ref, oseq_ref, qm_ref, km_ref, vm_ref, wo_ref, g_ref, o_ref):
    qm = qm_ref[...]
    km = km_ref[...]
    vm = vm_ref[...]
    lane = lax.broadcasted_iota(I32, qm.shape, 1)
    o_mem = jnp.zeros(qm.shape, F32)
    for h in range(MEM_HEADS):
        mine = (lane // HEAD_DIM) == h
        s = _dot_nt(jnp.where(mine, qm, jnp.zeros_like(qm)), km)
        p = jnp.exp(s - jnp.max(s, axis=1, keepdims=True))
        p = p / jnp.sum(p, axis=1, keepdims=True)
        o_mem = o_mem + jnp.where(mine, _dot(p.astype(BF16), vm), 0.0)
    y = _dot(oseq_ref[...], wo_ref[0:SEQ_WIDTH, :]) + _dot(o_mem.astype(BF16), wo_ref[SEQ_WIDTH:, :])
    o_ref[...] = x_ref[...] + _rms(y, g_ref[...])


def _mix_out(x2d, oseq, qm, kv, w_out, g, b, s, n_mem, tm):
    nt = s // tm
    return pl.pallas_call(
        _mix_out_kernel, grid=(b, nt),
        in_specs=[
            pl.BlockSpec((tm, D_MODEL), lambda bb, i: (bb * nt + i, 0)),
            pl.BlockSpec((tm, SEQ_WIDTH), lambda bb, i: (bb * nt + i, 0)),
            pl.BlockSpec((tm, MEM_WIDTH), lambda bb, i: (bb * nt + i, 0)),
            pl.BlockSpec((n_mem, MEM_WIDTH), lambda bb, i: (bb, 0)),
            pl.BlockSpec((n_mem, MEM_WIDTH), lambda bb, i: (bb, 1)),
            pl.BlockSpec((D_MODEL, D_MODEL), lambda bb, i: (0, 0)),
            pl.BlockSpec((1, D_MODEL), lambda bb, i: (0, 0)),
        ],
        out_specs=pl.BlockSpec((tm, D_MODEL), lambda bb, i: (bb * nt + i, 0)),
        out_shape=jax.ShapeDtypeStruct(x2d.shape, F32),
        compiler_params=_params(("parallel", "parallel")), name="mix_out",
    )(x2d, oseq, qm, kv, kv, w_out, g.reshape(1, D_MODEL))


def _mlp_kernel(x_ref, gpre_ref, wup_ref, wdn_ref, gpost_ref, o_ref, h_s, acc_s):
    j = pl.program_id(1)

    @pl.when(j == 0)
    def _():
        h_s[...] = _rms(x_ref[...], gpre_ref[...]).astype(BF16)
        acc_s[...] = jnp.zeros(acc_s.shape, F32)

    u = jnp.maximum(_dot(h_s[...], wup_ref[...]), 0.0)
    acc_s[...] += _dot((u * u).astype(BF16), wdn_ref[...])

    @pl.when(j == pl.num_programs(1) - 1)
    def _():
        o_ref[...] = x_ref[...] + _rms(acc_s[...], gpost_ref[...])


def _mlp(x2d, g_pre, w_up, w_dn, g_post, tm, tf):
    m = x2d.shape[0]
    return pl.pallas_call(
        _mlp_kernel, grid=(m // tm, D_FF // tf),
        in_specs=[
            pl.BlockSpec((tm, D_MODEL), lambda i, j: (i, 0)),
            pl.BlockSpec((1, D_MODEL), lambda i, j: (0, 0)),
            pl.BlockSpec((D_MODEL, tf), lambda i, j: (0, j)),
            pl.BlockSpec((tf, D_MODEL), lambda i, j: (j, 0)),
            pl.BlockSpec((1, D_MODEL), lambda i, j: (0, 0)),
        ],
        out_specs=pl.BlockSpec((tm, D_MODEL), lambda i, j: (i, 0)),
        out_shape=jax.ShapeDtypeStruct(x2d.shape, F32),
        scratch_shapes=[pltpu.VMEM((tm, D_MODEL), BF16), pltpu.VMEM((tm, D_MODEL), F32)],
        compiler_params=_params(("parallel", "arbitrary")), name="mlp",
    )(x2d, g_pre.reshape(1, D_MODEL), w_up, w_dn, g_post.reshape(1, D_MODEL))


def _rope_tables(positions):
    half = HEAD_DIM // 2
    inv = ROPE_THETA ** (-jnp.arange(0, HEAD_DIM, 2, dtype=F32) / HEAD_DIM)
    ang = positions.astype(F32).reshape(-1, 1) * inv
    cos, sin = jnp.cos(ang), jnp.sin(ang)
    reps = LANES // HEAD_DIM
    cos_t = jnp.tile(cos, (1, 2 * reps))
    sin_t = jnp.tile(jnp.concatenate([-sin, sin], axis=1), (1, reps))
    assert cos_t.shape[1] == LANES and half * 2 == HEAD_DIM
    return cos_t, sin_t


def _visibility(chunk_id, tq, tk):
    b, s = chunk_id.shape
    cq = chunk_id.reshape(b, s // tq, tq)
    ck = chunk_id.reshape(b, s // tk, tk)
    qmin, qmax = cq.min(-1)[:, :, None], cq.max(-1)[:, :, None]
    kmin, kmax = ck.min(-1)[:, None, :], ck.max(-1)[:, None, :]
    status = jnp.where(kmin > qmax, 0, jnp.where(kmax <= qmin, 1, 2)).astype(I32)
    tiles = jnp.arange(s // tk, dtype=I32)[None, None, :]
    nhi = jnp.max(jnp.where(status != 0, tiles + 1, 0), axis=-1).astype(I32)
    return status.reshape(-1), nhi.reshape(-1)


def _dup(w):
    return jnp.concatenate([w, w], axis=1)


def kernel(x, mem, positions, g_pre_mix, g_post_mix, g_mem, w_mem_kv, w_out, g_pre_mlp, g_post_mlp,
           w_mlp_up, w_mlp_down, w_in_diff, lambda_q1, lambda_k1, lambda_q2, lambda_k2, g_diff_subln,
           w_in_dsa):
    b, s, d = x.shape
    n_mem = mem.shape[1]
    depth = g_pre_mix.shape[0]
    m = b * s
    scale = HEAD_DIM ** -0.5
    scale2 = scale * math.log2(math.e)
    tm_proj = min(512, s)
    tq_diff = tk_diff = min(512, s)
    tq_dsa, tk_dsa = min(256, s), min(512, s)
    tm_mix = min(512, s)
    tm_mlp, tf_mlp = min(1024, m), 1024
    top_k = min(DSA_TOPK_MAX, s // 4)
    assert tk_dsa >= top_k, "the threshold search needs one key tile to hold top_k candidates"

    cos_t, sin_t = _rope_tables(positions)
    chunk_id = positions // CHUNK
    cq3 = chunk_id.reshape(-1, 1)
    ck3 = chunk_id.reshape(b, 1, s)
    vis_diff = _visibility(chunk_id, tq_diff, tk_diff)
    vis_dsa = _visibility(chunk_id, tq_dsa, tk_dsa)

    x2d = x.reshape(m, d)
    mem2d = mem.reshape(b * n_mem, d)
    for i in range(depth):
        j = i // 2
        kv = _inproj(mem2d, g_mem[i], None, None, None, w_mem_kv[i].astype(BF16), (),
                     ((2 * MEM_WIDTH, 1.0, BF16),), min(256, b * n_mem))[0]
        if i % 2 == 0:
            w = w_in_diff[j].astype(BF16)
            nqk = 2 * DIFF_HEADS * HEAD_DIM
            q, k, v, qm = _inproj(
                x2d, g_pre_mix[i], cos_t, sin_t, w[:, :2 * nqk], w[:, 2 * nqk:],
                ((nqk, scale2, BF16), (nqk, 1.0, BF16)),
                ((SEQ_WIDTH, 1.0, BF16), (MEM_WIDTH, scale, BF16)), tm_proj)
            lam_p = jnp.stack([lambda_q1[j], lambda_k1[j], lambda_q2[j], lambda_k2[j]]).astype(F32)
            lam_init = 0.8 - 0.6 * math.exp(-0.3 * i)
            o_seq = _diff_attention(
                q, k, v, cq3.reshape(m // tq_diff, tq_diff, 1), ck3, vis_diff[0], vis_diff[1],
                lam_p, g_diff_subln[j].reshape(1, LANES), b, s, tq_diff, tk_diff, lam_init)
        else:
            w = w_in_dsa[j]
            o0 = 0
            parts = []
            for width in (DSA_HEADS * HEAD_DIM, HEAD_DIM, HEAD_DIM, IDX_HEADS * IDX_DIM, IDX_DIM,
                          IDX_HEADS, MEM_WIDTH):
                parts.append(w[:, o0:o0 + width])
                o0 += width
            wq, wk, wv, wiq, wik, wiw, wqm = parts
            wiw = jnp.pad(wiw, ((0, 0), (0, LANES - IDX_HEADS)))
            w_rope = jnp.concatenate([wq, wiq, _dup(wk), _dup(wik)], axis=1).astype(BF16)
            w_plain = jnp.concatenate([_dup(wv), wiw, wqm], axis=1).astype(BF16)
            iw_scale = IDX_HEADS ** -0.5 * IDX_DIM ** -0.5
            q, iq, kk, ikk, vv, iw, qm = _inproj(
                x2d, g_pre_mix[i], cos_t, sin_t, w_rope, w_plain,
                ((DSA_HEADS * HEAD_DIM, scale2, BF16), (IDX_HEADS * IDX_DIM, 1.0, BF16),
                 (LANES, 1.0, BF16), (LANES, 1.0, BF16)),
                ((LANES, 1.0, BF16), (LANES, iw_scale, F32), (MEM_WIDTH, scale, BF16)), tm_proj)
            o_seq = _dsa_attention(
                q, iq, iw, kk, vv, ikk, cq3.reshape(m // tq_dsa, tq_dsa, 1), ck3,
                vis_dsa[0], vis_dsa[1], b, s, tq_dsa, tk_dsa, top_k)
        x2d = _mix_out(x2d, o_seq, qm, kv, w_out[i].astype(BF16), g_post_mix[i], b, s, n_mem, tm_mix)
        x2d = _mlp(x2d, g_pre_mlp[i], w_mlp_up[i].astype(BF16), w_mlp_down[i].astype(BF16),
                   g_post_mlp[i], tm_mlp, tf_mlp)
    return x2d.reshape(b, s, d)
```

```python
import functools
import math

import jax
import jax.numpy as jnp
from jax import lax
from jax.experimental import pallas as pl
from jax.experimental.pallas import tpu as pltpu

D_MODEL = 1024
CHUNK = 64
HEAD_DIM = 64
LANES = 128
ROPE_THETA = 10000.0
EPS = 1e-6
MEM_HEADS = 4
MEM_WIDTH = MEM_HEADS * HEAD_DIM
SEQ_WIDTH = D_MODEL - MEM_WIDTH
DIFF_HEADS = SEQ_WIDTH // (2 * HEAD_DIM)
DSA_HEADS = SEQ_WIDTH // HEAD_DIM
IDX_HEADS = 8
IDX_DIM = 64
DSA_TOPK_MAX = 256
D_FF = 4 * D_MODEL

NEG = -1e30
SORTABLE16_NEG_INF = -32513
VMEM_LIMIT = 56 * 1024 * 1024

F32 = jnp.float32
BF16 = jnp.bfloat16
I32 = jnp.int32


def _dot(a, b):
    return jnp.dot(a, b, preferred_element_type=F32)


def _dot_nt(a, b):
    return lax.dot_general(a, b, (((1,), (1,)), ((), ())), preferred_element_type=F32)


def _lane_tile(x, width):
    return jnp.concatenate([x] * (width // LANES), axis=1)


def _hi16(x):
    bits = lax.bitcast_convert_type(x, I32)
    return lax.bitcast_convert_type(bits & jnp.int32(-65536), F32).astype(BF16)


def _low_digits(x):
    bits = lax.bitcast_convert_type(x, I32)
    low = bits & 0xFFFF
    return jnp.where(bits < 0, low ^ 0xFFFF, low)


def _pattern16_to_bf16(key16):
    pat = jnp.where(key16 < 0, key16 ^ 0x7FFF, key16)
    return lax.bitcast_convert_type(pat << 16, F32).astype(BF16)


def _rms(x, g):
    return x * lax.rsqrt(jnp.mean(x * x, axis=-1, keepdims=True) + EPS) * g


def _params(sem):
    return pltpu.CompilerParams(dimension_semantics=sem, vmem_limit_bytes=VMEM_LIMIT)


def _inproj_kernel(*refs, rope_outs, plain_outs):
    n_r, n_p = len(rope_outs), len(plain_outs)
    x_ref, g_ref = refs[0], refs[1]
    pos = 2
    if n_r:
        cos_ref, sin_ref, wr_ref = refs[2], refs[3], refs[4]
        pos = 5
    if n_p:
        wp_ref = refs[pos]
        pos += 1
    out_refs = refs[pos:]
    x = x_ref[...]
    h = _rms(x, g_ref[...]).astype(BF16)
    oi = 0
    if n_r:
        cos = cos_ref[...]
        sin = sin_ref[...]
        lane = lax.broadcasted_iota(I32, cos.shape, 1)
        low = (lane % HEAD_DIM) < (HEAD_DIM // 2)
        c0 = 0
        for width, scale in rope_outs:
            o_ref = out_refs[oi]
            oi += 1
            for s in range(width // LANES):
                y = _dot(h, wr_ref[:, c0 + s * LANES:c0 + (s + 1) * LANES])
                swapped = jnp.where(low, pltpu.roll(y, LANES - HEAD_DIM // 2, 1),
                                    pltpu.roll(y, HEAD_DIM // 2, 1))
                r = y * cos + swapped * sin
                if scale != 1.0:
                    r = r * scale
                o_ref[:, s * LANES:(s + 1) * LANES] = r.astype(o_ref.dtype)
            c0 += width
    c0 = 0
    for width, scale in plain_outs:
        o_ref = out_refs[oi]
        oi += 1
        y = _dot(h, wp_ref[:, c0:c0 + width])
        if scale != 1.0:
            y = y * scale
        o_ref[...] = y.astype(o_ref.dtype)
        c0 += width


def _inproj(x2d, g, cos, sin, w_rope, w_plain, rope_outs, plain_outs, tm):
    m = x2d.shape[0]
    row = lambda i: (i, 0)
    fixed = lambda i: (0, 0)
    in_specs = [pl.BlockSpec((tm, D_MODEL), row), pl.BlockSpec((1, D_MODEL), fixed)]
    args = [x2d, g.reshape(1, D_MODEL)]
    if rope_outs:
        in_specs += [pl.BlockSpec((tm, LANES), row), pl.BlockSpec((tm, LANES), row),
                     pl.BlockSpec(w_rope.shape, fixed)]
        args += [cos, sin, w_rope]
    if plain_outs:
        in_specs += [pl.BlockSpec(w_plain.shape, fixed)]
        args += [w_plain]
    outs = tuple(rope_outs) + tuple(plain_outs)
    out_shape = [jax.ShapeDtypeStruct((m, w), dt) for w, _, dt in outs]
    out_specs = [pl.BlockSpec((tm, w), row) for w, _, _ in outs]
    kern = functools.partial(_inproj_kernel,
                             rope_outs=tuple((w, s) for w, s, _ in rope_outs),
                             plain_outs=tuple((w, s) for w, s, _ in plain_outs))
    return pl.pallas_call(
        kern, grid=(m // tm,), in_specs=in_specs, out_specs=out_specs, out_shape=out_shape,
        compiler_params=_params(("parallel",)), name="inproj")(*args)


def _diff_attn_kernel(status_ref, nhi_ref, q_ref, k_ref, v_ref, cq_ref, ck_ref, lam_ref, gsub_ref,
                      o_ref, m_s, acc_s, s0_s, s1_s, *, tq, tk, nq, nk, lam_init):
    b = pl.program_id(0)
    i = pl.program_id(2)
    q = q_ref[...]
    lane = lax.broadcasted_iota(I32, q.shape, 1)
    zero = jnp.zeros_like(q)
    qs = (jnp.where(lane < HEAD_DIM, q, zero), jnp.where(lane >= HEAD_DIM, q, zero))
    cq = cq_ref[0]
    m_s[...] = jnp.full(m_s.shape, NEG, F32)
    acc_s[...] = jnp.zeros(acc_s.shape, F32)
    base = (b * nq + i) * nk
    n_hi = nhi_ref[b * nq + i]

    def logits(t, dst_s):
        kt = k_ref[pl.ds(pl.multiple_of(t * tk, tk), tk), :]
        for j in range(2):
            dst_s[j] = _dot_nt(qs[j], kt)

    def tile(t, masked, cur_s, nxt_s):
        if nxt_s is not None:
            logits(t + 1, nxt_s)
        off = pl.multiple_of(t * tk, tk)
        vt = jnp.concatenate([v_ref[pl.ds(off, tk), :], jnp.ones((tk, LANES), BF16)], axis=1)
        if masked:
            ck = ck_ref[0, :, pl.ds(off, tk)]
            bias = jnp.where(cq >= ck, 0.0, NEG)
        for j in range(2):
            s = cur_s[j] + bias if masked else cur_s[j]
            m_old = m_s[j]
            m_new = jnp.maximum(m_old, jnp.max(s, axis=1, keepdims=True))
            p = jnp.exp2(s - _lane_tile(m_new, tk))
            alpha = jnp.exp2(m_old - m_new)
            acc_s[j] = _lane_tile(alpha, 2 * LANES) * acc_s[j] + _dot(p.astype(BF16), vt)
            m_s[j] = m_new

    def step(t, cur_s, nxt_s):
        st = status_ref[base + jnp.minimum(t, nk - 1)]
        live = t < n_hi
        more = t + 1 < n_hi

        @pl.when(jnp.logical_and(st == 0, more))
        def _():
            logits(t + 1, nxt_s)

        for code, masked in ((1, False), (2, True)):
            hit = jnp.logical_and(live, st == code)

            @pl.when(jnp.logical_and(hit, more))
            def _():
                tile(t, masked, cur_s, nxt_s)

            @pl.when(jnp.logical_and(hit, jnp.logical_not(more)))
            def _():
                tile(t, masked, cur_s, None)

    logits(0, s0_s)

    def body(u, carry):
        step(2 * u, s0_s, s1_s)
        step(2 * u + 1, s1_s, s0_s)
        return carry

    lax.fori_loop(0, (n_hi + 1) // 2, body, 0)

    lam_p = lam_ref[...]
    lam = (jnp.exp(jnp.sum(lam_p[0:1] * lam_p[1:2], axis=1, keepdims=True))
           - jnp.exp(jnp.sum(lam_p[2:3] * lam_p[3:4], axis=1, keepdims=True)) + lam_init)
    o = (acc_s[0, :, :LANES] / acc_s[0, :, LANES:]
         - lam * (acc_s[1, :, :LANES] / acc_s[1, :, LANES:]))
    o = _rms(o, gsub_ref[...]) * (1.0 - lam_init)
    o_ref[...] = o.astype(o_ref.dtype)


def _diff_attention(q, k, v, cq3, ck3, status, nhi, lam_p, gsub, b, s, tq, tk, lam_init):
    nq, nk = s // tq, s // tk
    kern = functools.partial(_diff_attn_kernel, tq=tq, tk=tk, nq=nq, nk=nk, lam_init=lam_init)
    grid_spec = pltpu.PrefetchScalarGridSpec(
        num_scalar_prefetch=2,
        grid=(b, DIFF_HEADS, nq),
        in_specs=[
            pl.BlockSpec((tq, LANES), lambda bb, h, i, *_: (bb * nq + i, h)),
            pl.BlockSpec((s, LANES), lambda bb, h, i, *_: (bb, h)),
            pl.BlockSpec((s, LANES), lambda bb, h, i, *_: (bb, h)),
            pl.BlockSpec((1, tq, 1), lambda bb, h, i, *_: (bb * nq + i, 0, 0)),
            pl.BlockSpec((1, 1, s), lambda bb, h, i, *_: (bb, 0, 0)),
            pl.BlockSpec((4, HEAD_DIM), lambda bb, h, i, *_: (0, 0)),
            pl.BlockSpec((1, LANES), lambda bb, h, i, *_: (0, 0)),
        ],
        out_specs=pl.BlockSpec((tq, LANES), lambda bb, h, i, *_: (bb * nq + i, h)),
        scratch_shapes=[pltpu.VMEM((2, tq, LANES), F32),
                        pltpu.VMEM((2, tq, 2 * LANES), F32),
                        pltpu.VMEM((2, tq, tk), F32),
                        pltpu.VMEM((2, tq, tk), F32)],
    )
    return pl.pallas_call(
        kern, grid_spec=grid_spec,
        out_shape=jax.ShapeDtypeStruct((b * s, SEQ_WIDTH), BF16),
        compiler_params=_params(("parallel", "parallel", "arbitrary")), name="diff_attn",
    )(status, nhi, q, k, v, cq3, ck3, lam_p, gsub)


def _dsa_kernel(status_ref, nhi_ref, q_ref, iq_ref, iw_ref, kk_ref, vv_ref, ikk_ref, cq_ref, ck_ref,
                o_ref, sc_s, hi_s, work_s, qst_s, iqst_s, iwb_s, m_s, acc_s, thr_s, jst_s,
                *, tq, tk, nq, nk, s_len, top_k):
    b = pl.program_id(0)
    i = pl.program_id(1)
    base = (b * nq + i) * nk
    n_hi = nhi_ref[b * nq + i]
    cq = cq_ref[0]
    lane = lax.broadcasted_iota(I32, (tq, LANES), 1)
    first = lane < HEAD_DIM

    for h in range(DSA_HEADS):
        slab = q_ref[:, (h // 2) * LANES:(h // 2 + 1) * LANES]
        keep = first if h % 2 == 0 else jnp.logical_not(first)
        qst_s[h * tq:(h + 1) * tq, :] = jnp.where(keep, slab, jnp.zeros_like(slab))
    for h in range(IDX_HEADS):
        slab = iq_ref[:, (h // 2) * LANES:(h // 2 + 1) * LANES]
        keep = first if h % 2 == 0 else jnp.logical_not(first)
        iqst_s[h * tq:(h + 1) * tq, :] = jnp.where(keep, slab, jnp.zeros_like(slab))
        iwb_s[h * tq:(h + 1) * tq, :] = jnp.broadcast_to(iw_ref[:, h:h + 1], (tq, LANES))

    def score_body(t, carry):
        st = status_ref[base + t]
        off = pl.multiple_of(t * tk, tk)

        @pl.when(st == 0)
        def _():
            sc_s[:, pl.ds(off, tk)] = jnp.full((tq, tk), -jnp.inf, F32)
            hi_s[:, pl.ds(off, tk)] = jnp.full((tq, tk), -jnp.inf, BF16)

        def scores(masked):
            ikt = ikk_ref[pl.ds(off, tk), :]
            logits = _dot_nt(iqst_s[...], ikt)
            r = jnp.maximum(logits, 0.0) * _lane_tile(iwb_s[...], tk)
            sc = r[0:tq]
            for h in range(1, IDX_HEADS):
                sc = sc + r[h * tq:(h + 1) * tq]
            sc = jnp.where(sc == 0.0, 0.0, sc)
            if masked:
                ck = ck_ref[0, :, pl.ds(off, tk)]
                sc = jnp.where(cq >= ck, sc, -jnp.inf)
            sc_s[:, pl.ds(off, tk)] = sc
            hi_s[:, pl.ds(off, tk)] = _hi16(sc)

        @pl.when(st == 1)
        def _():
            scores(False)

        @pl.when(st == 2)
        def _():
            scores(True)

        return carry

    lax.fori_loop(0, n_hi, score_body, 0)

    one_bf = jnp.ones((tq, LANES), BF16)
    zero_bf = jnp.zeros((tq, LANES), BF16)

    def count_packed(arr_s, pred, operand):
        def body(t, acc):
            off = pl.multiple_of(t * tk, tk)
            for c in range(tk // LANES):
                slab = arr_s[:, pl.ds(off + c * LANES, LANES)]
                acc = acc + jnp.where(pred(slab, operand), one_bf, zero_bf)
            return acc
        acc = lax.fori_loop(0, n_hi, body, zero_bf)
        total = jnp.sum(acc.astype(F32), axis=1, keepdims=True)
        return jnp.broadcast_to(total, (tq, LANES)).astype(I32)

    ge = lambda slab, c: slab >= c
    gt = lambda slab, c: slab > c
    eq = lambda slab, c: slab == c

    def hi_body(j, t16):
        cand = t16 + lax.shift_left(jnp.int32(1), 15 - j)
        cnt = count_packed(hi_s, ge, _pattern16_to_bf16(cand))
        return jnp.where(cnt >= top_k, cand, t16)

    t16 = lax.fori_loop(0, 16, hi_body, jnp.full((tq, LANES), -32768, I32))
    real = t16 > SORTABLE16_NEG_INF
    hi_thr = _pattern16_to_bf16(t16)
    k_left = top_k - count_packed(hi_s, gt, hi_thr)

    def digit_stage(shift, group_s, group_val, k_need):
        def build(t, carry):
            off = pl.multiple_of(t * tk, tk)
            dig = (_low_digits(sc_s[:, pl.ds(off, tk)]) >> shift) & 0xFF
            tied = group_s[:, pl.ds(off, tk)] == _lane_tile(group_val, tk)
            work_s[:, pl.ds(off, tk)] = jnp.where(tied, dig.astype(F32).astype(BF16),
                                                  jnp.full((tq, tk), -1.0, BF16))
            return carry
        lax.fori_loop(0, n_hi, build, 0)

        def body(j, cur):
            cand = cur + lax.shift_left(jnp.int32(1), 7 - j)
            cnt = count_packed(work_s, ge, cand.astype(F32).astype(BF16))
            return jnp.where(cnt >= k_need, cand, cur)
        cur = lax.fori_loop(0, 8, body, jnp.zeros((tq, LANES), I32))
        cur_bf = cur.astype(F32).astype(BF16)
        return cur, cur_bf, k_need - count_packed(work_s, gt, cur_bf)

    d1, d1_bf, k_left = digit_stage(8, hi_s, hi_thr, k_left)
    d0, d0_bf, need = digit_stage(0, work_s, d1_bf, k_left)
    cnt_eq = count_packed(work_s, eq, d0_bf)
    excess = jnp.logical_and(real, cnt_eq > need)
    pat16 = jnp.where(t16 < 0, t16 ^ 0x7FFF, t16) & 0xFFFF
    low = d1 * 256 + d0
    low = jnp.where((pat16 & 0x8000) != 0, low ^ 0xFFFF, low)
    thr = lax.bitcast_convert_type((pat16 << 16) | low, F32)
    thr = jnp.where(real, thr, -jnp.inf)
    thr_s[...] = thr
    jst_s[...] = jnp.where(real, jnp.int32(s_len), jnp.int32(-1))

    @pl.when(jnp.max(excess.astype(I32)) > 0)
    def _():
        rg = min(tq, LANES)
        lane_rg = lax.broadcasted_iota(I32, (rg, LANES), 1)

        def count_ties(bound):
            accs = []
            for r in range(tq // rg):
                rows = slice(r * rg, (r + 1) * rg)
                th_r, bound_r = thr[rows], bound[rows]

                def body(t, acc, rows=rows, th_r=th_r, bound_r=bound_r):
                    off = pl.multiple_of(t * tk, tk)
                    for c in range(tk // LANES):
                        slab = sc_s[rows, pl.ds(off + c * LANES, LANES)]
                        hit = jnp.logical_and(slab == th_r, (off + c * LANES + lane_rg) < bound_r)
                        acc = acc + jnp.where(hit, 1, 0).astype(I32)
                    return acc
                accs.append(lax.fori_loop(0, n_hi, body, jnp.zeros((rg, LANES), I32)))
            acc = accs[0] if len(accs) == 1 else jnp.concatenate(accs, axis=0)
            return jnp.broadcast_to(jnp.sum(acc, axis=1, keepdims=True), (tq, LANES))

        def idx_body(j, jcur):
            cand = jcur + lax.shift_left(jnp.int32(1), (s_len.bit_length() - 2) - j)
            return jnp.where(count_ties(cand) < need, cand, jcur)
        jbest = lax.fori_loop(0, s_len.bit_length() - 1, idx_body, jnp.zeros((tq, LANES), I32))
        jst_s[...] = jnp.where(excess, jbest, jst_s[...])

    m_s[...] = jnp.full(m_s.shape, NEG, F32)
    acc_s[...] = jnp.zeros(acc_s.shape, F32)
    lane_k = lax.broadcasted_iota(I32, (tq, tk), 1)
    lane_v = lax.broadcasted_iota(I32, (tk, LANES), 1)

    def attn_body(t, carry):
        st = status_ref[base + t]

        @pl.when(st != 0)
        def _():
            off = pl.multiple_of(t * tk, tk)
            sc = sc_s[:, pl.ds(off, tk)]
            th = _lane_tile(thr_s[...], tk)
            jst = _lane_tile(jst_s[...], tk)
            sel = jnp.logical_or(sc > th, jnp.logical_and(sc == th, (lane_k + off) <= jst))
            bias = jnp.where(sel, 0.0, NEG)
            kt = kk_ref[pl.ds(off, tk), :]
            vt = vv_ref[pl.ds(off, tk), :]
            vt = jnp.where(lane_v < HEAD_DIM, vt, jnp.ones_like(vt))
            s = _dot_nt(qst_s[...], kt)
            s = (s.reshape(DSA_HEADS, tq, tk) + bias[None]).reshape(DSA_HEADS * tq, tk)
            m_old = m_s[...]
            m_new = jnp.maximum(m_old, jnp.max(s, axis=1, keepdims=True))
            p = jnp.exp2(s - _lane_tile(m_new, tk))
            alpha = jnp.exp2(m_old - m_new)
            acc_s[...] = alpha * acc_s[...] + _dot(p.astype(BF16), vt)
            m_s[...] = m_new

        return carry

    lax.fori_loop(0, n_hi, attn_body, 0)

    for j in range(DSA_HEADS // 2):
        acc_a = acc_s[2 * j * tq:(2 * j + 1) * tq]
        acc_b = acc_s[(2 * j + 1) * tq:(2 * j + 2) * tq]
        swap_a = pltpu.roll(acc_a, HEAD_DIM, 1)
        swap_b = pltpu.roll(acc_b, HEAD_DIM, 1)
        o_ref[:, j * LANES:(j + 1) * LANES] = jnp.where(first, acc_a / swap_a,
                                                        swap_b / acc_b).astype(o_ref.dtype)


def _dsa_attention(q, iq, iw, kk, vv, ikk, cq3, ck3, status, nhi, b, s, tq, tk, top_k):
    nq, nk = s // tq, s // tk
    kern = functools.partial(_dsa_kernel, tq=tq, tk=tk, nq=nq, nk=nk, s_len=s, top_k=top_k)
    rowblk = lambda bb, i, *_: (bb * nq + i, 0)
    perb = lambda bb, i, *_: (bb, 0)
    grid_spec = pltpu.PrefetchScalarGridSpec(
        num_scalar_prefetch=2,
        grid=(b, nq),
        in_specs=[
            pl.BlockSpec((tq, SEQ_WIDTH), rowblk),
            pl.BlockSpec((tq, IDX_HEADS * IDX_DIM), rowblk),
            pl.BlockSpec((tq, LANES), rowblk),
            pl.BlockSpec((s, LANES), perb),
            pl.BlockSpec((s, LANES), perb),
            pl.BlockSpec((s, LANES), perb),
            pl.BlockSpec((1, tq, 1), lambda bb, i, *_: (bb * nq + i, 0, 0)),
            pl.BlockSpec((1, 1, s), lambda bb, i, *_: (bb, 0, 0)),
        ],
        out_specs=pl.BlockSpec((tq, SEQ_WIDTH), rowblk),
        scratch_shapes=[
            pltpu.VMEM((tq, s), F32),
            pltpu.VMEM((tq, s), BF16),
            pltpu.VMEM((tq, s), BF16),
            pltpu.VMEM((DSA_HEADS * tq, LANES), BF16),
            pltpu.VMEM((IDX_HEADS * tq, LANES), BF16),
            pltpu.VMEM((IDX_HEADS * tq, LANES), F32),
            pltpu.VMEM((DSA_HEADS * tq, LANES), F32),
            pltpu.VMEM((DSA_HEADS * tq, LANES), F32),
            pltpu.VMEM((tq, LANES), F32),
            pltpu.VMEM((tq, LANES), I32),
        ],
    )
    return pl.pallas_call(
        kern, grid_spec=grid_spec,
        out_shape=jax.ShapeDtypeStruct((b * s, SEQ_WIDTH), BF16),
        compiler_params=_params(("parallel", "arbitrary")), name="dsa_attn",
    )(status, nhi, q, iq, iw, kk, vv, ikk, cq3, ck3)


def _mix_out_kernel(x_ref, oseq_ref, qm_ref, km_ref, vm_ref, wo_ref, g_ref, o_ref):
    qm = qm_ref[...]
    km = km_ref[...]
    vm = vm_ref[...]
    lane = lax.broadcasted_iota(I32, qm.shape, 1)
    o_mem = jnp.zeros(qm.shape, F32)
    for h in range(MEM_HEADS):
        mine = (lane // HEAD_DIM) == h
        s = _dot_nt(jnp.where(mine, qm, jnp.zeros_like(qm)), km)
        p = jnp.exp(s - jnp.max(s, axis=1, keepdims=True))
        p = p / jnp.sum(p, axis=1, keepdims=True)
        o_mem = o_mem + jnp.where(mine, _dot(p.astype(BF16), vm), 0.0)
    y = _dot(oseq_ref[...], wo_ref[0:SEQ_WIDTH, :]) + _dot(o_mem.astype(BF16), wo_ref[SEQ_WIDTH:, :])
    o_ref[...] = x_ref[...] + _rms(y, g_ref[...])


def _mix_out(x2d, oseq, qm, kv, w_out, g, b, s, n_mem, tm):
    nt = s // tm
    return pl.pallas_call(
        _mix_out_kernel, grid=(b, nt),
        in_specs=[
            pl.BlockSpec((tm, D_MODEL), lambda bb, i: (bb * nt + i, 0)),
            pl.BlockSpec((tm, SEQ_WIDTH), lambda bb, i: (bb * nt + i, 0)),
            pl.BlockSpec((tm, MEM_WIDTH), lambda bb, i: (bb * nt + i, 0)),
            pl.BlockSpec((n_mem, MEM_WIDTH), lambda bb, i: (bb, 0)),
            pl.BlockSpec((n_mem, MEM_WIDTH), lambda bb, i: (bb, 1)),
            pl.BlockSpec((D_MODEL, D_MODEL), lambda bb, i: (0, 0)),
            pl.BlockSpec((1, D_MODEL), lambda bb, i: (0, 0)),
        ],
        out_specs=pl.BlockSpec((tm, D_MODEL), lambda bb, i: (bb * nt + i, 0)),
        out_shape=jax.ShapeDtypeStruct(x2d.shape, F32),
        compiler_params=_params(("parallel", "parallel")), name="mix_out",
    )(x2d, oseq, qm, kv, kv, w_out, g.reshape(1, D_MODEL))


def _mlp_kernel(x_ref, gpre_ref, wup_ref, wdn_ref, gpost_ref, o_ref, h_s, acc_s):
    j = pl.program_id(1)

    @pl.when(j == 0)
    def _():
        h_s[...] = _rms(x_ref[...], gpre_ref[...]).astype(BF16)
        acc_s[...] = jnp.zeros(acc_s.shape, F32)

    u = jnp.maximum(_dot(h_s[...], wup_ref[...]), 0.0)
    acc_s[...] += _dot((u * u).astype(BF16), wdn_ref[...])

    @pl.when(j == pl.num_programs(1) - 1)
    def _():
        o_ref[...] = x_ref[...] + _rms(acc_s[...], gpost_ref[...])


def _mlp(x2d, g_pre, w_up, w_dn, g_post, tm, tf):
    m = x2d.shape[0]
    return pl.pallas_call(
        _mlp_kernel, grid=(m // tm, D_FF // tf),
        in_specs=[
            pl.BlockSpec((tm, D_MODEL), lambda i, j: (i, 0)),
            pl.BlockSpec((1, D_MODEL), lambda i, j: (0, 0)),
            pl.BlockSpec((D_MODEL, tf), lambda i, j: (0, j)),
            pl.BlockSpec((tf, D_MODEL), lambda i, j: (j, 0)),
            pl.BlockSpec((1, D_MODEL), lambda i, j: (0, 0)),
        ],
        out_specs=pl.BlockSpec((tm, D_MODEL), lambda i, j: (i, 0)),
        out_shape=jax.ShapeDtypeStruct(x2d.shape, F32),
        scratch_shapes=[pltpu.VMEM((tm, D_MODEL), BF16), pltpu.VMEM((tm, D_MODEL), F32)],
        compiler_params=_params(("parallel", "arbitrary")), name="mlp",
    )(x2d, g_pre.reshape(1, D_MODEL), w_up, w_dn, g_post.reshape(1, D_MODEL))


def _rope_tables(positions):
    half = HEAD_DIM // 2
    inv = ROPE_THETA ** (-jnp.arange(0, HEAD_DIM, 2, dtype=F32) / HEAD_DIM)
    ang = positions.astype(F32).reshape(-1, 1) * inv
    cos, sin = jnp.cos(ang), jnp.sin(ang)
    reps = LANES // HEAD_DIM
    cos_t = jnp.tile(cos, (1, 2 * reps))
    sin_t = jnp.tile(jnp.concatenate([-sin, sin], axis=1), (1, reps))
    assert cos_t.shape[1] == LANES and half * 2 == HEAD_DIM
    return cos_t, sin_t


def _visibility(chunk_id, tq, tk):
    b, s = chunk_id.shape
    cq = chunk_id.reshape(b, s // tq, tq)
    ck = chunk_id.reshape(b, s // tk, tk)
    qmin, qmax = cq.min(-1)[:, :, None], cq.max(-1)[:, :, None]
    kmin, kmax = ck.min(-1)[:, None, :], ck.max(-1)[:, None, :]
    status = jnp.where(kmin > qmax, 0, jnp.where(kmax <= qmin, 1, 2)).astype(I32)
    tiles = jnp.arange(s // tk, dtype=I32)[None, None, :]
    nhi = jnp.max(jnp.where(status != 0, tiles + 1, 0), axis=-1).astype(I32)
    return status.reshape(-1), nhi.reshape(-1)


def _dup(w):
    return jnp.concatenate([w, w], axis=1)


def kernel(x, mem, positions, g_pre_mix, g_post_mix, g_mem, w_mem_kv, w_out, g_pre_mlp, g_post_mlp,
           w_mlp_up, w_mlp_down, w_in_diff, lambda_q1, lambda_k1, lambda_q2, lambda_k2, g_diff_subln,
           w_in_dsa):
    b, s, d = x.shape
    n_mem = mem.shape[1]
    depth = g_pre_mix.shape[0]
    m = b * s
    scale = HEAD_DIM ** -0.5
    scale2 = scale * math.log2(math.e)
    tm_proj = min(512, s)
    tq_diff, tk_diff = min(512, s), min(1024, s)
    tq_dsa, tk_dsa = min(256, s), min(512, s)
    tm_mix = min(512, s)
    tm_mlp, tf_mlp = min(1024, m), 1024
    top_k = min(DSA_TOPK_MAX, s // 4)
    assert tk_dsa >= top_k, "the threshold search needs one key tile to hold top_k candidates"

    cos_t, sin_t = _rope_tables(positions)
    chunk_id = positions // CHUNK
    cq3 = chunk_id.reshape(-1, 1)
    ck3 = chunk_id.reshape(b, 1, s)
    vis_diff = _visibility(chunk_id, tq_diff, tk_diff)
    vis_dsa = _visibility(chunk_id, tq_dsa, tk_dsa)

    x2d = x.reshape(m, d)
    mem2d = mem.reshape(b * n_mem, d)
    for i in range(depth):
        j = i // 2
        kv = _inproj(mem2d, g_mem[i], None, None, None, w_mem_kv[i].astype(BF16), (),
                     ((2 * MEM_WIDTH, 1.0, BF16),), min(256, b * n_mem))[0]
        if i % 2 == 0:
            w = w_in_diff[j].astype(BF16)
            nqk = 2 * DIFF_HEADS * HEAD_DIM
            q, k, v, qm = _inproj(
                x2d, g_pre_mix[i], cos_t, sin_t, w[:, :2 * nqk], w[:, 2 * nqk:],
                ((nqk, scale2, BF16), (nqk, 1.0, BF16)),
                ((SEQ_WIDTH, 1.0, BF16), (MEM_WIDTH, scale, BF16)), tm_proj)
            lam_p = jnp.stack([lambda_q1[j], lambda_k1[j], lambda_q2[j], lambda_k2[j]]).astype(F32)
            lam_init = 0.8 - 0.6 * math.exp(-0.3 * i)
            o_seq = _diff_attention(
                q, k, v, cq3.reshape(m // tq_diff, tq_diff, 1), ck3, vis_diff[0], vis_diff[1],
                lam_p, g_diff_subln[j].reshape(1, LANES), b, s, tq_diff, tk_diff, lam_init)
        else:
            w = w_in_dsa[j]
            o0 = 0
            parts = []
            for width in (DSA_HEADS * HEAD_DIM, HEAD_DIM, HEAD_DIM, IDX_HEADS * IDX_DIM, IDX_DIM,
                          IDX_HEADS, MEM_WIDTH):
                parts.append(w[:, o0:o0 + width])
                o0 += width
            wq, wk, wv, wiq, wik, wiw, wqm = parts
            wiw = jnp.pad(wiw, ((0, 0), (0, LANES - IDX_HEADS)))
            w_rope = jnp.concatenate([wq, wiq, _dup(wk), _dup(wik)], axis=1).astype(BF16)
            w_plain = jnp.concatenate([_dup(wv), wiw, wqm], axis=1).astype(BF16)
            iw_scale = IDX_HEADS ** -0.5 * IDX_DIM ** -0.5
            q, iq, kk, ikk, vv, iw, qm = _inproj(
                x2d, g_pre_mix[i], cos_t, sin_t, w_rope, w_plain,
                ((DSA_HEADS * HEAD_DIM, scale2, BF16), (IDX_HEADS * IDX_DIM, 1.0, BF16),
                 (LANES, 1.0, BF16), (LANES, 1.0, BF16)),
                ((LANES, 1.0, BF16), (LANES, iw_scale, F32), (MEM_WIDTH, scale, BF16)), tm_proj)
            o_seq = _dsa_attention(
                q, iq, iw, kk, vv, ikk, cq3.reshape(m // tq_dsa, tq_dsa, 1), ck3,
                vis_dsa[0], vis_dsa[1], b, s, tq_dsa, tk_dsa, top_k)
        x2d = _mix_out(x2d, o_seq, qm, kv, w_out[i].astype(BF16), g_post_mix[i], b, s, n_mem, tm_mix)
        x2d = _mlp(x2d, g_pre_mlp[i], w_mlp_up[i].astype(BF16), w_mlp_down[i].astype(BF16),
                   g_post_mlp[i], tm_mlp, tf_mlp)
    return x2d.reshape(b, s, d)
```

```python
import functools
import math

import jax
import jax.numpy as jnp
from jax import lax
from jax.experimental import pallas as pl
from jax.experimental.pallas import tpu as pltpu

D_MODEL = 1024
CHUNK = 64
HEAD_DIM = 64
LANES = 128
ROPE_THETA = 10000.0
EPS = 1e-6
MEM_HEADS = 4
MEM_WIDTH = MEM_HEADS * HEAD_DIM
SEQ_WIDTH = D_MODEL - MEM_WIDTH
DIFF_HEADS = SEQ_WIDTH // (2 * HEAD_DIM)
DSA_HEADS = SEQ_WIDTH // HEAD_DIM
IDX_HEADS = 8
IDX_DIM = 64
DSA_TOPK_MAX = 256
D_FF = 4 * D_MODEL

NEG = -1e30
INT_MIN = -(2 ** 31)
MASKED_KEY = -2139095041
VMEM_LIMIT = 56 * 1024 * 1024

F32 = jnp.float32
BF16 = jnp.bfloat16
I32 = jnp.int32


def _dot(a, b):
    return jnp.dot(a, b, preferred_element_type=F32)


def _dot_nt(a, b):
    return lax.dot_general(a, b, (((1,), (1,)), ((), ())), preferred_element_type=F32)


def _lane_tile(x, width):
    return jnp.concatenate([x] * (width // LANES), axis=1)


def _rms(x, g):
    return x * lax.rsqrt(jnp.mean(x * x, axis=-1, keepdims=True) + EPS) * g


def _params(sem):
    return pltpu.CompilerParams(dimension_semantics=sem, vmem_limit_bytes=VMEM_LIMIT)


def _inproj_kernel(*refs, rope_outs, plain_outs):
    n_r, n_p = len(rope_outs), len(plain_outs)
    x_ref, g_ref = refs[0], refs[1]
    pos = 2
    if n_r:
        cos_ref, sin_ref, wr_ref = refs[2], refs[3], refs[4]
        pos = 5
    if n_p:
        wp_ref = refs[pos]
        pos += 1
    out_refs = refs[pos:]
    x = x_ref[...]
    h = _rms(x, g_ref[...]).astype(BF16)
    oi = 0
    if n_r:
        cos = cos_ref[...]
        sin = sin_ref[...]
        lane = lax.broadcasted_iota(I32, cos.shape, 1)
        low = (lane % HEAD_DIM) < (HEAD_DIM // 2)
        c0 = 0
        for width, scale in rope_outs:
            o_ref = out_refs[oi]
            oi += 1
            for s in range(width // LANES):
                y = _dot(h, wr_ref[:, c0 + s * LANES:c0 + (s + 1) * LANES])
                swapped = jnp.where(low, pltpu.roll(y, LANES - HEAD_DIM // 2, 1),
                                    pltpu.roll(y, HEAD_DIM // 2, 1))
                r = y * cos + swapped * sin
                if scale != 1.0:
                    r = r * scale
                o_ref[:, s * LANES:(s + 1) * LANES] = r.astype(o_ref.dtype)
            c0 += width
    c0 = 0
    for width, scale in plain_outs:
        o_ref = out_refs[oi]
        oi += 1
        y = _dot(h, wp_ref[:, c0:c0 + width])
        if scale != 1.0:
            y = y * scale
        o_ref[...] = y.astype(o_ref.dtype)
        c0 += width


def _inproj(x2d, g, cos, sin, w_rope, w_plain, rope_outs, plain_outs, tm):
    m = x2d.shape[0]
    row = lambda i: (i, 0)
    fixed = lambda i: (0, 0)
    in_specs = [pl.BlockSpec((tm, D_MODEL), row), pl.BlockSpec((1, D_MODEL), fixed)]
    args = [x2d, g.reshape(1, D_MODEL)]
    if rope_outs:
        in_specs += [pl.BlockSpec((tm, LANES), row), pl.BlockSpec((tm, LANES), row),
                     pl.BlockSpec(w_rope.shape, fixed)]
        args += [cos, sin, w_rope]
    if plain_outs:
        in_specs += [pl.BlockSpec(w_plain.shape, fixed)]
        args += [w_plain]
    outs = tuple(rope_outs) + tuple(plain_outs)
    out_shape = [jax.ShapeDtypeStruct((m, w), dt) for w, _, dt in outs]
    out_specs = [pl.BlockSpec((tm, w), row) for w, _, _ in outs]
    kern = functools.partial(_inproj_kernel,
                             rope_outs=tuple((w, s) for w, s, _ in rope_outs),
                             plain_outs=tuple((w, s) for w, s, _ in plain_outs))
    return pl.pallas_call(
        kern, grid=(m // tm,), in_specs=in_specs, out_specs=out_specs, out_shape=out_shape,
        compiler_params=_params(("parallel",)), name="inproj")(*args)


def _diff_attn_kernel(status_ref, nhi_ref, q_ref, k_ref, v_ref, cq_ref, ck_ref, lam_ref, gsub_ref,
                      o_ref, m_s, acc_s, s0_s, s1_s, *, tq, tk, nq, nk, lam_init):
    b = pl.program_id(0)
    i = pl.program_id(2)
    q = q_ref[...]
    lane = lax.broadcasted_iota(I32, q.shape, 1)
    zero = jnp.zeros_like(q)
    qs = (jnp.where(lane < HEAD_DIM, q, zero), jnp.where(lane >= HEAD_DIM, q, zero))
    cq = cq_ref[0]
    m_s[...] = jnp.full(m_s.shape, NEG, F32)
    acc_s[...] = jnp.zeros(acc_s.shape, F32)
    base = (b * nq + i) * nk
    n_hi = nhi_ref[b * nq + i]

    def logits(t, dst_s):
        kt = k_ref[pl.ds(pl.multiple_of(t * tk, tk), tk), :]
        for j in range(2):
            dst_s[j] = _dot_nt(qs[j], kt)

    def tile(t, masked, cur_s, nxt_s):
        if nxt_s is not None:
            logits(t + 1, nxt_s)
        off = pl.multiple_of(t * tk, tk)
        vt = jnp.concatenate([v_ref[pl.ds(off, tk), :], jnp.ones((tk, LANES), BF16)], axis=1)
        if masked:
            ck = ck_ref[0, :, pl.ds(off, tk)]
            bias = jnp.where(cq >= ck, 0.0, NEG)
        for j in range(2):
            s = cur_s[j] + bias if masked else cur_s[j]
            m_old = m_s[j]
            m_new = jnp.maximum(m_old, jnp.max(s, axis=1, keepdims=True))
            p = jnp.exp2(s - _lane_tile(m_new, tk))
            alpha = jnp.exp2(m_old - m_new)
            acc_s[j] = _lane_tile(alpha, 2 * LANES) * acc_s[j] + _dot(p.astype(BF16), vt)
            m_s[j] = m_new

    def step(t, cur_s, nxt_s):
        st = status_ref[base + jnp.minimum(t, nk - 1)]
        live = t < n_hi
        more = t + 1 < n_hi

        @pl.when(jnp.logical_and(st == 0, more))
        def _():
            logits(t + 1, nxt_s)

        for code, masked in ((1, False), (2, True)):
            hit = jnp.logical_and(live, st == code)

            @pl.when(jnp.logical_and(hit, more))
            def _():
                tile(t, masked, cur_s, nxt_s)

            @pl.when(jnp.logical_and(hit, jnp.logical_not(more)))
            def _():
                tile(t, masked, cur_s, None)

    logits(0, s0_s)

    def body(u, carry):
        step(2 * u, s0_s, s1_s)
        step(2 * u + 1, s1_s, s0_s)
        return carry

    lax.fori_loop(0, (n_hi + 1) // 2, body, 0)

    lam_p = lam_ref[...]
    lam = (jnp.exp(jnp.sum(lam_p[0:1] * lam_p[1:2], axis=1, keepdims=True))
           - jnp.exp(jnp.sum(lam_p[2:3] * lam_p[3:4], axis=1, keepdims=True)) + lam_init)
    o = (acc_s[0, :, :LANES] / acc_s[0, :, LANES:]
         - lam * (acc_s[1, :, :LANES] / acc_s[1, :, LANES:]))
    o = _rms(o, gsub_ref[...]) * (1.0 - lam_init)
    o_ref[...] = o.astype(o_ref.dtype)


def _diff_attention(q, k, v, cq3, ck3, status, nhi, lam_p, gsub, b, s, tq, tk, lam_init):
    nq, nk = s // tq, s // tk
    kern = functools.partial(_diff_attn_kernel, tq=tq, tk=tk, nq=nq, nk=nk, lam_init=lam_init)
    grid_spec = pltpu.PrefetchScalarGridSpec(
        num_scalar_prefetch=2,
        grid=(b, DIFF_HEADS, nq),
        in_specs=[
            pl.BlockSpec((tq, LANES), lambda bb, h, i, *_: (bb * nq + i, h)),
            pl.BlockSpec((s, LANES), lambda bb, h, i, *_: (bb, h)),
            pl.BlockSpec((s, LANES), lambda bb, h, i, *_: (bb, h)),
            pl.BlockSpec((1, tq, 1), lambda bb, h, i, *_: (bb * nq + i, 0, 0)),
            pl.BlockSpec((1, 1, s), lambda bb, h, i, *_: (bb, 0, 0)),
            pl.BlockSpec((4, HEAD_DIM), lambda bb, h, i, *_: (0, 0)),
            pl.BlockSpec((1, LANES), lambda bb, h, i, *_: (0, 0)),
        ],
        out_specs=pl.BlockSpec((tq, LANES), lambda bb, h, i, *_: (bb * nq + i, h)),
        scratch_shapes=[pltpu.VMEM((2, tq, LANES), F32),
                        pltpu.VMEM((2, tq, 2 * LANES), F32),
                        pltpu.VMEM((2, tq, tk), F32),
                        pltpu.VMEM((2, tq, tk), F32)],
    )
    return pl.pallas_call(
        kern, grid_spec=grid_spec,
        out_shape=jax.ShapeDtypeStruct((b * s, SEQ_WIDTH), BF16),
        compiler_params=_params(("parallel", "parallel", "arbitrary")), name="diff_attn",
    )(status, nhi, q, k, v, cq3, ck3, lam_p, gsub)


def _dsa_kernel(status_ref, nhi_ref, q_ref, iq_ref, iw_ref, kk_ref, vv_ref, ikk_ref, cq_ref, ck_ref,
                o_ref, keys_s, qst_s, iqst_s, iwb_s, m_s, acc_s, thr_s, jst_s,
                *, tq, tk, nq, nk, s_len, top_k):
    b = pl.program_id(0)
    i = pl.program_id(1)
    base = (b * nq + i) * nk
    n_hi = nhi_ref[b * nq + i]
    cq = cq_ref[0]
    lane = lax.broadcasted_iota(I32, (tq, LANES), 1)
    first = lane < HEAD_DIM

    for h in range(DSA_HEADS):
        slab = q_ref[:, (h // 2) * LANES:(h // 2 + 1) * LANES]
        keep = first if h % 2 == 0 else jnp.logical_not(first)
        qst_s[h * tq:(h + 1) * tq, :] = jnp.where(keep, slab, jnp.zeros_like(slab))
    for h in range(IDX_HEADS):
        slab = iq_ref[:, (h // 2) * LANES:(h // 2 + 1) * LANES]
        keep = first if h % 2 == 0 else jnp.logical_not(first)
        iqst_s[h * tq:(h + 1) * tq, :] = jnp.where(keep, slab, jnp.zeros_like(slab))
        iwb_s[h * tq:(h + 1) * tq, :] = jnp.broadcast_to(iw_ref[:, h:h + 1], (tq, LANES))

    def score_body(t, carry):
        st = status_ref[base + t]
        off = pl.multiple_of(t * tk, tk)

        @pl.when(st == 0)
        def _():
            keys_s[:, pl.ds(off, tk)] = jnp.full((tq, tk), MASKED_KEY, I32)

        def scores(masked):
            ikt = ikk_ref[pl.ds(off, tk), :]
            logits = _dot_nt(iqst_s[...], ikt)
            r = jnp.maximum(logits, 0.0) * _lane_tile(iwb_s[...], tk)
            sc = r[0:tq]
            for h in range(1, IDX_HEADS):
                sc = sc + r[h * tq:(h + 1) * tq]
            bits = lax.bitcast_convert_type(sc, I32)
            key = jnp.where(bits < 0, bits ^ jnp.int32(0x7FFFFFFF), bits)
            if masked:
                ck = ck_ref[0, :, pl.ds(off, tk)]
                key = jnp.where(cq >= ck, key, jnp.int32(MASKED_KEY))
            keys_s[:, pl.ds(off, tk)] = key

        @pl.when(st == 1)
        def _():
            scores(False)

        @pl.when(st == 2)
        def _():
            scores(True)

        return carry

    lax.fori_loop(0, n_hi, score_body, 0)

    rg = min(tq, LANES)
    lane_rg = lax.broadcasted_iota(I32, (rg, LANES), 1)

    def count_tiles(pred, *operands):
        accs = []
        for r in range(tq // rg):
            rows = slice(r * rg, (r + 1) * rg)
            ops = [o[rows] for o in operands]

            def body(t, acc, rows=rows, ops=ops):
                off = pl.multiple_of(t * tk, tk)
                for c in range(tk // LANES):
                    slab = keys_s[rows, pl.ds(off + c * LANES, LANES)]
                    acc = acc + jnp.where(pred(slab, off + c * LANES, *ops), 1, 0).astype(I32)
                return acc
            accs.append(lax.fori_loop(0, n_hi, body, jnp.zeros((rg, LANES), I32)))
        acc = accs[0] if len(accs) == 1 else jnp.concatenate(accs, axis=0)
        return jnp.broadcast_to(jnp.sum(acc, axis=1, keepdims=True), (tq, LANES))

    def bit_body(j, thr):
        cand = thr + lax.shift_left(jnp.int32(1), 31 - j)
        cnt = count_tiles(lambda slab, col0, c: slab >= c, cand)
        return jnp.where(cnt >= top_k, cand, thr)

    thr = lax.fori_loop(0, 32, bit_body, jnp.full((tq, LANES), INT_MIN, I32))
    cnt_gt = count_tiles(lambda slab, col0, th: slab > th, thr)
    cnt_eq = count_tiles(lambda slab, col0, th: slab == th, thr)
    need = top_k - cnt_gt
    real = thr > MASKED_KEY
    excess = jnp.logical_and(real, cnt_eq > need)
    thr_s[...] = thr
    jst_s[...] = jnp.where(real, jnp.int32(s_len), jnp.int32(-1))

    @pl.when(jnp.max(excess.astype(I32)) > 0)
    def _():
        def idx_body(j, jcur):
            cand = jcur + lax.shift_left(jnp.int32(1), (s_len.bit_length() - 2) - j)
            cnt = count_tiles(
                lambda slab, col0, th, c: jnp.logical_and(slab == th, (col0 + lane_rg) < c), thr, cand)
            return jnp.where(cnt < need, cand, jcur)
        jbest = lax.fori_loop(0, s_len.bit_length() - 1, idx_body, jnp.zeros((tq, LANES), I32))
        jst_s[...] = jnp.where(excess, jbest, jst_s[...])

    m_s[...] = jnp.full(m_s.shape, NEG, F32)
    acc_s[...] = jnp.zeros(acc_s.shape, F32)
    lane_k = lax.broadcasted_iota(I32, (tq, tk), 1)
    lane_v = lax.broadcasted_iota(I32, (tk, LANES), 1)

    def attn_body(t, carry):
        st = status_ref[base + t]

        @pl.when(st != 0)
        def _():
            off = pl.multiple_of(t * tk, tk)
            key = keys_s[:, pl.ds(off, tk)]
            th = _lane_tile(thr_s[...], tk)
            jst = _lane_tile(jst_s[...], tk)
            sel = jnp.logical_or(key > th, jnp.logical_and(key == th, (lane_k + off) <= jst))
            bias = jnp.where(sel, 0.0, NEG)
            kt = kk_ref[pl.ds(off, tk), :]
            vt = vv_ref[pl.ds(off, tk), :]
            vt = jnp.where(lane_v < HEAD_DIM, vt, jnp.ones_like(vt))
            s = _dot_nt(qst_s[...], kt)
            s = (s.reshape(DSA_HEADS, tq, tk) + bias[None]).reshape(DSA_HEADS * tq, tk)
            m_old = m_s[...]
            m_new = jnp.maximum(m_old, jnp.max(s, axis=1, keepdims=True))
            p = jnp.exp2(s - _lane_tile(m_new, tk))
            alpha = jnp.exp2(m_old - m_new)
            acc_s[...] = alpha * acc_s[...] + _dot(p.astype(BF16), vt)
            m_s[...] = m_new

        return carry

    lax.fori_loop(0, n_hi, attn_body, 0)

    for j in range(DSA_HEADS // 2):
        acc_a = acc_s[2 * j * tq:(2 * j + 1) * tq]
        acc_b = acc_s[(2 * j + 1) * tq:(2 * j + 2) * tq]
        swap_a = pltpu.roll(acc_a, HEAD_DIM, 1)
        swap_b = pltpu.roll(acc_b, HEAD_DIM, 1)
        o_ref[:, j * LANES:(j + 1) * LANES] = jnp.where(first, acc_a / swap_a,
                                                        swap_b / acc_b).astype(o_ref.dtype)


def _dsa_attention(q, iq, iw, kk, vv, ikk, cq3, ck3, status, nhi, b, s, tq, tk, top_k):
    nq, nk = s // tq, s // tk
    kern = functools.partial(_dsa_kernel, tq=tq, tk=tk, nq=nq, nk=nk, s_len=s, top_k=top_k)
    rowblk = lambda bb, i, *_: (bb * nq + i, 0)
    perb = lambda bb, i, *_: (bb, 0)
    grid_spec = pltpu.PrefetchScalarGridSpec(
        num_scalar_prefetch=2,
        grid=(b, nq),
        in_specs=[
            pl.BlockSpec((tq, SEQ_WIDTH), rowblk),
            pl.BlockSpec((tq, IDX_HEADS * IDX_DIM), rowblk),
            pl.BlockSpec((tq, LANES), rowblk),
            pl.BlockSpec((s, LANES), perb),
            pl.BlockSpec((s, LANES), perb),
            pl.BlockSpec((s, LANES), perb),
            pl.BlockSpec((1, tq, 1), lambda bb, i, *_: (bb * nq + i, 0, 0)),
            pl.BlockSpec((1, 1, s), lambda bb, i, *_: (bb, 0, 0)),
        ],
        out_specs=pl.BlockSpec((tq, SEQ_WIDTH), rowblk),
        scratch_shapes=[
            pltpu.VMEM((tq, s), I32),
            pltpu.VMEM((DSA_HEADS * tq, LANES), BF16),
            pltpu.VMEM((IDX_HEADS * tq, LANES), BF16),
            pltpu.VMEM((IDX_HEADS * tq, LANES), F32),
            pltpu.VMEM((DSA_HEADS * tq, LANES), F32),
            pltpu.VMEM((DSA_HEADS * tq, LANES), F32),
            pltpu.VMEM((tq, LANES), I32),
            pltpu.VMEM((tq, LANES), I32),
        ],
    )
    return pl.pallas_call(
        kern, grid_spec=grid_spec,
        out_shape=jax.ShapeDtypeStruct((b * s, SEQ_WIDTH), BF16),
        compiler_params=_params(("parallel", "arbitrary")), name="dsa_attn",
    )(status, nhi, q, iq, iw, kk, vv, ikk, cq3, ck3)


def _mix_out_kernel(x_ref, oseq_ref, qm_ref, km_ref, vm_ref, wo_ref, g_ref, o_ref):
    qm = qm_ref[...]
    km = km_ref[...]
    vm = vm_ref[...]
    lane = lax.broadcasted_iota(I32, qm.shape, 1)
    o_mem = jnp.zeros(qm.shape, F32)
    for h in range(MEM_HEADS):
        mine = (lane // HEAD_DIM) == h
        s = _dot_nt(jnp.where(mine, qm, jnp.zeros_like(qm)), km)
        p = jnp.exp(s - jnp.max(s, axis=1, keepdims=True))
        p = p / jnp.sum(p, axis=1, keepdims=True)
        o_mem = o_mem + jnp.where(mine, _dot(p.astype(BF16), vm), 0.0)
    y = _dot(oseq_ref[...], wo_ref[0:SEQ_WIDTH, :]) + _dot(o_mem.astype(BF16), wo_ref[SEQ_WIDTH:, :])
    o_ref[...] = x_ref[...] + _rms(y, g_ref[...])


def _mix_out(x2d, oseq, qm, kv, w_out, g, b, s, n_mem, tm):
    nt = s // tm
    return pl.pallas_call(
        _mix_out_kernel, grid=(b, nt),
        in_specs=[
            pl.BlockSpec((tm, D_MODEL), lambda bb, i: (bb * nt + i, 0)),
            pl.BlockSpec((tm, SEQ_WIDTH), lambda bb, i: (bb * nt + i, 0)),
            pl.BlockSpec((tm, MEM_WIDTH), lambda bb, i: (bb * nt + i, 0)),
            pl.BlockSpec((n_mem, MEM_WIDTH), lambda bb, i: (bb, 0)),
            pl.BlockSpec((n_mem, MEM_WIDTH), lambda bb, i: (bb, 1)),
            pl.BlockSpec((D_MODEL, D_MODEL), lambda bb, i: (0, 0)),
            pl.BlockSpec((1, D_MODEL), lambda bb, i: (0, 0)),
        ],
        out_specs=pl.BlockSpec((tm, D_MODEL), lambda bb, i: (bb * nt + i, 0)),
        out_shape=jax.ShapeDtypeStruct(x2d.shape, F32),
        compiler_params=_params(("parallel", "parallel")), name="mix_out",
    )(x2d, oseq, qm, kv, kv, w_out, g.reshape(1, D_MODEL))


def _mlp_kernel(x_ref, gpre_ref, wup_ref, wdn_ref, gpost_ref, o_ref, h_s, acc_s):
    j = pl.program_id(1)

    @pl.when(j == 0)
    def _():
        h_s[...] = _rms(x_ref[...], gpre_ref[...]).astype(BF16)
        acc_s[...] = jnp.zeros(acc_s.shape, F32)

    u = jnp.maximum(_dot(h_s[...], wup_ref[...]), 0.0)
    acc_s[...] += _dot((u * u).astype(BF16), wdn_ref[...])

    @pl.when(j == pl.num_programs(1) - 1)
    def _():
        o_ref[...] = x_ref[...] + _rms(acc_s[...], gpost_ref[...])


def _mlp(x2d, g_pre, w_up, w_dn, g_post, tm, tf):
    m = x2d.shape[0]
    return pl.pallas_call(
        _mlp_kernel, grid=(m // tm, D_FF // tf),
        in_specs=[
            pl.BlockSpec((tm, D_MODEL), lambda i, j: (i, 0)),
            pl.BlockSpec((1, D_MODEL), lambda i, j: (0, 0)),
            pl.BlockSpec((D_MODEL, tf), lambda i, j: (0, j)),
            pl.BlockSpec((tf, D_MODEL), lambda i, j: (j, 0)),
            pl.BlockSpec((1, D_MODEL), lambda i, j: (0, 0)),
        ],
        out_specs=pl.BlockSpec((tm, D_MODEL), lambda i, j: (i, 0)),
        out_shape=jax.ShapeDtypeStruct(x2d.shape, F32),
        scratch_shapes=[pltpu.VMEM((tm, D_MODEL), BF16), pltpu.VMEM((tm, D_MODEL), F32)],
        compiler_params=_params(("parallel", "arbitrary")), name="mlp",
    )(x2d, g_pre.reshape(1, D_MODEL), w_up, w_dn, g_post.reshape(1, D_MODEL))


def _rope_tables(positions):
    half = HEAD_DIM // 2
    inv = ROPE_THETA ** (-jnp.arange(0, HEAD_DIM, 2, dtype=F32) / HEAD_DIM)
    ang = positions.astype(F32).reshape(-1, 1) * inv
    cos, sin = jnp.cos(ang), jnp.sin(ang)
    reps = LANES // HEAD_DIM
    cos_t = jnp.tile(cos, (1, 2 * reps))
    sin_t = jnp.tile(jnp.concatenate([-sin, sin], axis=1), (1, reps))
    assert cos_t.shape[1] == LANES and half * 2 == HEAD_DIM
    return cos_t, sin_t


def _visibility(chunk_id, tq, tk):
    b, s = chunk_id.shape
    cq = chunk_id.reshape(b, s // tq, tq)
    ck = chunk_id.reshape(b, s // tk, tk)
    qmin, qmax = cq.min(-1)[:, :, None], cq.max(-1)[:, :, None]
    kmin, kmax = ck.min(-1)[:, None, :], ck.max(-1)[:, None, :]
    status = jnp.where(kmin > qmax, 0, jnp.where(kmax <= qmin, 1, 2)).astype(I32)
    tiles = jnp.arange(s // tk, dtype=I32)[None, None, :]
    nhi = jnp.max(jnp.where(status != 0, tiles + 1, 0), axis=-1).astype(I32)
    return status.reshape(-1), nhi.reshape(-1)


def _dup(w):
    return jnp.concatenate([w, w], axis=1)


def kernel(x, mem, positions, g_pre_mix, g_post_mix, g_mem, w_mem_kv, w_out, g_pre_mlp, g_post_mlp,
           w_mlp_up, w_mlp_down, w_in_diff, lambda_q1, lambda_k1, lambda_q2, lambda_k2, g_diff_subln,
           w_in_dsa):
    b, s, d = x.shape
    n_mem = mem.shape[1]
    depth = g_pre_mix.shape[0]
    m = b * s
    scale = HEAD_DIM ** -0.5
    scale2 = scale * math.log2(math.e)
    tm_proj = min(512, s)
    tq_diff, tk_diff = min(512, s), min(1024, s)
    tq_dsa, tk_dsa = min(256, s), min(512, s)
    tm_mix = min(512, s)
    tm_mlp, tf_mlp = min(1024, m), 1024
    top_k = min(DSA_TOPK_MAX, s // 4)
    assert tk_dsa >= top_k, "the threshold search needs one key tile to hold top_k candidates"

    cos_t, sin_t = _rope_tables(positions)
    chunk_id = positions // CHUNK
    cq3 = chunk_id.reshape(-1, 1)
    ck3 = chunk_id.reshape(b, 1, s)
    vis_diff = _visibility(chunk_id, tq_diff, tk_diff)
    vis_dsa = _visibility(chunk_id, tq_dsa, tk_dsa)

    x2d = x.reshape(m, d)
    mem2d = mem.reshape(b * n_mem, d)
    for i in range(depth):
        j = i // 2
        kv = _inproj(mem2d, g_mem[i], None, None, None, w_mem_kv[i].astype(BF16), (),
                     ((2 * MEM_WIDTH, 1.0, BF16),), min(256, b * n_mem))[0]
        if i % 2 == 0:
            w = w_in_diff[j].astype(BF16)
            nqk = 2 * DIFF_HEADS * HEAD_DIM
            q, k, v, qm = _inproj(
                x2d, g_pre_mix[i], cos_t, sin_t, w[:, :2 * nqk], w[:, 2 * nqk:],
                ((nqk, scale2, BF16), (nqk, 1.0, BF16)),
                ((SEQ_WIDTH, 1.0, BF16), (MEM_WIDTH, scale, BF16)), tm_proj)
            lam_p = jnp.stack([lambda_q1[j], lambda_k1[j], lambda_q2[j], lambda_k2[j]]).astype(F32)
            lam_init = 0.8 - 0.6 * math.exp(-0.3 * i)
            o_seq = _diff_attention(
                q, k, v, cq3.reshape(m // tq_diff, tq_diff, 1), ck3, vis_diff[0], vis_diff[1],
                lam_p, g_diff_subln[j].reshape(1, LANES), b, s, tq_diff, tk_diff, lam_init)
        else:
            w = w_in_dsa[j]
            o0 = 0
            parts = []
            for width in (DSA_HEADS * HEAD_DIM, HEAD_DIM, HEAD_DIM, IDX_HEADS * IDX_DIM, IDX_DIM,
                          IDX_HEADS, MEM_WIDTH):
                parts.append(w[:, o0:o0 + width])
                o0 += width
            wq, wk, wv, wiq, wik, wiw, wqm = parts
            wiw = jnp.pad(wiw, ((0, 0), (0, LANES - IDX_HEADS)))
            w_rope = jnp.concatenate([wq, wiq, _dup(wk), _dup(wik)], axis=1).astype(BF16)
            w_plain = jnp.concatenate([_dup(wv), wiw, wqm], axis=1).astype(BF16)
            iw_scale = IDX_HEADS ** -0.5 * IDX_DIM ** -0.5
            q, iq, kk, ikk, vv, iw, qm = _inproj(
                x2d, g_pre_mix[i], cos_t, sin_t, w_rope, w_plain,
                ((DSA_HEADS * HEAD_DIM, scale2, BF16), (IDX_HEADS * IDX_DIM, 1.0, BF16),
                 (LANES, 1.0, BF16), (LANES, 1.0, BF16)),
                ((LANES, 1.0, BF16), (LANES, iw_scale, F32), (MEM_WIDTH, scale, BF16)), tm_proj)
            o_seq = _dsa_attention(
                q, iq, iw, kk, vv, ikk, cq3.reshape(m // tq_dsa, tq_dsa, 1), ck3,
                vis_dsa[0], vis_dsa[1], b, s, tq_dsa, tk_dsa, top_k)
        x2d = _mix_out(x2d, o_seq, qm, kv, w_out[i].astype(BF16), g_post_mix[i], b, s, n_mem, tm_mix)
        x2d = _mlp(x2d, g_pre_mlp[i], w_mlp_up[i].astype(BF16), w_mlp_down[i].astype(BF16),
                   g_post_mlp[i], tm_mlp, tf_mlp)
    return x2d.reshape(b, s, d)
```

```python
import functools
import math

import jax
import jax.numpy as jnp
from jax import lax
from jax.experimental import pallas as pl
from jax.experimental.pallas import tpu as pltpu

D_MODEL = 1024
CHUNK = 64
HEAD_DIM = 64
LANES = 128
ROPE_THETA = 10000.0
EPS = 1e-6
MEM_HEADS = 4
MEM_WIDTH = MEM_HEADS * HEAD_DIM
SEQ_WIDTH = D_MODEL - MEM_WIDTH
DIFF_HEADS = SEQ_WIDTH // (2 * HEAD_DIM)
DSA_HEADS = SEQ_WIDTH // HEAD_DIM
IDX_HEADS = 8
IDX_DIM = 64
DSA_TOPK_MAX = 256
D_FF = 4 * D_MODEL

NEG = -1e30
INT_MIN = -(2 ** 31)
MASKED_KEY = -2139095041
VMEM_LIMIT = 56 * 1024 * 1024

F32 = jnp.float32
BF16 = jnp.bfloat16
I32 = jnp.int32


def _dot(a, b):
    return jnp.dot(a, b, preferred_element_type=F32)


def _dot_nt(a, b):
    return lax.dot_general(a, b, (((1,), (1,)), ((), ())), preferred_element_type=F32)


def _lane_tile(x, width):
    return jnp.concatenate([x] * (width // LANES), axis=1)


def _rms(x, g):
    return x * lax.rsqrt(jnp.mean(x * x, axis=-1, keepdims=True) + EPS) * g


def _params(sem):
    return pltpu.CompilerParams(dimension_semantics=sem, vmem_limit_bytes=VMEM_LIMIT)


def _inproj_kernel(*refs, rope_outs, plain_outs):
    n_r, n_p = len(rope_outs), len(plain_outs)
    x_ref, g_ref = refs[0], refs[1]
    pos = 2
    if n_r:
        cos_ref, sin_ref, wr_ref = refs[2], refs[3], refs[4]
        pos = 5
    if n_p:
        wp_ref = refs[pos]
        pos += 1
    out_refs = refs[pos:]
    x = x_ref[...]
    h = _rms(x, g_ref[...]).astype(BF16)
    oi = 0
    if n_r:
        cos = cos_ref[...]
        sin = sin_ref[...]
        lane = lax.broadcasted_iota(I32, cos.shape, 1)
        low = (lane % HEAD_DIM) < (HEAD_DIM // 2)
        c0 = 0
        for width, scale in rope_outs:
            o_ref = out_refs[oi]
            oi += 1
            y_all = _dot(h, wr_ref[:, c0:c0 + width])
            for s in range(width // LANES):
                y = y_all[:, s * LANES:(s + 1) * LANES]
                swapped = jnp.where(low, pltpu.roll(y, LANES - HEAD_DIM // 2, 1),
                                    pltpu.roll(y, HEAD_DIM // 2, 1))
                r = y * cos + swapped * sin
                if scale != 1.0:
                    r = r * scale
                o_ref[:, s * LANES:(s + 1) * LANES] = r.astype(o_ref.dtype)
            c0 += width
    c0 = 0
    for width, scale in plain_outs:
        o_ref = out_refs[oi]
        oi += 1
        y = _dot(h, wp_ref[:, c0:c0 + width])
        if scale != 1.0:
            y = y * scale
        o_ref[...] = y.astype(o_ref.dtype)
        c0 += width


def _inproj(x2d, g, cos, sin, w_rope, w_plain, rope_outs, plain_outs, tm):
    m = x2d.shape[0]
    row = lambda i: (i, 0)
    fixed = lambda i: (0, 0)
    in_specs = [pl.BlockSpec((tm, D_MODEL), row), pl.BlockSpec((1, D_MODEL), fixed)]
    args = [x2d, g.reshape(1, D_MODEL)]
    if rope_outs:
        in_specs += [pl.BlockSpec((tm, LANES), row), pl.BlockSpec((tm, LANES), row),
                     pl.BlockSpec(w_rope.shape, fixed)]
        args += [cos, sin, w_rope]
    if plain_outs:
        in_specs += [pl.BlockSpec(w_plain.shape, fixed)]
        args += [w_plain]
    outs = tuple(rope_outs) + tuple(plain_outs)
    out_shape = [jax.ShapeDtypeStruct((m, w), dt) for w, _, dt in outs]
    out_specs = [pl.BlockSpec((tm, w), row) for w, _, _ in outs]
    kern = functools.partial(_inproj_kernel,
                             rope_outs=tuple((w, s) for w, s, _ in rope_outs),
                             plain_outs=tuple((w, s) for w, s, _ in plain_outs))
    return pl.pallas_call(
        kern, grid=(m // tm,), in_specs=in_specs, out_specs=out_specs, out_shape=out_shape,
        compiler_params=_params(("parallel",)), name="inproj")(*args)


def _diff_attn_kernel(status_ref, nhi_ref, q_ref, k_ref, v_ref, cq_ref, ck_ref, lam_ref, gsub_ref,
                      o_ref, m_s, acc_s, s0_s, s1_s, *, tq, tk, nq, nk, lam_init):
    b = pl.program_id(0)
    i = pl.program_id(2)
    q = q_ref[...]
    lane = lax.broadcasted_iota(I32, q.shape, 1)
    zero = jnp.zeros_like(q)
    qs = (jnp.where(lane < HEAD_DIM, q, zero), jnp.where(lane >= HEAD_DIM, q, zero))
    cq = cq_ref[0]
    m_s[...] = jnp.full(m_s.shape, NEG, F32)
    acc_s[...] = jnp.zeros(acc_s.shape, F32)
    base = (b * nq + i) * nk
    n_hi = nhi_ref[b * nq + i]

    def logits(t, dst_s):
        kt = k_ref[pl.ds(pl.multiple_of(t * tk, tk), tk), :]
        for j in range(2):
            dst_s[j] = _dot_nt(qs[j], kt)

    def tile(t, masked, cur_s, nxt_s):
        if nxt_s is not None:
            logits(t + 1, nxt_s)
        off = pl.multiple_of(t * tk, tk)
        vt = jnp.concatenate([v_ref[pl.ds(off, tk), :], jnp.ones((tk, LANES), BF16)], axis=1)
        if masked:
            ck = ck_ref[0, :, pl.ds(off, tk)]
            bias = jnp.where(cq >= ck, 0.0, NEG)
        for j in range(2):
            s = cur_s[j] + bias if masked else cur_s[j]
            m_old = m_s[j]
            m_new = jnp.maximum(m_old, jnp.max(s, axis=1, keepdims=True))
            p = jnp.exp2(s - _lane_tile(m_new, tk))
            alpha = jnp.exp2(m_old - m_new)
            acc_s[j] = _lane_tile(alpha, 2 * LANES) * acc_s[j] + _dot(p.astype(BF16), vt)
            m_s[j] = m_new

    def step(t, cur_s, nxt_s):
        st = status_ref[base + jnp.minimum(t, nk - 1)]
        live = t < n_hi
        more = t + 1 < n_hi

        @pl.when(jnp.logical_and(st == 0, more))
        def _():
            logits(t + 1, nxt_s)

        for code, masked in ((1, False), (2, True)):
            hit = jnp.logical_and(live, st == code)

            @pl.when(jnp.logical_and(hit, more))
            def _():
                tile(t, masked, cur_s, nxt_s)

            @pl.when(jnp.logical_and(hit, jnp.logical_not(more)))
            def _():
                tile(t, masked, cur_s, None)

    logits(0, s0_s)

    def body(u, carry):
        step(2 * u, s0_s, s1_s)
        step(2 * u + 1, s1_s, s0_s)
        return carry

    lax.fori_loop(0, (n_hi + 1) // 2, body, 0)

    lam_p = lam_ref[...]
    lam = (jnp.exp(jnp.sum(lam_p[0:1] * lam_p[1:2], axis=1, keepdims=True))
           - jnp.exp(jnp.sum(lam_p[2:3] * lam_p[3:4], axis=1, keepdims=True)) + lam_init)
    o = (acc_s[0, :, :LANES] / acc_s[0, :, LANES:]
         - lam * (acc_s[1, :, :LANES] / acc_s[1, :, LANES:]))
    o = _rms(o, gsub_ref[...]) * (1.0 - lam_init)
    o_ref[...] = o.astype(o_ref.dtype)


def _diff_attention(q, k, v, cq3, ck3, status, nhi, lam_p, gsub, b, s, tq, tk, lam_init):
    nq, nk = s // tq, s // tk
    kern = functools.partial(_diff_attn_kernel, tq=tq, tk=tk, nq=nq, nk=nk, lam_init=lam_init)
    grid_spec = pltpu.PrefetchScalarGridSpec(
        num_scalar_prefetch=2,
        grid=(b, DIFF_HEADS, nq),
        in_specs=[
            pl.BlockSpec((tq, LANES), lambda bb, h, i, *_: (bb * nq + i, h)),
            pl.BlockSpec((s, LANES), lambda bb, h, i, *_: (bb, h)),
            pl.BlockSpec((s, LANES), lambda bb, h, i, *_: (bb, h)),
            pl.BlockSpec((1, tq, 1), lambda bb, h, i, *_: (bb * nq + i, 0, 0)),
            pl.BlockSpec((1, 1, s), lambda bb, h, i, *_: (bb, 0, 0)),
            pl.BlockSpec((4, HEAD_DIM), lambda bb, h, i, *_: (0, 0)),
            pl.BlockSpec((1, LANES), lambda bb, h, i, *_: (0, 0)),
        ],
        out_specs=pl.BlockSpec((tq, LANES), lambda bb, h, i, *_: (bb * nq + i, h)),
        scratch_shapes=[pltpu.VMEM((2, tq, LANES), F32),
                        pltpu.VMEM((2, tq, 2 * LANES), F32),
                        pltpu.VMEM((2, tq, tk), F32),
                        pltpu.VMEM((2, tq, tk), F32)],
    )
    return pl.pallas_call(
        kern, grid_spec=grid_spec,
        out_shape=jax.ShapeDtypeStruct((b * s, SEQ_WIDTH), BF16),
        compiler_params=_params(("parallel", "parallel", "arbitrary")), name="diff_attn",
    )(status, nhi, q, k, v, cq3, ck3, lam_p, gsub)


def _dsa_kernel(status_ref, nhi_ref, q_ref, iq_ref, iw_ref, kk_ref, vv_ref, ikk_ref, cq_ref, ck_ref,
                o_ref, keys_s, qst_s, iqst_s, iwb_s, m_s, acc_s, thr_s, jst_s,
                *, tq, tk, nq, nk, s_len, top_k):
    b = pl.program_id(0)
    i = pl.program_id(1)
    base = (b * nq + i) * nk
    n_hi = nhi_ref[b * nq + i]
    cq = cq_ref[0]
    lane = lax.broadcasted_iota(I32, (tq, LANES), 1)
    first = lane < HEAD_DIM

    for h in range(DSA_HEADS):
        slab = q_ref[:, (h // 2) * LANES:(h // 2 + 1) * LANES]
        keep = first if h % 2 == 0 else jnp.logical_not(first)
        qst_s[h * tq:(h + 1) * tq, :] = jnp.where(keep, slab, jnp.zeros_like(slab))
    for h in range(IDX_HEADS):
        slab = iq_ref[:, (h // 2) * LANES:(h // 2 + 1) * LANES]
        keep = first if h % 2 == 0 else jnp.logical_not(first)
        iqst_s[h * tq:(h + 1) * tq, :] = jnp.where(keep, slab, jnp.zeros_like(slab))
        iwb_s[h * tq:(h + 1) * tq, :] = jnp.broadcast_to(iw_ref[:, h:h + 1], (tq, LANES))

    def score_body(t, carry):
        st = status_ref[base + t]
        off = pl.multiple_of(t * tk, tk)

        @pl.when(st == 0)
        def _():
            keys_s[:, pl.ds(off, tk)] = jnp.full((tq, tk), MASKED_KEY, I32)

        def scores(masked):
            ikt = ikk_ref[pl.ds(off, tk), :]
            logits = _dot_nt(iqst_s[...], ikt)
            r = jnp.maximum(logits, 0.0) * _lane_tile(iwb_s[...], tk)
            sc = r[0:tq]
            for h in range(1, IDX_HEADS):
                sc = sc + r[h * tq:(h + 1) * tq]
            bits = lax.bitcast_convert_type(sc, I32)
            key = jnp.where(bits < 0, bits ^ jnp.int32(0x7FFFFFFF), bits)
            if masked:
                ck = ck_ref[0, :, pl.ds(off, tk)]
                key = jnp.where(cq >= ck, key, jnp.int32(MASKED_KEY))
            keys_s[:, pl.ds(off, tk)] = key

        @pl.when(st == 1)
        def _():
            scores(False)

        @pl.when(st == 2)
        def _():
            scores(True)

        return carry

    lax.fori_loop(0, n_hi, score_body, 0)

    rg = min(tq, LANES)
    lane_rg = lax.broadcasted_iota(I32, (rg, LANES), 1)

    def count_tiles(preds, *operands):
        zero = jnp.zeros((rg, LANES), I32)
        accs = []
        for r in range(tq // rg):
            rows = slice(r * rg, (r + 1) * rg)
            ops = [o[rows] for o in operands]

            def body(t, acc, rows=rows, ops=ops):
                off = pl.multiple_of(t * tk, tk)
                acc = list(acc)
                for c in range(tk // LANES):
                    slab = keys_s[rows, pl.ds(off + c * LANES, LANES)]
                    for n, pred in enumerate(preds):
                        acc[n] = acc[n] + jnp.where(pred(slab, off + c * LANES, *ops), 1, 0).astype(I32)
                return tuple(acc)
            accs.append(lax.fori_loop(0, n_hi, body, (zero,) * len(preds)))
        outs = []
        for n in range(len(preds)):
            acc = accs[0][n] if len(accs) == 1 else jnp.concatenate([a[n] for a in accs], axis=0)
            outs.append(jnp.broadcast_to(jnp.sum(acc, axis=1, keepdims=True), (tq, LANES)))
        return outs

    def bit_body(j, thr):
        cand = thr + lax.shift_left(jnp.int32(1), 31 - j)
        cnt, = count_tiles([lambda slab, col0, c: slab >= c], cand)
        return jnp.where(cnt >= top_k, cand, thr)

    thr = lax.fori_loop(0, 32, bit_body, jnp.full((tq, LANES), INT_MIN, I32))
    cnt_gt, cnt_eq = count_tiles([lambda slab, col0, th: slab > th,
                                  lambda slab, col0, th: slab == th], thr)
    need = top_k - cnt_gt
    real = thr > MASKED_KEY
    excess = jnp.logical_and(real, cnt_eq > need)
    thr_s[...] = thr
    jst_s[...] = jnp.where(real, jnp.int32(s_len), jnp.int32(-1))

    @pl.when(jnp.max(excess.astype(I32)) > 0)
    def _():
        def idx_body(j, jcur):
            cand = jcur + lax.shift_left(jnp.int32(1), (s_len.bit_length() - 2) - j)
            cnt, = count_tiles(
                [lambda slab, col0, th, c: jnp.logical_and(slab == th, (col0 + lane_rg) < c)], thr, cand)
            return jnp.where(cnt < need, cand, jcur)
        jbest = lax.fori_loop(0, s_len.bit_length() - 1, idx_body, jnp.zeros((tq, LANES), I32))
        jst_s[...] = jnp.where(excess, jbest, jst_s[...])

    m_s[...] = jnp.full(m_s.shape, NEG, F32)
    acc_s[...] = jnp.zeros(acc_s.shape, F32)
    lane_k = lax.broadcasted_iota(I32, (tq, tk), 1)
    lane_v = lax.broadcasted_iota(I32, (tk, LANES), 1)

    def attn_body(t, carry):
        st = status_ref[base + t]

        @pl.when(st != 0)
        def _():
            off = pl.multiple_of(t * tk, tk)
            key = keys_s[:, pl.ds(off, tk)]
            th = _lane_tile(thr_s[...], tk)
            jst = _lane_tile(jst_s[...], tk)
            sel = jnp.logical_or(key > th, jnp.logical_and(key == th, (lane_k + off) <= jst))
            bias = jnp.where(sel, 0.0, NEG)
            kt = kk_ref[pl.ds(off, tk), :]
            vt = vv_ref[pl.ds(off, tk), :]
            vt = jnp.where(lane_v < HEAD_DIM, vt, jnp.ones_like(vt))
            s = _dot_nt(qst_s[...], kt)
            s = (s.reshape(DSA_HEADS, tq, tk) + bias[None]).reshape(DSA_HEADS * tq, tk)
            m_old = m_s[...]
            m_new = jnp.maximum(m_old, jnp.max(s, axis=1, keepdims=True))
            p = jnp.exp2(s - _lane_tile(m_new, tk))
            alpha = jnp.exp2(m_old - m_new)
            acc_s[...] = alpha * acc_s[...] + _dot(p.astype(BF16), vt)
            m_s[...] = m_new

        return carry

    lax.fori_loop(0, n_hi, attn_body, 0)

    for j in range(DSA_HEADS // 2):
        acc_a = acc_s[2 * j * tq:(2 * j + 1) * tq]
        acc_b = acc_s[(2 * j + 1) * tq:(2 * j + 2) * tq]
        swap_a = pltpu.roll(acc_a, HEAD_DIM, 1)
        swap_b = pltpu.roll(acc_b, HEAD_DIM, 1)
        o_ref[:, j * LANES:(j + 1) * LANES] = jnp.where(first, acc_a / swap_a,
                                                        swap_b / acc_b).astype(o_ref.dtype)


def _dsa_attention(q, iq, iw, kk, vv, ikk, cq3, ck3, status, nhi, b, s, tq, tk, top_k):
    nq, nk = s // tq, s // tk
    kern = functools.partial(_dsa_kernel, tq=tq, tk=tk, nq=nq, nk=nk, s_len=s, top_k=top_k)
    rowblk = lambda bb, i, *_: (bb * nq + i, 0)
    perb = lambda bb, i, *_: (bb, 0)
    grid_spec = pltpu.PrefetchScalarGridSpec(
        num_scalar_prefetch=2,
        grid=(b, nq),
        in_specs=[
            pl.BlockSpec((tq, SEQ_WIDTH), rowblk),
            pl.BlockSpec((tq, IDX_HEADS * IDX_DIM), rowblk),
            pl.BlockSpec((tq, LANES), rowblk),
            pl.BlockSpec((s, LANES), perb),
            pl.BlockSpec((s, LANES), perb),
            pl.BlockSpec((s, LANES), perb),
            pl.BlockSpec((1, tq, 1), lambda bb, i, *_: (bb * nq + i, 0, 0)),
            pl.BlockSpec((1, 1, s), lambda bb, i, *_: (bb, 0, 0)),
        ],
        out_specs=pl.BlockSpec((tq, SEQ_WIDTH), rowblk),
        scratch_shapes=[
            pltpu.VMEM((tq, s), I32),
            pltpu.VMEM((DSA_HEADS * tq, LANES), BF16),
            pltpu.VMEM((IDX_HEADS * tq, LANES), BF16),
            pltpu.VMEM((IDX_HEADS * tq, LANES), F32),
            pltpu.VMEM((DSA_HEADS * tq, LANES), F32),
            pltpu.VMEM((DSA_HEADS * tq, LANES), F32),
            pltpu.VMEM((tq, LANES), I32),
            pltpu.VMEM((tq, LANES), I32),
        ],
    )
    return pl.pallas_call(
        kern, grid_spec=grid_spec,
        out_shape=jax.ShapeDtypeStruct((b * s, SEQ_WIDTH), BF16),
        compiler_params=_params(("parallel", "arbitrary")), name="dsa_attn",
    )(status, nhi, q, iq, iw, kk, vv, ikk, cq3, ck3)


def _mix_out_kernel(x_ref, oseq_ref, qm_ref, km_ref, vm_ref, wo_ref, g_ref, o_ref):
    qm = qm_ref[...]
    km = km_ref[...]
    vm = vm_ref[...]
    lane = lax.broadcasted_iota(I32, qm.shape, 1)
    o_mem = jnp.zeros(qm.shape, F32)
    for h in range(MEM_HEADS):
        mine = (lane // HEAD_DIM) == h
        s = _dot_nt(jnp.where(mine, qm, jnp.zeros_like(qm)), km)
        p = jnp.exp(s - jnp.max(s, axis=1, keepdims=True))
        p = p / jnp.sum(p, axis=1, keepdims=True)
        o_mem = o_mem + jnp.where(mine, _dot(p.astype(BF16), vm), 0.0)
    y = _dot(oseq_ref[...], wo_ref[0:SEQ_WIDTH, :]) + _dot(o_mem.astype(BF16), wo_ref[SEQ_WIDTH:, :])
    o_ref[...] = x_ref[...] + _rms(y, g_ref[...])


def _mix_out(x2d, oseq, qm, kv, w_out, g, b, s, n_mem, tm):
    nt = s // tm
    return pl.pallas_call(
        _mix_out_kernel, grid=(b, nt),
        in_specs=[
            pl.BlockSpec((tm, D_MODEL), lambda bb, i: (bb * nt + i, 0)),
            pl.BlockSpec((tm, SEQ_WIDTH), lambda bb, i: (bb * nt + i, 0)),
            pl.BlockSpec((tm, MEM_WIDTH), lambda bb, i: (bb * nt + i, 0)),
            pl.BlockSpec((n_mem, MEM_WIDTH), lambda bb, i: (bb, 0)),
            pl.BlockSpec((n_mem, MEM_WIDTH), lambda bb, i: (bb, 1)),
            pl.BlockSpec((D_MODEL, D_MODEL), lambda bb, i: (0, 0)),
            pl.BlockSpec((1, D_MODEL), lambda bb, i: (0, 0)),
        ],
        out_specs=pl.BlockSpec((tm, D_MODEL), lambda bb, i: (bb * nt + i, 0)),
        out_shape=jax.ShapeDtypeStruct(x2d.shape, F32),
        compiler_params=_params(("parallel", "parallel")), name="mix_out",
    )(x2d, oseq, qm, kv, kv, w_out, g.reshape(1, D_MODEL))


def _mlp_kernel(x_ref, gpre_ref, wup_ref, wdn_ref, gpost_ref, o_ref, h_s, acc_s):
    j = pl.program_id(1)

    @pl.when(j == 0)
    def _():
        h_s[...] = _rms(x_ref[...], gpre_ref[...]).astype(BF16)
        acc_s[...] = jnp.zeros(acc_s.shape, F32)

    u = jnp.maximum(_dot(h_s[...], wup_ref[...]), 0.0)
    acc_s[...] += _dot((u * u).astype(BF16), wdn_ref[...])

    @pl.when(j == pl.num_programs(1) - 1)
    def _():
        o_ref[...] = x_ref[...] + _rms(acc_s[...], gpost_ref[...])


def _mlp(x2d, g_pre, w_up, w_dn, g_post, tm, tf):
    m = x2d.shape[0]
    return pl.pallas_call(
        _mlp_kernel, grid=(m // tm, D_FF // tf),
        in_specs=[
            pl.BlockSpec((tm, D_MODEL), lambda i, j: (i, 0)),
            pl.BlockSpec((1, D_MODEL), lambda i, j: (0, 0)),
            pl.BlockSpec((D_MODEL, tf), lambda i, j: (0, j)),
            pl.BlockSpec((tf, D_MODEL), lambda i, j: (j, 0)),
            pl.BlockSpec((1, D_MODEL), lambda i, j: (0, 0)),
        ],
        out_specs=pl.BlockSpec((tm, D_MODEL), lambda i, j: (i, 0)),
        out_shape=jax.ShapeDtypeStruct(x2d.shape, F32),
        scratch_shapes=[pltpu.VMEM((tm, D_MODEL), BF16), pltpu.VMEM((tm, D_MODEL), F32)],
        compiler_params=_params(("parallel", "arbitrary")), name="mlp",
    )(x2d, g_pre.reshape(1, D_MODEL), w_up, w_dn, g_post.reshape(1, D_MODEL))


def _rope_tables(positions):
    half = HEAD_DIM // 2
    inv = ROPE_THETA ** (-jnp.arange(0, HEAD_DIM, 2, dtype=F32) / HEAD_DIM)
    ang = positions.astype(F32).reshape(-1, 1) * inv
    cos, sin = jnp.cos(ang), jnp.sin(ang)
    reps = LANES // HEAD_DIM
    cos_t = jnp.tile(cos, (1, 2 * reps))
    sin_t = jnp.tile(jnp.concatenate([-sin, sin], axis=1), (1, reps))
    assert cos_t.shape[1] == LANES and half * 2 == HEAD_DIM
    return cos_t, sin_t


def _visibility(chunk_id, tq, tk):
    b, s = chunk_id.shape
    cq = chunk_id.reshape(b, s // tq, tq)
    ck = chunk_id.reshape(b, s // tk, tk)
    qmin, qmax = cq.min(-1)[:, :, None], cq.max(-1)[:, :, None]
    kmin, kmax = ck.min(-1)[:, None, :], ck.max(-1)[:, None, :]
    status = jnp.where(kmin > qmax, 0, jnp.where(kmax <= qmin, 1, 2)).astype(I32)
    tiles = jnp.arange(s // tk, dtype=I32)[None, None, :]
    nhi = jnp.max(jnp.where(status != 0, tiles + 1, 0), axis=-1).astype(I32)
    return status.reshape(-1), nhi.reshape(-1)


def _dup(w):
    return jnp.concatenate([w, w], axis=1)


def kernel(x, mem, positions, g_pre_mix, g_post_mix, g_mem, w_mem_kv, w_out, g_pre_mlp, g_post_mlp,
           w_mlp_up, w_mlp_down, w_in_diff, lambda_q1, lambda_k1, lambda_q2, lambda_k2, g_diff_subln,
           w_in_dsa):
    b, s, d = x.shape
    n_mem = mem.shape[1]
    depth = g_pre_mix.shape[0]
    m = b * s
    scale = HEAD_DIM ** -0.5
    scale2 = scale * math.log2(math.e)
    tm_proj = min(512, s)
    tq_diff, tk_diff = min(512, s), min(1024, s)
    tq_dsa, tk_dsa = min(256, s), min(512, s)
    tm_mix = min(512, s)
    tm_mlp, tf_mlp = min(1024, m), 1024
    top_k = min(DSA_TOPK_MAX, s // 4)
    assert tk_dsa >= top_k, "the threshold search needs one key tile to hold top_k candidates"

    cos_t, sin_t = _rope_tables(positions)
    chunk_id = positions // CHUNK
    cq3 = chunk_id.reshape(-1, 1)
    ck3 = chunk_id.reshape(b, 1, s)
    vis_diff = _visibility(chunk_id, tq_diff, tk_diff)
    vis_dsa = _visibility(chunk_id, tq_dsa, tk_dsa)

    x2d = x.reshape(m, d)
    mem2d = mem.reshape(b * n_mem, d)
    for i in range(depth):
        j = i // 2
        kv = _inproj(mem2d, g_mem[i], None, None, None, w_mem_kv[i].astype(BF16), (),
                     ((2 * MEM_WIDTH, 1.0, BF16),), min(256, b * n_mem))[0]
        if i % 2 == 0:
            w = w_in_diff[j].astype(BF16)
            nqk = 2 * DIFF_HEADS * HEAD_DIM
            q, k, v, qm = _inproj(
                x2d, g_pre_mix[i], cos_t, sin_t, w[:, :2 * nqk], w[:, 2 * nqk:],
                ((nqk, scale2, BF16), (nqk, 1.0, BF16)),
                ((SEQ_WIDTH, 1.0, BF16), (MEM_WIDTH, scale, BF16)), tm_proj)
            lam_p = jnp.stack([lambda_q1[j], lambda_k1[j], lambda_q2[j], lambda_k2[j]]).astype(F32)
            lam_init = 0.8 - 0.6 * math.exp(-0.3 * i)
            o_seq = _diff_attention(
                q, k, v, cq3.reshape(m // tq_diff, tq_diff, 1), ck3, vis_diff[0], vis_diff[1],
                lam_p, g_diff_subln[j].reshape(1, LANES), b, s, tq_diff, tk_diff, lam_init)
        else:
            w = w_in_dsa[j]
            o0 = 0
            parts = []
            for width in (DSA_HEADS * HEAD_DIM, HEAD_DIM, HEAD_DIM, IDX_HEADS * IDX_DIM, IDX_DIM,
                          IDX_HEADS, MEM_WIDTH):
                parts.append(w[:, o0:o0 + width])
                o0 += width
            wq, wk, wv, wiq, wik, wiw, wqm = parts
            wiw = jnp.pad(wiw, ((0, 0), (0, LANES - IDX_HEADS)))
            w_rope = jnp.concatenate([wq, wiq, _dup(wk), _dup(wik)], axis=1).astype(BF16)
            w_plain = jnp.concatenate([_dup(wv), wiw, wqm], axis=1).astype(BF16)
            iw_scale = IDX_HEADS ** -0.5 * IDX_DIM ** -0.5
            q, iq, kk, ikk, vv, iw, qm = _inproj(
                x2d, g_pre_mix[i], cos_t, sin_t, w_rope, w_plain,
                ((DSA_HEADS * HEAD_DIM, scale2, BF16), (IDX_HEADS * IDX_DIM, 1.0, BF16),
                 (LANES, 1.0, BF16), (LANES, 1.0, BF16)),
                ((LANES, 1.0, BF16), (LANES, iw_scale, F32), (MEM_WIDTH, scale, BF16)), tm_proj)
            o_seq = _dsa_attention(
                q, iq, iw, kk, vv, ikk, cq3.reshape(m // tq_dsa, tq_dsa, 1), ck3,
                vis_dsa[0], vis_dsa[1], b, s, tq_dsa, tk_dsa, top_k)
        x2d = _mix_out(x2d, o_seq, qm, kv, w_out[i].astype(BF16), g_post_mix[i], b, s, n_mem, tm_mix)
        x2d = _mlp(x2d, g_pre_mlp[i], w_mlp_up[i].astype(BF16), w_mlp_down[i].astype(BF16),
                   g_post_mlp[i], tm_mlp, tf_mlp)
    return x2d.reshape(b, s, d)
```

```python
import functools
import math

import jax
import jax.numpy as jnp
from jax import lax
from jax.experimental import pallas as pl
from jax.experimental.pallas import tpu as pltpu

D_MODEL = 1024
CHUNK = 64
HEAD_DIM = 64
LANES = 128
ROPE_THETA = 10000.0
EPS = 1e-6
MEM_HEADS = 4
MEM_WIDTH = MEM_HEADS * HEAD_DIM
SEQ_WIDTH = D_MODEL - MEM_WIDTH
DIFF_HEADS = SEQ_WIDTH // (2 * HEAD_DIM)
DSA_HEADS = SEQ_WIDTH // HEAD_DIM
IDX_HEADS = 8
IDX_DIM = 64
DSA_TOPK_MAX = 256
D_FF = 4 * D_MODEL

NEG = -1e30
INT_MIN = -(2 ** 31)
MASKED_KEY = -2139095041
TOP_PER_LANE = 12
CAND_ROWS = 32
VMEM_LIMIT = 56 * 1024 * 1024

F32 = jnp.float32
BF16 = jnp.bfloat16
I32 = jnp.int32


def _dot(a, b):
    return jnp.dot(a, b, preferred_element_type=F32)


def _dot_nt(a, b):
    return lax.dot_general(a, b, (((1,), (1,)), ((), ())), preferred_element_type=F32)


def _lane_tile(x, width):
    return jnp.concatenate([x] * (width // LANES), axis=1)


def _key_to_f32(key):
    return lax.bitcast_convert_type(jnp.where(key < 0, key ^ 0x7FFFFFFF, key), F32)


def _rms(x, g):
    return x * lax.rsqrt(jnp.mean(x * x, axis=-1, keepdims=True) + EPS) * g


def _params(sem):
    return pltpu.CompilerParams(dimension_semantics=sem, vmem_limit_bytes=VMEM_LIMIT)


def _inproj_kernel(*refs, rope_outs, plain_outs):
    n_r, n_p = len(rope_outs), len(plain_outs)
    x_ref, g_ref = refs[0], refs[1]
    pos = 2
    if n_r:
        cos_ref, sin_ref, wr_ref = refs[2], refs[3], refs[4]
        pos = 5
    if n_p:
        wp_ref = refs[pos]
        pos += 1
    out_refs = refs[pos:]
    x = x_ref[...]
    h = _rms(x, g_ref[...]).astype(BF16)
    oi = 0
    if n_r:
        cos = cos_ref[...]
        sin = sin_ref[...]
        lane = lax.broadcasted_iota(I32, cos.shape, 1)
        low = (lane % HEAD_DIM) < (HEAD_DIM // 2)
        c0 = 0
        for width, scale in rope_outs:
            o_ref = out_refs[oi]
            oi += 1
            y_all = _dot(h, wr_ref[:, c0:c0 + width])
            for s in range(width // LANES):
                y = y_all[:, s * LANES:(s + 1) * LANES]
                swapped = jnp.where(low, pltpu.roll(y, LANES - HEAD_DIM // 2, 1),
                                    pltpu.roll(y, HEAD_DIM // 2, 1))
                r = y * cos + swapped * sin
                if scale != 1.0:
                    r = r * scale
                o_ref[:, s * LANES:(s + 1) * LANES] = r.astype(o_ref.dtype)
            c0 += width
    c0 = 0
    for width, scale in plain_outs:
        o_ref = out_refs[oi]
        oi += 1
        y = _dot(h, wp_ref[:, c0:c0 + width])
        if scale != 1.0:
            y = y * scale
        o_ref[...] = y.astype(o_ref.dtype)
        c0 += width


def _inproj(x2d, g, cos, sin, w_rope, w_plain, rope_outs, plain_outs, tm):
    m = x2d.shape[0]
    row = lambda i: (i, 0)
    fixed = lambda i: (0, 0)
    in_specs = [pl.BlockSpec((tm, D_MODEL), row), pl.BlockSpec((1, D_MODEL), fixed)]
    args = [x2d, g.reshape(1, D_MODEL)]
    if rope_outs:
        in_specs += [pl.BlockSpec((tm, LANES), row), pl.BlockSpec((tm, LANES), row),
                     pl.BlockSpec(w_rope.shape, fixed)]
        args += [cos, sin, w_rope]
    if plain_outs:
        in_specs += [pl.BlockSpec(w_plain.shape, fixed)]
        args += [w_plain]
    outs = tuple(rope_outs) + tuple(plain_outs)
    out_shape = [jax.ShapeDtypeStruct((m, w), dt) for w, _, dt in outs]
    out_specs = [pl.BlockSpec((tm, w), row) for w, _, _ in outs]
    kern = functools.partial(_inproj_kernel,
                             rope_outs=tuple((w, s) for w, s, _ in rope_outs),
                             plain_outs=tuple((w, s) for w, s, _ in plain_outs))
    return pl.pallas_call(
        kern, grid=(m // tm,), in_specs=in_specs, out_specs=out_specs, out_shape=out_shape,
        compiler_params=_params(("parallel",)), name="inproj")(*args)


def _diff_attn_kernel(status_ref, nhi_ref, q_ref, k_ref, v_ref, cq_ref, ck_ref, lam_ref, gsub_ref,
                      o_ref, m_s, acc_s, s0_s, s1_s, *, tq, tk, nq, nk, lam_init):
    b = pl.program_id(0)
    i = pl.program_id(2)
    q = q_ref[...]
    lane = lax.broadcasted_iota(I32, q.shape, 1)
    zero = jnp.zeros_like(q)
    qs = (jnp.where(lane < HEAD_DIM, q, zero), jnp.where(lane >= HEAD_DIM, q, zero))
    cq = cq_ref[0]
    m_s[...] = jnp.full(m_s.shape, NEG, F32)
    acc_s[...] = jnp.zeros(acc_s.shape, F32)
    base = (b * nq + i) * nk
    n_hi = nhi_ref[b * nq + i]

    def logits(t, dst_s):
        kt = k_ref[pl.ds(pl.multiple_of(t * tk, tk), tk), :]
        for j in range(2):
            dst_s[j] = _dot_nt(qs[j], kt)

    def tile(t, masked, cur_s, nxt_s):
        if nxt_s is not None:
            logits(t + 1, nxt_s)
        off = pl.multiple_of(t * tk, tk)
        vt = jnp.concatenate([v_ref[pl.ds(off, tk), :], jnp.ones((tk, LANES), BF16)], axis=1)
        if masked:
            ck = ck_ref[0, :, pl.ds(off, tk)]
            bias = jnp.where(cq >= ck, 0.0, NEG)
        for j in range(2):
            s = cur_s[j] + bias if masked else cur_s[j]
            m_old = m_s[j]
            m_new = jnp.maximum(m_old, jnp.max(s, axis=1, keepdims=True))
            p = jnp.exp2(s - _lane_tile(m_new, tk))
            alpha = jnp.exp2(m_old - m_new)
            acc_s[j] = _lane_tile(alpha, 2 * LANES) * acc_s[j] + _dot(p.astype(BF16), vt)
            m_s[j] = m_new

    def step(t, cur_s, nxt_s):
        st = status_ref[base + jnp.minimum(t, nk - 1)]
        live = t < n_hi
        more = t + 1 < n_hi

        @pl.when(jnp.logical_and(st == 0, more))
        def _():
            logits(t + 1, nxt_s)

        for code, masked in ((1, False), (2, True)):
            hit = jnp.logical_and(live, st == code)

            @pl.when(jnp.logical_and(hit, more))
            def _():
                tile(t, masked, cur_s, nxt_s)

            @pl.when(jnp.logical_and(hit, jnp.logical_not(more)))
            def _():
                tile(t, masked, cur_s, None)

    logits(0, s0_s)

    def body(u, carry):
        step(2 * u, s0_s, s1_s)
        step(2 * u + 1, s1_s, s0_s)
        return carry

    lax.fori_loop(0, (n_hi + 1) // 2, body, 0)

    lam_p = lam_ref[...]
    lam = (jnp.exp(jnp.sum(lam_p[0:1] * lam_p[1:2], axis=1, keepdims=True))
           - jnp.exp(jnp.sum(lam_p[2:3] * lam_p[3:4], axis=1, keepdims=True)) + lam_init)
    o = (acc_s[0, :, :LANES] / acc_s[0, :, LANES:]
         - lam * (acc_s[1, :, :LANES] / acc_s[1, :, LANES:]))
    o = _rms(o, gsub_ref[...]) * (1.0 - lam_init)
    o_ref[...] = o.astype(o_ref.dtype)


def _diff_attention(q, k, v, cq3, ck3, status, nhi, lam_p, gsub, b, s, tq, tk, lam_init):
    nq, nk = s // tq, s // tk
    kern = functools.partial(_diff_attn_kernel, tq=tq, tk=tk, nq=nq, nk=nk, lam_init=lam_init)
    grid_spec = pltpu.PrefetchScalarGridSpec(
        num_scalar_prefetch=2,
        grid=(b, DIFF_HEADS, nq),
        in_specs=[
            pl.BlockSpec((tq, LANES), lambda bb, h, i, *_: (bb * nq + i, h)),
            pl.BlockSpec((s, LANES), lambda bb, h, i, *_: (bb, h)),
            pl.BlockSpec((s, LANES), lambda bb, h, i, *_: (bb, h)),
            pl.BlockSpec((1, tq, 1), lambda bb, h, i, *_: (bb * nq + i, 0, 0)),
            pl.BlockSpec((1, 1, s), lambda bb, h, i, *_: (bb, 0, 0)),
            pl.BlockSpec((4, HEAD_DIM), lambda bb, h, i, *_: (0, 0)),
            pl.BlockSpec((1, LANES), lambda bb, h, i, *_: (0, 0)),
        ],
        out_specs=pl.BlockSpec((tq, LANES), lambda bb, h, i, *_: (bb * nq + i, h)),
        scratch_shapes=[pltpu.VMEM((2, tq, LANES), F32),
                        pltpu.VMEM((2, tq, 2 * LANES), F32),
                        pltpu.VMEM((2, tq, tk), F32),
                        pltpu.VMEM((2, tq, tk), F32)],
    )
    return pl.pallas_call(
        kern, grid_spec=grid_spec,
        out_shape=jax.ShapeDtypeStruct((b * s, SEQ_WIDTH), BF16),
        compiler_params=_params(("parallel", "parallel", "arbitrary")), name="diff_attn",
    )(status, nhi, q, k, v, cq3, ck3, lam_p, gsub)


def _dsa_kernel(status_ref, nhi_ref, q_ref, iq_ref, iw_ref, kk_ref, vv_ref, ikk_ref, cq_ref, ck_ref,
                o_ref, keys_s, cand_s, qst_s, iqst_s, iwb_s, m_s, acc_s, thr_s, thf_s, jst_s,
                *, tq, tk, nq, nk, s_len, top_k):
    b = pl.program_id(0)
    i = pl.program_id(1)
    base = (b * nq + i) * nk
    n_hi = nhi_ref[b * nq + i]
    cq = cq_ref[0]
    lane = lax.broadcasted_iota(I32, (tq, LANES), 1)
    first = lane < HEAD_DIM

    for h in range(DSA_HEADS):
        slab = q_ref[:, (h // 2) * LANES:(h // 2 + 1) * LANES]
        keep = first if h % 2 == 0 else jnp.logical_not(first)
        qst_s[h * tq:(h + 1) * tq, :] = jnp.where(keep, slab, jnp.zeros_like(slab))
    for h in range(IDX_HEADS):
        slab = iq_ref[:, (h // 2) * LANES:(h // 2 + 1) * LANES]
        keep = first if h % 2 == 0 else jnp.logical_not(first)
        iqst_s[h * tq:(h + 1) * tq, :] = jnp.where(keep, slab, jnp.zeros_like(slab))
        iwb_s[h * tq:(h + 1) * tq, :] = jnp.broadcast_to(iw_ref[:, h:h + 1], (tq, LANES))

    def score_body(t, carry):
        st = status_ref[base + t]
        off = pl.multiple_of(t * tk, tk)

        @pl.when(st == 0)
        def _():
            keys_s[:, pl.ds(off, tk)] = jnp.full((tq, tk), -jnp.inf, F32)

        def scores(masked):
            ikt = ikk_ref[pl.ds(off, tk), :]
            logits = _dot_nt(iqst_s[...], ikt)
            r = jnp.maximum(logits, 0.0) * _lane_tile(iwb_s[...], tk)
            sc = r[0:tq]
            for h in range(1, IDX_HEADS):
                sc = sc + r[h * tq:(h + 1) * tq]
            sc = jnp.where(sc == 0.0, 0.0, sc)
            if masked:
                ck = ck_ref[0, :, pl.ds(off, tk)]
                sc = jnp.where(cq >= ck, sc, -jnp.inf)
            keys_s[:, pl.ds(off, tk)] = sc

        @pl.when(st == 1)
        def _():
            scores(False)

        @pl.when(st == 2)
        def _():
            scores(True)

        return carry

    lax.fori_loop(0, n_hi, score_body, 0)

    rg = min(tq, LANES)
    lane_rg = lax.broadcasted_iota(I32, (rg, LANES), 1)

    def count_tiles(preds, *operands):
        zero = jnp.zeros((rg, LANES), I32)
        accs = []
        for r in range(tq // rg):
            rows = slice(r * rg, (r + 1) * rg)
            ops = [o[rows] for o in operands]

            def body(t, acc, rows=rows, ops=ops):
                off = pl.multiple_of(t * tk, tk)
                acc = list(acc)
                for c in range(tk // LANES):
                    slab = keys_s[rows, pl.ds(off + c * LANES, LANES)]
                    for n, pred in enumerate(preds):
                        acc[n] = acc[n] + jnp.where(pred(slab, off + c * LANES, *ops), 1, 0).astype(I32)
                return tuple(acc)
            accs.append(lax.fori_loop(0, n_hi, body, (zero,) * len(preds)))
        outs = []
        for n in range(len(preds)):
            acc = accs[0][n] if len(accs) == 1 else jnp.concatenate([a[n] for a in accs], axis=0)
            outs.append(jnp.broadcast_to(jnp.sum(acc, axis=1, keepdims=True), (tq, LANES)))
        return outs

    ge = lambda slab, col0, c: slab >= c
    gt = lambda slab, col0, c: slab > c
    eq = lambda slab, col0, c: slab == c

    for g in range(tq // CAND_ROWS):
        rows = slice(g * CAND_ROWS, (g + 1) * CAND_ROWS)

        def top_body(t, tops, rows=rows):
            off = pl.multiple_of(t * tk, tk)
            tops = list(tops)
            for c in range(tk // LANES):
                x = keys_s[rows, pl.ds(off + c * LANES, LANES)]
                for r in range(TOP_PER_LANE):
                    tops[r], x = jnp.maximum(tops[r], x), jnp.minimum(tops[r], x)
            return tuple(tops)

        tops = lax.fori_loop(0, n_hi, top_body,
                             (jnp.full((CAND_ROWS, LANES), -jnp.inf, F32),) * TOP_PER_LANE)
        for r in range(TOP_PER_LANE):
            cand_s[r, rows, :] = tops[r]

    def count_cand(preds, operand):
        outs = [jnp.zeros((tq, LANES), I32) for _ in preds]
        for r in range(TOP_PER_LANE):
            slab = cand_s[r]
            for n, pred in enumerate(preds):
                outs[n] = outs[n] + jnp.where(pred(slab, None, operand), 1, 0).astype(I32)
        return [jnp.broadcast_to(jnp.sum(o, axis=1, keepdims=True), (tq, LANES)) for o in outs]

    def bisect(count):
        def bit_body(j, thr):
            cand = thr + lax.shift_left(jnp.int32(1), 31 - j)
            cnt, = count([ge], _key_to_f32(cand))
            return jnp.where(cnt >= top_k, cand, thr)
        return lax.fori_loop(0, 32, bit_body, jnp.full((tq, LANES), INT_MIN, I32))

    thr = bisect(count_cand)
    floor = _key_to_f32(jnp.maximum(thr, MASKED_KEY + 1))
    n_cand, = count_cand([ge], floor)
    n_all, = count_tiles([ge], floor)
    thr_s[...] = thr

    @pl.when(jnp.max((n_all != n_cand).astype(I32)) > 0)
    def _():
        thr_s[...] = bisect(count_tiles)

    thr_key = thr_s[...]
    real = thr_key > MASKED_KEY
    thr = jnp.where(real, _key_to_f32(thr_key), -jnp.inf)
    thf_s[...] = thr
    cnt_gt, cnt_eq = count_tiles([gt, eq], thr)
    need = top_k - cnt_gt
    excess = jnp.logical_and(real, cnt_eq > need)
    jst_s[...] = jnp.where(real, jnp.int32(s_len), jnp.int32(-1))

    @pl.when(jnp.max(excess.astype(I32)) > 0)
    def _():
        def idx_body(j, jcur):
            cand = jcur + lax.shift_left(jnp.int32(1), (s_len.bit_length() - 2) - j)
            cnt, = count_tiles(
                [lambda slab, col0, th, c: jnp.logical_and(slab == th, (col0 + lane_rg) < c)], thr, cand)
            return jnp.where(cnt < need, cand, jcur)
        jbest = lax.fori_loop(0, s_len.bit_length() - 1, idx_body, jnp.zeros((tq, LANES), I32))
        jst_s[...] = jnp.where(excess, jbest, jst_s[...])

    m_s[...] = jnp.full(m_s.shape, NEG, F32)
    acc_s[...] = jnp.zeros(acc_s.shape, F32)
    lane_k = lax.broadcasted_iota(I32, (tq, tk), 1)
    lane_v = lax.broadcasted_iota(I32, (tk, LANES), 1)

    def attn_body(t, carry):
        st = status_ref[base + t]

        @pl.when(st != 0)
        def _():
            off = pl.multiple_of(t * tk, tk)
            key = keys_s[:, pl.ds(off, tk)]
            th = _lane_tile(thf_s[...], tk)
            jst = _lane_tile(jst_s[...], tk)
            sel = jnp.logical_or(key > th, jnp.logical_and(key == th, (lane_k + off) <= jst))
            bias = jnp.where(sel, 0.0, NEG)
            kt = kk_ref[pl.ds(off, tk), :]
            vt = vv_ref[pl.ds(off, tk), :]
            vt = jnp.where(lane_v < HEAD_DIM, vt, jnp.ones_like(vt))
            s = _dot_nt(qst_s[...], kt)
            s = (s.reshape(DSA_HEADS, tq, tk) + bias[None]).reshape(DSA_HEADS * tq, tk)
            m_old = m_s[...]
            m_new = jnp.maximum(m_old, jnp.max(s, axis=1, keepdims=True))
            p = jnp.exp2(s - _lane_tile(m_new, tk))
            alpha = jnp.exp2(m_old - m_new)
            acc_s[...] = alpha * acc_s[...] + _dot(p.astype(BF16), vt)
            m_s[...] = m_new

        return carry

    lax.fori_loop(0, n_hi, attn_body, 0)

    for j in range(DSA_HEADS // 2):
        acc_a = acc_s[2 * j * tq:(2 * j + 1) * tq]
        acc_b = acc_s[(2 * j + 1) * tq:(2 * j + 2) * tq]
        swap_a = pltpu.roll(acc_a, HEAD_DIM, 1)
        swap_b = pltpu.roll(acc_b, HEAD_DIM, 1)
        o_ref[:, j * LANES:(j + 1) * LANES] = jnp.where(first, acc_a / swap_a,
                                                        swap_b / acc_b).astype(o_ref.dtype)


def _dsa_attention(q, iq, iw, kk, vv, ikk, cq3, ck3, status, nhi, b, s, tq, tk, top_k):
    nq, nk = s // tq, s // tk
    kern = functools.partial(_dsa_kernel, tq=tq, tk=tk, nq=nq, nk=nk, s_len=s, top_k=top_k)
    rowblk = lambda bb, i, *_: (bb * nq + i, 0)
    perb = lambda bb, i, *_: (bb, 0)
    grid_spec = pltpu.PrefetchScalarGridSpec(
        num_scalar_prefetch=2,
        grid=(b, nq),
        in_specs=[
            pl.BlockSpec((tq, SEQ_WIDTH), rowblk),
            pl.BlockSpec((tq, IDX_HEADS * IDX_DIM), rowblk),
            pl.BlockSpec((tq, LANES), rowblk),
            pl.BlockSpec((s, LANES), perb),
            pl.BlockSpec((s, LANES), perb),
            pl.BlockSpec((s, LANES), perb),
            pl.BlockSpec((1, tq, 1), lambda bb, i, *_: (bb * nq + i, 0, 0)),
            pl.BlockSpec((1, 1, s), lambda bb, i, *_: (bb, 0, 0)),
        ],
        out_specs=pl.BlockSpec((tq, SEQ_WIDTH), rowblk),
        scratch_shapes=[
            pltpu.VMEM((tq, s), F32),
            pltpu.VMEM((TOP_PER_LANE, tq, LANES), F32),
            pltpu.VMEM((DSA_HEADS * tq, LANES), BF16),
            pltpu.VMEM((IDX_HEADS * tq, LANES), BF16),
            pltpu.VMEM((IDX_HEADS * tq, LANES), F32),
            pltpu.VMEM((DSA_HEADS * tq, LANES), F32),
            pltpu.VMEM((DSA_HEADS * tq, LANES), F32),
            pltpu.VMEM((tq, LANES), I32),
            pltpu.VMEM((tq, LANES), F32),
            pltpu.VMEM((tq, LANES), I32),
        ],
    )
    return pl.pallas_call(
        kern, grid_spec=grid_spec,
        out_shape=jax.ShapeDtypeStruct((b * s, SEQ_WIDTH), BF16),
        compiler_params=_params(("parallel", "arbitrary")), name="dsa_attn",
    )(status, nhi, q, iq, iw, kk, vv, ikk, cq3, ck3)


def _mix_out_kernel(x_ref, oseq_ref, qm_ref, km_ref, vm_ref, wo_ref, g_ref, o_ref):
    qm = qm_ref[...]
    km = km_ref[...]
    vm = vm_ref[...]
    lane = lax.broadcasted_iota(I32, qm.shape, 1)
    o_mem = jnp.zeros(qm.shape, F32)
    for h in range(MEM_HEADS):
        mine = (lane // HEAD_DIM) == h
        s = _dot_nt(jnp.where(mine, qm, jnp.zeros_like(qm)), km)
        p = jnp.exp(s - jnp.max(s, axis=1, keepdims=True))
        p = p / jnp.sum(p, axis=1, keepdims=True)
        o_mem = o_mem + jnp.where(mine, _dot(p.astype(BF16), vm), 0.0)
    y = _dot(oseq_ref[...], wo_ref[0:SEQ_WIDTH, :]) + _dot(o_mem.astype(BF16), wo_ref[SEQ_WIDTH:, :])
    o_ref[...] = x_ref[...] + _rms(y, g_ref[...])


def _mix_out(x2d, oseq, qm, kv, w_out, g, b, s, n_mem, tm):
    nt = s // tm
    return pl.pallas_call(
        _mix_out_kernel, grid=(b, nt),
        in_specs=[
            pl.BlockSpec((tm, D_MODEL), lambda bb, i: (bb * nt + i, 0)),
            pl.BlockSpec((tm, SEQ_WIDTH), lambda bb, i: (bb * nt + i, 0)),
            pl.BlockSpec((tm, MEM_WIDTH), lambda bb, i: (bb * nt + i, 0)),
            pl.BlockSpec((n_mem, MEM_WIDTH), lambda bb, i: (bb, 0)),
            pl.BlockSpec((n_mem, MEM_WIDTH), lambda bb, i: (bb, 1)),
            pl.BlockSpec((D_MODEL, D_MODEL), lambda bb, i: (0, 0)),
            pl.BlockSpec((1, D_MODEL), lambda bb, i: (0, 0)),
        ],
        out_specs=pl.BlockSpec((tm, D_MODEL), lambda bb, i: (bb * nt + i, 0)),
        out_shape=jax.ShapeDtypeStruct(x2d.shape, F32),
        compiler_params=_params(("parallel", "parallel")), name="mix_out",
    )(x2d, oseq, qm, kv, kv, w_out, g.reshape(1, D_MODEL))


def _mlp_kernel(x_ref, gpre_ref, wup_ref, wdn_ref, gpost_ref, o_ref, h_s, acc_s):
    j = pl.program_id(1)

    @pl.when(j == 0)
    def _():
        h_s[...] = _rms(x_ref[...], gpre_ref[...]).astype(BF16)
        acc_s[...] = jnp.zeros(acc_s.shape, F32)

    u = jnp.maximum(_dot(h_s[...], wup_ref[...]), 0.0)
    acc_s[...] += _dot((u * u).astype(BF16), wdn_ref[...])

    @pl.when(j == pl.num_programs(1) - 1)
    def _():
        o_ref[...] = x_ref[...] + _rms(acc_s[...], gpost_ref[...])


def _mlp(x2d, g_pre, w_up, w_dn, g_post, tm, tf):
    m = x2d.shape[0]
    return pl.pallas_call(
        _mlp_kernel, grid=(m // tm, D_FF // tf),
        in_specs=[
            pl.BlockSpec((tm, D_MODEL), lambda i, j: (i, 0)),
            pl.BlockSpec((1, D_MODEL), lambda i, j: (0, 0)),
            pl.BlockSpec((D_MODEL, tf), lambda i, j: (0, j)),
            pl.BlockSpec((tf, D_MODEL), lambda i, j: (j, 0)),
            pl.BlockSpec((1, D_MODEL), lambda i, j: (0, 0)),
        ],
        out_specs=pl.BlockSpec((tm, D_MODEL), lambda i, j: (i, 0)),
        out_shape=jax.ShapeDtypeStruct(x2d.shape, F32),
        scratch_shapes=[pltpu.VMEM((tm, D_MODEL), BF16), pltpu.VMEM((tm, D_MODEL), F32)],
        compiler_params=_params(("parallel", "arbitrary")), name="mlp",
    )(x2d, g_pre.reshape(1, D_MODEL), w_up, w_dn, g_post.reshape(1, D_MODEL))


def _rope_tables(positions):
    half = HEAD_DIM // 2
    inv = ROPE_THETA ** (-jnp.arange(0, HEAD_DIM, 2, dtype=F32) / HEAD_DIM)
    ang = positions.astype(F32).reshape(-1, 1) * inv
    cos, sin = jnp.cos(ang), jnp.sin(ang)
    reps = LANES // HEAD_DIM
    cos_t = jnp.tile(cos, (1, 2 * reps))
    sin_t = jnp.tile(jnp.concatenate([-sin, sin], axis=1), (1, reps))
    assert cos_t.shape[1] == LANES and half * 2 == HEAD_DIM
    return cos_t, sin_t


def _visibility(chunk_id, tq, tk):
    b, s = chunk_id.shape
    cq = chunk_id.reshape(b, s // tq, tq)
    ck = chunk_id.reshape(b, s // tk, tk)
    qmin, qmax = cq.min(-1)[:, :, None], cq.max(-1)[:, :, None]
    kmin, kmax = ck.min(-1)[:, None, :], ck.max(-1)[:, None, :]
    status = jnp.where(kmin > qmax, 0, jnp.where(kmax <= qmin, 1, 2)).astype(I32)
    tiles = jnp.arange(s // tk, dtype=I32)[None, None, :]
    nhi = jnp.max(jnp.where(status != 0, tiles + 1, 0), axis=-1).astype(I32)
    return status.reshape(-1), nhi.reshape(-1)


def _dup(w):
    return jnp.concatenate([w, w], axis=1)


def kernel(x, mem, positions, g_pre_mix, g_post_mix, g_mem, w_mem_kv, w_out, g_pre_mlp, g_post_mlp,
           w_mlp_up, w_mlp_down, w_in_diff, lambda_q1, lambda_k1, lambda_q2, lambda_k2, g_diff_subln,
           w_in_dsa):
    b, s, d = x.shape
    n_mem = mem.shape[1]
    depth = g_pre_mix.shape[0]
    m = b * s
    scale = HEAD_DIM ** -0.5
    scale2 = scale * math.log2(math.e)
    tm_proj = min(512, s)
    tq_diff, tk_diff = min(512, s), min(1024, s)
    tq_dsa, tk_dsa = min(256, s), min(512, s)
    tm_mix = min(512, s)
    tm_mlp, tf_mlp = min(1024, m), 1024
    top_k = min(DSA_TOPK_MAX, s // 4)
    assert tk_dsa >= top_k, "the threshold search needs one key tile to hold top_k candidates"

    cos_t, sin_t = _rope_tables(positions)
    chunk_id = positions // CHUNK
    cq3 = chunk_id.reshape(-1, 1)
    ck3 = chunk_id.reshape(b, 1, s)
    vis_diff = _visibility(chunk_id, tq_diff, tk_diff)
    vis_dsa = _visibility(chunk_id, tq_dsa, tk_dsa)

    x2d = x.reshape(m, d)
    mem2d = mem.reshape(b * n_mem, d)
    for i in range(depth):
        j = i // 2
        kv = _inproj(mem2d, g_mem[i], None, None, None, w_mem_kv[i].astype(BF16), (),
                     ((2 * MEM_WIDTH, 1.0, BF16),), min(256, b * n_mem))[0]
        if i % 2 == 0:
            w = w_in_diff[j].astype(BF16)
            nqk = 2 * DIFF_HEADS * HEAD_DIM
            q, k, v, qm = _inproj(
                x2d, g_pre_mix[i], cos_t, sin_t, w[:, :2 * nqk], w[:, 2 * nqk:],
                ((nqk, scale2, BF16), (nqk, 1.0, BF16)),
                ((SEQ_WIDTH, 1.0, BF16), (MEM_WIDTH, scale, BF16)), tm_proj)
            lam_p = jnp.stack([lambda_q1[j], lambda_k1[j], lambda_q2[j], lambda_k2[j]]).astype(F32)
            lam_init = 0.8 - 0.6 * math.exp(-0.3 * i)
            o_seq = _diff_attention(
                q, k, v, cq3.reshape(m // tq_diff, tq_diff, 1), ck3, vis_diff[0], vis_diff[1],
                lam_p, g_diff_subln[j].reshape(1, LANES), b, s, tq_diff, tk_diff, lam_init)
        else:
            w = w_in_dsa[j]
            o0 = 0
            parts = []
            for width in (DSA_HEADS * HEAD_DIM, HEAD_DIM, HEAD_DIM, IDX_HEADS * IDX_DIM, IDX_DIM,
                          IDX_HEADS, MEM_WIDTH):
                parts.append(w[:, o0:o0 + width])
                o0 += width
            wq, wk, wv, wiq, wik, wiw, wqm = parts
            wiw = jnp.pad(wiw, ((0, 0), (0, LANES - IDX_HEADS)))
            w_rope = jnp.concatenate([wq, wiq, _dup(wk), _dup(wik)], axis=1).astype(BF16)
            w_plain = jnp.concatenate([_dup(wv), wiw, wqm], axis=1).astype(BF16)
            iw_scale = IDX_HEADS ** -0.5 * IDX_DIM ** -0.5
            q, iq, kk, ikk, vv, iw, qm = _inproj(
                x2d, g_pre_mix[i], cos_t, sin_t, w_rope, w_plain,
                ((DSA_HEADS * HEAD_DIM, scale2, BF16), (IDX_HEADS * IDX_DIM, 1.0, BF16),
                 (LANES, 1.0, BF16), (LANES, 1.0, BF16)),
                ((LANES, 1.0, BF16), (LANES, iw_scale, F32), (MEM_WIDTH, scale, BF16)), tm_proj)
            o_seq = _dsa_attention(
                q, iq, iw, kk, vv, ikk, cq3.reshape(m // tq_dsa, tq_dsa, 1), ck3,
                vis_dsa[0], vis_dsa[1], b, s, tq_dsa, tk_dsa, top_k)
        x2d = _mix_out(x2d, o_seq, qm, kv, w_out[i].astype(BF16), g_post_mix[i], b, s, n_mem, tm_mix)
        x2d = _mlp(x2d, g_pre_mlp[i], w_mlp_up[i].astype(BF16), w_mlp_down[i].astype(BF16),
                   g_post_mlp[i], tm_mlp, tf_mlp)
    return x2d.reshape(b, s, d)
```

```python
import functools
import math

import jax
import jax.numpy as jnp
from jax import lax
from jax.experimental import pallas as pl
from jax.experimental.pallas import tpu as pltpu

D_MODEL = 1024
CHUNK = 64
HEAD_DIM = 64
LANES = 128
ROPE_THETA = 10000.0
EPS = 1e-6
MEM_HEADS = 4
MEM_WIDTH = MEM_HEADS * HEAD_DIM
SEQ_WIDTH = D_MODEL - MEM_WIDTH
DIFF_HEADS = SEQ_WIDTH // (2 * HEAD_DIM)
DSA_HEADS = SEQ_WIDTH // HEAD_DIM
IDX_HEADS = 8
IDX_DIM = 64
DSA_TOPK_MAX = 256
D_FF = 4 * D_MODEL

NEG = -1e30
INT_MIN = -(2 ** 31)
MASKED_KEY = -2139095041
TOP_PER_LANE = 12
CAND_ROWS = 32
CAND_GROUP = 4
assert TOP_PER_LANE % CAND_GROUP == 0
VMEM_LIMIT = 56 * 1024 * 1024

F32 = jnp.float32
BF16 = jnp.bfloat16
I32 = jnp.int32


def _dot(a, b):
    return jnp.dot(a, b, preferred_element_type=F32)


def _dot_nt(a, b):
    return lax.dot_general(a, b, (((1,), (1,)), ((), ())), preferred_element_type=F32)


def _lane_tile(x, width):
    return jnp.concatenate([x] * (width // LANES), axis=1)


def _key_to_f32(key):
    return lax.bitcast_convert_type(jnp.where(key < 0, key ^ 0x7FFFFFFF, key), F32)


def _rms(x, g):
    return x * lax.rsqrt(jnp.mean(x * x, axis=-1, keepdims=True) + EPS) * g


def _params(sem):
    return pltpu.CompilerParams(dimension_semantics=sem, vmem_limit_bytes=VMEM_LIMIT)


def _inproj_kernel(*refs, rope_outs, plain_outs):
    n_r, n_p = len(rope_outs), len(plain_outs)
    x_ref, g_ref = refs[0], refs[1]
    pos = 2
    if n_r:
        cos_ref, sin_ref, wr_ref = refs[2], refs[3], refs[4]
        pos = 5
    if n_p:
        wp_ref = refs[pos]
        pos += 1
    out_refs = refs[pos:]
    x = x_ref[...]
    h = _rms(x, g_ref[...]).astype(BF16)
    oi = 0
    if n_r:
        cos = cos_ref[...]
        sin = sin_ref[...]
        lane = lax.broadcasted_iota(I32, cos.shape, 1)
        low = (lane % HEAD_DIM) < (HEAD_DIM // 2)
        c0 = 0
        for width, scale in rope_outs:
            o_ref = out_refs[oi]
            oi += 1
            y_all = _dot(h, wr_ref[:, c0:c0 + width])
            for s in range(width // LANES):
                y = y_all[:, s * LANES:(s + 1) * LANES]
                swapped = jnp.where(low, pltpu.roll(y, LANES - HEAD_DIM // 2, 1),
                                    pltpu.roll(y, HEAD_DIM // 2, 1))
                r = y * cos + swapped * sin
                if scale != 1.0:
                    r = r * scale
                o_ref[:, s * LANES:(s + 1) * LANES] = r.astype(o_ref.dtype)
            c0 += width
    c0 = 0
    for width, scale in plain_outs:
        o_ref = out_refs[oi]
        oi += 1
        y = _dot(h, wp_ref[:, c0:c0 + width])
        if scale != 1.0:
            y = y * scale
        o_ref[...] = y.astype(o_ref.dtype)
        c0 += width


def _inproj(x2d, g, cos, sin, w_rope, w_plain, rope_outs, plain_outs, tm):
    m = x2d.shape[0]
    row = lambda i: (i, 0)
    fixed = lambda i: (0, 0)
    in_specs = [pl.BlockSpec((tm, D_MODEL), row), pl.BlockSpec((1, D_MODEL), fixed)]
    args = [x2d, g.reshape(1, D_MODEL)]
    if rope_outs:
        in_specs += [pl.BlockSpec((tm, LANES), row), pl.BlockSpec((tm, LANES), row),
                     pl.BlockSpec(w_rope.shape, fixed)]
        args += [cos, sin, w_rope]
    if plain_outs:
        in_specs += [pl.BlockSpec(w_plain.shape, fixed)]
        args += [w_plain]
    outs = tuple(rope_outs) + tuple(plain_outs)
    out_shape = [jax.ShapeDtypeStruct((m, w), dt) for w, _, dt in outs]
    out_specs = [pl.BlockSpec((tm, w), row) for w, _, _ in outs]
    kern = functools.partial(_inproj_kernel,
                             rope_outs=tuple((w, s) for w, s, _ in rope_outs),
                             plain_outs=tuple((w, s) for w, s, _ in plain_outs))
    return pl.pallas_call(
        kern, grid=(m // tm,), in_specs=in_specs, out_specs=out_specs, out_shape=out_shape,
        compiler_params=_params(("parallel",)), name="inproj")(*args)


def _diff_attn_kernel(status_ref, nhi_ref, q_ref, k_ref, v_ref, cq_ref, ck_ref, lam_ref, gsub_ref,
                      o_ref, m_s, acc_s, s0_s, s1_s, *, tq, tk, nq, nk, lam_init):
    b = pl.program_id(0)
    i = pl.program_id(2)
    q = q_ref[...]
    lane = lax.broadcasted_iota(I32, q.shape, 1)
    zero = jnp.zeros_like(q)
    qs = (jnp.where(lane < HEAD_DIM, q, zero), jnp.where(lane >= HEAD_DIM, q, zero))
    cq = cq_ref[0]
    m_s[...] = jnp.full(m_s.shape, NEG, F32)
    acc_s[...] = jnp.zeros(acc_s.shape, F32)
    base = (b * nq + i) * nk
    n_hi = nhi_ref[b * nq + i]

    def logits(t, dst_s):
        kt = k_ref[pl.ds(pl.multiple_of(t * tk, tk), tk), :]
        for j in range(2):
            dst_s[j] = _dot_nt(qs[j], kt)

    def tile(t, masked, cur_s, nxt_s):
        if nxt_s is not None:
            logits(t + 1, nxt_s)
        off = pl.multiple_of(t * tk, tk)
        vt = jnp.concatenate([v_ref[pl.ds(off, tk), :], jnp.ones((tk, LANES), BF16)], axis=1)
        if masked:
            ck = ck_ref[0, :, pl.ds(off, tk)]
            bias = jnp.where(cq >= ck, 0.0, NEG)
        for j in range(2):
            s = cur_s[j] + bias if masked else cur_s[j]
            m_old = m_s[j]
            m_new = jnp.maximum(m_old, jnp.max(s, axis=1, keepdims=True))
            p = jnp.exp2(s - _lane_tile(m_new, tk))
            alpha = jnp.exp2(m_old - m_new)
            acc_s[j] = _lane_tile(alpha, 2 * LANES) * acc_s[j] + _dot(p.astype(BF16), vt)
            m_s[j] = m_new

    def step(t, cur_s, nxt_s):
        st = status_ref[base + jnp.minimum(t, nk - 1)]
        live = t < n_hi
        more = t + 1 < n_hi

        @pl.when(jnp.logical_and(st == 0, more))
        def _():
            logits(t + 1, nxt_s)

        for code, masked in ((1, False), (2, True)):
            hit = jnp.logical_and(live, st == code)

            @pl.when(jnp.logical_and(hit, more))
            def _():
                tile(t, masked, cur_s, nxt_s)

            @pl.when(jnp.logical_and(hit, jnp.logical_not(more)))
            def _():
                tile(t, masked, cur_s, None)

    logits(0, s0_s)

    def body(u, carry):
        step(2 * u, s0_s, s1_s)
        step(2 * u + 1, s1_s, s0_s)
        return carry

    lax.fori_loop(0, (n_hi + 1) // 2, body, 0)

    lam_p = lam_ref[...]
    lam = (jnp.exp(jnp.sum(lam_p[0:1] * lam_p[1:2], axis=1, keepdims=True))
           - jnp.exp(jnp.sum(lam_p[2:3] * lam_p[3:4], axis=1, keepdims=True)) + lam_init)
    o = (acc_s[0, :, :LANES] / acc_s[0, :, LANES:]
         - lam * (acc_s[1, :, :LANES] / acc_s[1, :, LANES:]))
    o = _rms(o, gsub_ref[...]) * (1.0 - lam_init)
    o_ref[...] = o.astype(o_ref.dtype)


def _diff_attention(q, k, v, cq3, ck3, status, nhi, lam_p, gsub, b, s, tq, tk, lam_init):
    nq, nk = s // tq, s // tk
    kern = functools.partial(_diff_attn_kernel, tq=tq, tk=tk, nq=nq, nk=nk, lam_init=lam_init)
    grid_spec = pltpu.PrefetchScalarGridSpec(
        num_scalar_prefetch=2,
        grid=(b, DIFF_HEADS, nq),
        in_specs=[
            pl.BlockSpec((tq, LANES), lambda bb, h, i, *_: (bb * nq + i, h)),
            pl.BlockSpec((s, LANES), lambda bb, h, i, *_: (bb, h)),
            pl.BlockSpec((s, LANES), lambda bb, h, i, *_: (bb, h)),
            pl.BlockSpec((1, tq, 1), lambda bb, h, i, *_: (bb * nq + i, 0, 0)),
            pl.BlockSpec((1, 1, s), lambda bb, h, i, *_: (bb, 0, 0)),
            pl.BlockSpec((4, HEAD_DIM), lambda bb, h, i, *_: (0, 0)),
            pl.BlockSpec((1, LANES), lambda bb, h, i, *_: (0, 0)),
        ],
        out_specs=pl.BlockSpec((tq, LANES), lambda bb, h, i, *_: (bb * nq + i, h)),
        scratch_shapes=[pltpu.VMEM((2, tq, LANES), F32),
                        pltpu.VMEM((2, tq, 2 * LANES), F32),
                        pltpu.VMEM((2, tq, tk), F32),
                        pltpu.VMEM((2, tq, tk), F32)],
    )
    return pl.pallas_call(
        kern, grid_spec=grid_spec,
        out_shape=jax.ShapeDtypeStruct((b * s, SEQ_WIDTH), BF16),
        compiler_params=_params(("parallel", "parallel", "arbitrary")), name="diff_attn",
    )(status, nhi, q, k, v, cq3, ck3, lam_p, gsub)


def _dsa_kernel(status_ref, nhi_ref, q_ref, iq_ref, iw_ref, kk_ref, vv_ref, ikk_ref, cq_ref, ck_ref,
                o_ref, keys_s, cand_s, qst_s, iqst_s, iwb_s, m_s, acc_s, thr_s, thf_s, jst_s,
                *, tq, tk, nq, nk, s_len, top_k):
    b = pl.program_id(0)
    i = pl.program_id(1)
    base = (b * nq + i) * nk
    n_hi = nhi_ref[b * nq + i]
    cq = cq_ref[0]
    lane = lax.broadcasted_iota(I32, (tq, LANES), 1)
    first = lane < HEAD_DIM

    for h in range(DSA_HEADS):
        slab = q_ref[:, (h // 2) * LANES:(h // 2 + 1) * LANES]
        keep = first if h % 2 == 0 else jnp.logical_not(first)
        qst_s[h * tq:(h + 1) * tq, :] = jnp.where(keep, slab, jnp.zeros_like(slab))
    for h in range(IDX_HEADS):
        slab = iq_ref[:, (h // 2) * LANES:(h // 2 + 1) * LANES]
        keep = first if h % 2 == 0 else jnp.logical_not(first)
        iqst_s[h * tq:(h + 1) * tq, :] = jnp.where(keep, slab, jnp.zeros_like(slab))
        iwb_s[h * tq:(h + 1) * tq, :] = jnp.broadcast_to(iw_ref[:, h:h + 1], (tq, LANES))

    cand_s[...] = jnp.full(cand_s.shape, -jnp.inf, F32)

    def fold_candidates(sc):
        for g in range(tq // CAND_ROWS):
            rows = slice(g * CAND_ROWS, (g + 1) * CAND_ROWS)
            tops = [cand_s[r, rows, :] for r in range(TOP_PER_LANE)]
            for c in range(tk // LANES):
                x = sc[rows, c * LANES:(c + 1) * LANES]
                for r in range(TOP_PER_LANE):
                    tops[r], x = jnp.maximum(tops[r], x), jnp.minimum(tops[r], x)
            for r in range(TOP_PER_LANE):
                cand_s[r, rows, :] = tops[r]

    def score_body(t, carry):
        st = status_ref[base + t]
        off = pl.multiple_of(t * tk, tk)

        @pl.when(st == 0)
        def _():
            keys_s[:, pl.ds(off, tk)] = jnp.full((tq, tk), -jnp.inf, F32)

        def scores(masked):
            ikt = ikk_ref[pl.ds(off, tk), :]
            logits = _dot_nt(iqst_s[...], ikt)
            r = jnp.maximum(logits, 0.0) * _lane_tile(iwb_s[...], tk)
            sc = r[0:tq]
            for h in range(1, IDX_HEADS):
                sc = sc + r[h * tq:(h + 1) * tq]
            sc = jnp.where(sc == 0.0, 0.0, sc)
            if masked:
                ck = ck_ref[0, :, pl.ds(off, tk)]
                sc = jnp.where(cq >= ck, sc, -jnp.inf)
            keys_s[:, pl.ds(off, tk)] = sc
            fold_candidates(sc)

        @pl.when(st == 1)
        def _():
            scores(False)

        @pl.when(st == 2)
        def _():
            scores(True)

        return carry

    lax.fori_loop(0, n_hi, score_body, 0)

    rg = min(tq, LANES)
    lane_rg = lax.broadcasted_iota(I32, (rg, LANES), 1)

    def count_tiles(preds, *operands):
        zero = jnp.zeros((rg, LANES), I32)
        accs = []
        for r in range(tq // rg):
            rows = slice(r * rg, (r + 1) * rg)
            ops = [o[rows] for o in operands]

            def body(t, acc, rows=rows, ops=ops):
                off = pl.multiple_of(t * tk, tk)
                acc = list(acc)
                for c in range(tk // LANES):
                    slab = keys_s[rows, pl.ds(off + c * LANES, LANES)]
                    for n, pred in enumerate(preds):
                        acc[n] = acc[n] + jnp.where(pred(slab, off + c * LANES, *ops), 1, 0).astype(I32)
                return tuple(acc)
            accs.append(lax.fori_loop(0, n_hi, body, (zero,) * len(preds)))
        outs = []
        for n in range(len(preds)):
            acc = accs[0][n] if len(accs) == 1 else jnp.concatenate([a[n] for a in accs], axis=0)
            outs.append(jnp.broadcast_to(jnp.sum(acc, axis=1, keepdims=True), (tq, LANES)))
        return outs

    ge = lambda slab, col0, c: slab >= c
    gt = lambda slab, col0, c: slab > c
    eq = lambda slab, col0, c: slab == c

    def count_cand(rows, c):
        n_groups = TOP_PER_LANE // CAND_GROUP
        full = [cand_s[CAND_GROUP * i + CAND_GROUP - 1, rows, :] >= c for i in range(n_groups)]
        base = jnp.full(c.shape, TOP_PER_LANE, I32)
        rest = [jnp.full(c.shape, -jnp.inf, F32)] * (CAND_GROUP - 1)
        for i in reversed(range(n_groups)):
            base = jnp.where(full[i], base, CAND_GROUP * i)
            rest = [jnp.where(full[i], rest[k], cand_s[CAND_GROUP * i + k, rows, :])
                    for k in range(CAND_GROUP - 1)]
        for x in rest:
            base = base + jnp.where(x >= c, 1, 0).astype(I32)
        return base

    def bisect(count):
        def bit_body(j, thr):
            cand = thr + lax.shift_left(jnp.int32(1), 31 - j)
            cnt, = count([ge], _key_to_f32(cand))
            return jnp.where(cnt >= top_k, cand, thr)
        return lax.fori_loop(0, 32, bit_body, jnp.full((tq, LANES), INT_MIN, I32))

    def bisect_candidates():
        half = tq // 2
        halves = (slice(0, half), slice(half, tq))

        def raw_count(rows, key):
            return count_cand(rows, _key_to_f32(key))

        def decide(acc, cand, thr):
            cnt = jnp.broadcast_to(jnp.sum(acc, axis=1, keepdims=True), (half, LANES))
            return jnp.where(cnt >= top_k, cand, thr)

        bit = lambda j: lax.shift_left(jnp.int32(1), 31 - j)
        start = jnp.full((half, LANES), INT_MIN, I32)

        def body(j, carry):
            thr_a, cand_a, acc_a, thr_b = carry
            cand_b = thr_b + bit(j)
            acc_b = raw_count(halves[1], cand_b)
            thr_a = decide(acc_a, cand_a, thr_a)
            cand_a = thr_a + bit(j + 1)
            acc_a = raw_count(halves[0], cand_a)
            thr_b = decide(acc_b, cand_b, thr_b)
            return thr_a, cand_a, acc_a, thr_b

        cand_a = start + bit(0)
        carry = (start, cand_a, raw_count(halves[0], cand_a), start)
        thr_a, cand_a, acc_a, thr_b = lax.fori_loop(0, 31, body, carry)
        thr_a = decide(acc_a, cand_a, thr_a)
        cand_b = thr_b + bit(31)
        thr_b = decide(raw_count(halves[1], cand_b), cand_b, thr_b)
        return jnp.concatenate([thr_a, thr_b], axis=0)

    thr = bisect_candidates()
    floor = _key_to_f32(jnp.maximum(thr, MASKED_KEY + 1))
    n_cand = jnp.broadcast_to(
        jnp.sum(count_cand(slice(0, tq), floor), axis=1, keepdims=True), (tq, LANES))
    n_all, = count_tiles([ge], floor)
    thr_s[...] = thr

    @pl.when(jnp.max((n_all != n_cand).astype(I32)) > 0)
    def _():
        thr_s[...] = bisect(count_tiles)

    thr_key = thr_s[...]
    real = thr_key > MASKED_KEY
    thr = jnp.where(real, _key_to_f32(thr_key), -jnp.inf)
    thf_s[...] = thr
    cnt_gt, cnt_eq = count_tiles([gt, eq], thr)
    need = top_k - cnt_gt
    excess = jnp.logical_and(real, cnt_eq > need)
    jst_s[...] = jnp.where(real, jnp.int32(s_len), jnp.int32(-1))

    @pl.when(jnp.max(excess.astype(I32)) > 0)
    def _():
        def idx_body(j, jcur):
            cand = jcur + lax.shift_left(jnp.int32(1), (s_len.bit_length() - 2) - j)
            cnt, = count_tiles(
                [lambda slab, col0, th, c: jnp.logical_and(slab == th, (col0 + lane_rg) < c)], thr, cand)
            return jnp.where(cnt < need, cand, jcur)
        jbest = lax.fori_loop(0, s_len.bit_length() - 1, idx_body, jnp.zeros((tq, LANES), I32))
        jst_s[...] = jnp.where(excess, jbest, jst_s[...])

    m_s[...] = jnp.full(m_s.shape, NEG, F32)
    acc_s[...] = jnp.zeros(acc_s.shape, F32)
    lane_k = lax.broadcasted_iota(I32, (tq, tk), 1)
    lane_v = lax.broadcasted_iota(I32, (tk, LANES), 1)

    def attn_body(t, carry):
        st = status_ref[base + t]

        @pl.when(st != 0)
        def _():
            off = pl.multiple_of(t * tk, tk)
            key = keys_s[:, pl.ds(off, tk)]
            th = _lane_tile(thf_s[...], tk)
            jst = _lane_tile(jst_s[...], tk)
            sel = jnp.logical_or(key > th, jnp.logical_and(key == th, (lane_k + off) <= jst))
            bias = jnp.where(sel, 0.0, NEG)
            kt = kk_ref[pl.ds(off, tk), :]
            vt = vv_ref[pl.ds(off, tk), :]
            vt = jnp.where(lane_v < HEAD_DIM, vt, jnp.ones_like(vt))
            s = _dot_nt(qst_s[...], kt)
            s = (s.reshape(DSA_HEADS, tq, tk) + bias[None]).reshape(DSA_HEADS * tq, tk)
            m_old = m_s[...]
            m_new = jnp.maximum(m_old, jnp.max(s, axis=1, keepdims=True))
            p = jnp.exp2(s - _lane_tile(m_new, tk))
            alpha = jnp.exp2(m_old - m_new)
            acc_s[...] = alpha * acc_s[...] + _dot(p.astype(BF16), vt)
            m_s[...] = m_new

        return carry

    lax.fori_loop(0, n_hi, attn_body, 0)

    for j in range(DSA_HEADS // 2):
        acc_a = acc_s[2 * j * tq:(2 * j + 1) * tq]
        acc_b = acc_s[(2 * j + 1) * tq:(2 * j + 2) * tq]
        swap_a = pltpu.roll(acc_a, HEAD_DIM, 1)
        swap_b = pltpu.roll(acc_b, HEAD_DIM, 1)
        o_ref[:, j * LANES:(j + 1) * LANES] = jnp.where(first, acc_a / swap_a,
                                                        swap_b / acc_b).astype(o_ref.dtype)


def _dsa_attention(q, iq, iw, kk, vv, ikk, cq3, ck3, status, nhi, b, s, tq, tk, top_k):
    nq, nk = s // tq, s // tk
    kern = functools.partial(_dsa_kernel, tq=tq, tk=tk, nq=nq, nk=nk, s_len=s, top_k=top_k)
    rowblk = lambda bb, i, *_: (bb * nq + i, 0)
    perb = lambda bb, i, *_: (bb, 0)
    grid_spec = pltpu.PrefetchScalarGridSpec(
        num_scalar_prefetch=2,
        grid=(b, nq),
        in_specs=[
            pl.BlockSpec((tq, SEQ_WIDTH), rowblk),
            pl.BlockSpec((tq, IDX_HEADS * IDX_DIM), rowblk),
            pl.BlockSpec((tq, LANES), rowblk),
            pl.BlockSpec((s, LANES), perb),
            pl.BlockSpec((s, LANES), perb),
            pl.BlockSpec((s, LANES), perb),
            pl.BlockSpec((1, tq, 1), lambda bb, i, *_: (bb * nq + i, 0, 0)),
            pl.BlockSpec((1, 1, s), lambda bb, i, *_: (bb, 0, 0)),
        ],
        out_specs=pl.BlockSpec((tq, SEQ_WIDTH), rowblk),
        scratch_shapes=[
            pltpu.VMEM((tq, s), F32),
            pltpu.VMEM((TOP_PER_LANE, tq, LANES), F32),
            pltpu.VMEM((DSA_HEADS * tq, LANES), BF16),
            pltpu.VMEM((IDX_HEADS * tq, LANES), BF16),
            pltpu.VMEM((IDX_HEADS * tq, LANES), F32),
            pltpu.VMEM((DSA_HEADS * tq, LANES), F32),
            pltpu.VMEM((DSA_HEADS * tq, LANES), F32),
            pltpu.VMEM((tq, LANES), I32),
            pltpu.VMEM((tq, LANES), F32),
            pltpu.VMEM((tq, LANES), I32),
        ],
    )
    return pl.pallas_call(
        kern, grid_spec=grid_spec,
        out_shape=jax.ShapeDtypeStruct((b * s, SEQ_WIDTH), BF16),
        compiler_params=_params(("parallel", "arbitrary")), name="dsa_attn",
    )(status, nhi, q, iq, iw, kk, vv, ikk, cq3, ck3)


def _mix_out_kernel(x_ref, oseq_ref, qm_ref, km_ref, vm_ref, wo_ref, g_ref, o_ref):
    qm = qm_ref[...]
    km = km_ref[...]
    vm = vm_ref[...]
    lane = lax.broadcasted_iota(I32, qm.shape, 1)
    o_mem = jnp.zeros(qm.shape, F32)
    for h in range(MEM_HEADS):
        mine = (lane // HEAD_DIM) == h
        s = _dot_nt(jnp.where(mine, qm, jnp.zeros_like(qm)), km)
        p = jnp.exp(s - jnp.max(s, axis=1, keepdims=True))
        p = p / jnp.sum(p, axis=1, keepdims=True)
        o_mem = o_mem + jnp.where(mine, _dot(p.astype(BF16), vm), 0.0)
    y = _dot(oseq_ref[...], wo_ref[0:SEQ_WIDTH, :]) + _dot(o_mem.astype(BF16), wo_ref[SEQ_WIDTH:, :])
    o_ref[...] = x_ref[...] + _rms(y, g_ref[...])


def _mix_out(x2d, oseq, qm, kv, w_out, g, b, s, n_mem, tm):
    nt = s // tm
    return pl.pallas_call(
        _mix_out_kernel, grid=(b, nt),
        in_specs=[
            pl.BlockSpec((tm, D_MODEL), lambda bb, i: (bb * nt + i, 0)),
            pl.BlockSpec((tm, SEQ_WIDTH), lambda bb, i: (bb * nt + i, 0)),
            pl.BlockSpec((tm, MEM_WIDTH), lambda bb, i: (bb * nt + i, 0)),
            pl.BlockSpec((n_mem, MEM_WIDTH), lambda bb, i: (bb, 0)),
            pl.BlockSpec((n_mem, MEM_WIDTH), lambda bb, i: (bb, 1)),
            pl.BlockSpec((D_MODEL, D_MODEL), lambda bb, i: (0, 0)),
            pl.BlockSpec((1, D_MODEL), lambda bb, i: (0, 0)),
        ],
        out_specs=pl.BlockSpec((tm, D_MODEL), lambda bb, i: (bb * nt + i, 0)),
        out_shape=jax.ShapeDtypeStruct(x2d.shape, F32),
        compiler_params=_params(("parallel", "parallel")), name="mix_out",
    )(x2d, oseq, qm, kv, kv, w_out, g.reshape(1, D_MODEL))


def _mlp_kernel(x_ref, gpre_ref, wup_ref, wdn_ref, gpost_ref, o_ref, h_s, acc_s):
    j = pl.program_id(1)

    @pl.when(j == 0)
    def _():
        h_s[...] = _rms(x_ref[...], gpre_ref[...]).astype(BF16)
        acc_s[...] = jnp.zeros(acc_s.shape, F32)

    u = jnp.maximum(_dot(h_s[...], wup_ref[...]), 0.0)
    acc_s[...] += _dot((u * u).astype(BF16), wdn_ref[...])

    @pl.when(j == pl.num_programs(1) - 1)
    def _():
        o_ref[...] = x_ref[...] + _rms(acc_s[...], gpost_ref[...])


def _mlp(x2d, g_pre, w_up, w_dn, g_post, tm, tf):
    m = x2d.shape[0]
    return pl.pallas_call(
        _mlp_kernel, grid=(m // tm, D_FF // tf),
        in_specs=[
            pl.BlockSpec((tm, D_MODEL), lambda i, j: (i, 0)),
            pl.BlockSpec((1, D_MODEL), lambda i, j: (0, 0)),
            pl.BlockSpec((D_MODEL, tf), lambda i, j: (0, j)),
            pl.BlockSpec((tf, D_MODEL), lambda i, j: (j, 0)),
            pl.BlockSpec((1, D_MODEL), lambda i, j: (0, 0)),
        ],
        out_specs=pl.BlockSpec((tm, D_MODEL), lambda i, j: (i, 0)),
        out_shape=jax.ShapeDtypeStruct(x2d.shape, F32),
        scratch_shapes=[pltpu.VMEM((tm, D_MODEL), BF16), pltpu.VMEM((tm, D_MODEL), F32)],
        compiler_params=_params(("parallel", "arbitrary")), name="mlp",
    )(x2d, g_pre.reshape(1, D_MODEL), w_up, w_dn, g_post.reshape(1, D_MODEL))


def _rope_tables(positions):
    half = HEAD_DIM // 2
    inv = ROPE_THETA ** (-jnp.arange(0, HEAD_DIM, 2, dtype=F32) / HEAD_DIM)
    ang = positions.astype(F32).reshape(-1, 1) * inv
    cos, sin = jnp.cos(ang), jnp.sin(ang)
    reps = LANES // HEAD_DIM
    cos_t = jnp.tile(cos, (1, 2 * reps))
    sin_t = jnp.tile(jnp.concatenate([-sin, sin], axis=1), (1, reps))
    assert cos_t.shape[1] == LANES and half * 2 == HEAD_DIM
    return cos_t, sin_t


def _visibility(chunk_id, tq, tk):
    b, s = chunk_id.shape
    cq = chunk_id.reshape(b, s // tq, tq)
    ck = chunk_id.reshape(b, s // tk, tk)
    qmin, qmax = cq.min(-1)[:, :, None], cq.max(-1)[:, :, None]
    kmin, kmax = ck.min(-1)[:, None, :], ck.max(-1)[:, None, :]
    status = jnp.where(kmin > qmax, 0, jnp.where(kmax <= qmin, 1, 2)).astype(I32)
    tiles = jnp.arange(s // tk, dtype=I32)[None, None, :]
    nhi = jnp.max(jnp.where(status != 0, tiles + 1, 0), axis=-1).astype(I32)
    return status.reshape(-1), nhi.reshape(-1)


def _dup(w):
    return jnp.concatenate([w, w], axis=1)


def kernel(x, mem, positions, g_pre_mix, g_post_mix, g_mem, w_mem_kv, w_out, g_pre_mlp, g_post_mlp,
           w_mlp_up, w_mlp_down, w_in_diff, lambda_q1, lambda_k1, lambda_q2, lambda_k2, g_diff_subln,
           w_in_dsa):
    b, s, d = x.shape
    n_mem = mem.shape[1]
    depth = g_pre_mix.shape[0]
    m = b * s
    scale = HEAD_DIM ** -0.5
    scale2 = scale * math.log2(math.e)
    tm_proj = min(512, s)
    tq_diff, tk_diff = min(512, s), min(1024, s)
    tq_dsa, tk_dsa = min(256, s), min(512, s)
    tm_mix = min(512, s)
    tm_mlp, tf_mlp = min(1024, m), 1024
    top_k = min(DSA_TOPK_MAX, s // 4)
    assert tk_dsa >= top_k, "the threshold search needs one key tile to hold top_k candidates"

    cos_t, sin_t = _rope_tables(positions)
    chunk_id = positions // CHUNK
    cq3 = chunk_id.reshape(-1, 1)
    ck3 = chunk_id.reshape(b, 1, s)
    vis_diff = _visibility(chunk_id, tq_diff, tk_diff)
    vis_dsa = _visibility(chunk_id, tq_dsa, tk_dsa)

    x2d = x.reshape(m, d)
    mem2d = mem.reshape(b * n_mem, d)
    for i in range(depth):
        j = i // 2
        kv = _inproj(mem2d, g_mem[i], None, None, None, w_mem_kv[i].astype(BF16), (),
                     ((2 * MEM_WIDTH, 1.0, BF16),), min(256, b * n_mem))[0]
        if i % 2 == 0:
            w = w_in_diff[j].astype(BF16)
            nqk = 2 * DIFF_HEADS * HEAD_DIM
            q, k, v, qm = _inproj(
                x2d, g_pre_mix[i], cos_t, sin_t, w[:, :2 * nqk], w[:, 2 * nqk:],
                ((nqk, scale2, BF16), (nqk, 1.0, BF16)),
                ((SEQ_WIDTH, 1.0, BF16), (MEM_WIDTH, scale, BF16)), tm_proj)
            lam_p = jnp.stack([lambda_q1[j], lambda_k1[j], lambda_q2[j], lambda_k2[j]]).astype(F32)
            lam_init = 0.8 - 0.6 * math.exp(-0.3 * i)
            o_seq = _diff_attention(
                q, k, v, cq3.reshape(m // tq_diff, tq_diff, 1), ck3, vis_diff[0], vis_diff[1],
                lam_p, g_diff_subln[j].reshape(1, LANES), b, s, tq_diff, tk_diff, lam_init)
        else:
            w = w_in_dsa[j]
            o0 = 0
            parts = []
            for width in (DSA_HEADS * HEAD_DIM, HEAD_DIM, HEAD_DIM, IDX_HEADS * IDX_DIM, IDX_DIM,
                          IDX_HEADS, MEM_WIDTH):
                parts.append(w[:, o0:o0 + width])
                o0 += width
            wq, wk, wv, wiq, wik, wiw, wqm = parts
            wiw = jnp.pad(wiw, ((0, 0), (0, LANES - IDX_HEADS)))
            w_rope = jnp.concatenate([wq, wiq, _dup(wk), _dup(wik)], axis=1).astype(BF16)
            w_plain = jnp.concatenate([_dup(wv), wiw, wqm], axis=1).astype(BF16)
            iw_scale = IDX_HEADS ** -0.5 * IDX_DIM ** -0.5
            q, iq, kk, ikk, vv, iw, qm = _inproj(
                x2d, g_pre_mix[i], cos_t, sin_t, w_rope, w_plain,
                ((DSA_HEADS * HEAD_DIM, scale2, BF16), (IDX_HEADS * IDX_DIM, 1.0, BF16),
                 (LANES, 1.0, BF16), (LANES, 1.0, BF16)),
                ((LANES, 1.0, BF16), (LANES, iw_scale, F32), (MEM_WIDTH, scale, BF16)), tm_proj)
            o_seq = _dsa_attention(
                q, iq, iw, kk, vv, ikk, cq3.reshape(m // tq_dsa, tq_dsa, 1), ck3,
                vis_dsa[0], vis_dsa[1], b, s, tq_dsa, tk_dsa, top_k)
        x2d = _mix_out(x2d, o_seq, qm, kv, w_out[i].astype(BF16), g_post_mix[i], b, s, n_mem, tm_mix)
        x2d = _mlp(x2d, g_pre_mlp[i], w_mlp_up[i].astype(BF16), w_mlp_down[i].astype(BF16),
                   g_post_mlp[i], tm_mlp, tf_mlp)
    return x2d.reshape(b, s, d)
```

```python
import functools
import math

import jax
import jax.numpy as jnp
from jax import lax
from jax.experimental import pallas as pl
from jax.experimental.pallas import tpu as pltpu

D_MODEL = 1024
CHUNK = 64
HEAD_DIM = 64
LANES = 128
ROPE_THETA = 10000.0
EPS = 1e-6
MEM_HEADS = 4
MEM_WIDTH = MEM_HEADS * HEAD_DIM
SEQ_WIDTH = D_MODEL - MEM_WIDTH
DIFF_HEADS = SEQ_WIDTH // (2 * HEAD_DIM)
DSA_HEADS = SEQ_WIDTH // HEAD_DIM
IDX_HEADS = 8
IDX_DIM = 64
DSA_TOPK_MAX = 256
D_FF = 4 * D_MODEL

NEG = -1e30
INT_MIN = -(2 ** 31)
MASKED_KEY = -2139095041
TOP_PER_LANE = 12
CAND_ROWS = 32
CAND_GROUP = 4
assert TOP_PER_LANE % CAND_GROUP == 0
VMEM_LIMIT = 56 * 1024 * 1024

F32 = jnp.float32
BF16 = jnp.bfloat16
I32 = jnp.int32


def _dot(a, b):
    return jnp.dot(a, b, preferred_element_type=F32)


def _dot_nt(a, b):
    return lax.dot_general(a, b, (((1,), (1,)), ((), ())), preferred_element_type=F32)


def _lane_tile(x, width):
    return jnp.concatenate([x] * (width // LANES), axis=1)


def _key_to_f32(key):
    return lax.bitcast_convert_type(jnp.where(key < 0, key ^ 0x7FFFFFFF, key), F32)


def _rms(x, g):
    return x * lax.rsqrt(jnp.mean(x * x, axis=-1, keepdims=True) + EPS) * g


def _params(sem):
    return pltpu.CompilerParams(dimension_semantics=sem, vmem_limit_bytes=VMEM_LIMIT)


def _inproj_kernel(*refs, rope_outs, plain_outs):
    n_r, n_p = len(rope_outs), len(plain_outs)
    x_ref, g_ref = refs[0], refs[1]
    pos = 2
    if n_r:
        cos_ref, sin_ref, wr_ref = refs[2], refs[3], refs[4]
        pos = 5
    if n_p:
        wp_ref = refs[pos]
        pos += 1
    out_refs = refs[pos:]
    x = x_ref[...]
    h = _rms(x, g_ref[...]).astype(BF16)
    oi = 0
    if n_r:
        cos = cos_ref[...]
        sin = sin_ref[...]
        lane = lax.broadcasted_iota(I32, cos.shape, 1)
        low = (lane % HEAD_DIM) < (HEAD_DIM // 2)
        c0 = 0
        for width, scale in rope_outs:
            o_ref = out_refs[oi]
            oi += 1
            y_all = _dot(h, wr_ref[:, c0:c0 + width])
            for s in range(width // LANES):
                y = y_all[:, s * LANES:(s + 1) * LANES]
                swapped = jnp.where(low, pltpu.roll(y, LANES - HEAD_DIM // 2, 1),
                                    pltpu.roll(y, HEAD_DIM // 2, 1))
                r = y * cos + swapped * sin
                if scale != 1.0:
                    r = r * scale
                o_ref[:, s * LANES:(s + 1) * LANES] = r.astype(o_ref.dtype)
            c0 += width
    c0 = 0
    for width, scale in plain_outs:
        o_ref = out_refs[oi]
        oi += 1
        y = _dot(h, wp_ref[:, c0:c0 + width])
        if scale != 1.0:
            y = y * scale
        o_ref[...] = y.astype(o_ref.dtype)
        c0 += width


def _inproj(x2d, g, cos, sin, w_rope, w_plain, rope_outs, plain_outs, tm):
    m = x2d.shape[0]
    row = lambda i: (i, 0)
    fixed = lambda i: (0, 0)
    in_specs = [pl.BlockSpec((tm, D_MODEL), row), pl.BlockSpec((1, D_MODEL), fixed)]
    args = [x2d, g.reshape(1, D_MODEL)]
    if rope_outs:
        in_specs += [pl.BlockSpec((tm, LANES), row), pl.BlockSpec((tm, LANES), row),
                     pl.BlockSpec(w_rope.shape, fixed)]
        args += [cos, sin, w_rope]
    if plain_outs:
        in_specs += [pl.BlockSpec(w_plain.shape, fixed)]
        args += [w_plain]
    outs = tuple(rope_outs) + tuple(plain_outs)
    out_shape = [jax.ShapeDtypeStruct((m, w), dt) for w, _, dt in outs]
    out_specs = [pl.BlockSpec((tm, w), row) for w, _, _ in outs]
    kern = functools.partial(_inproj_kernel,
                             rope_outs=tuple((w, s) for w, s, _ in rope_outs),
                             plain_outs=tuple((w, s) for w, s, _ in plain_outs))
    return pl.pallas_call(
        kern, grid=(m // tm,), in_specs=in_specs, out_specs=out_specs, out_shape=out_shape,
        compiler_params=_params(("parallel",)), name="inproj")(*args)


def _diff_attn_kernel(status_ref, nhi_ref, q_ref, k_ref, v_ref, cq_ref, ck_ref, lam_ref, gsub_ref,
                      o_ref, m_s, acc_s, s0_s, s1_s, *, tq, tk, nq, nk, lam_init):
    b = pl.program_id(0)
    i = pl.program_id(2)
    q = q_ref[...]
    lane = lax.broadcasted_iota(I32, q.shape, 1)
    zero = jnp.zeros_like(q)
    qs = (jnp.where(lane < HEAD_DIM, q, zero), jnp.where(lane >= HEAD_DIM, q, zero))
    cq = cq_ref[0]
    m_s[...] = jnp.full(m_s.shape, NEG, F32)
    acc_s[...] = jnp.zeros(acc_s.shape, F32)
    base = (b * nq + i) * nk
    n_hi = nhi_ref[b * nq + i]

    def logits(t, dst_s):
        kt = k_ref[pl.ds(pl.multiple_of(t * tk, tk), tk), :]
        for j in range(2):
            dst_s[j] = _dot_nt(qs[j], kt)

    def tile(t, masked, cur_s, nxt_s):
        if nxt_s is not None:
            logits(t + 1, nxt_s)
        off = pl.multiple_of(t * tk, tk)
        vt = jnp.concatenate([v_ref[pl.ds(off, tk), :], jnp.ones((tk, LANES), BF16)], axis=1)
        if masked:
            ck = ck_ref[0, :, pl.ds(off, tk)]
            bias = jnp.where(cq >= ck, 0.0, NEG)
        for j in range(2):
            s = cur_s[j] + bias if masked else cur_s[j]
            m_old = m_s[j]
            m_new = jnp.maximum(m_old, jnp.max(s, axis=1, keepdims=True))
            p = jnp.exp2(s - _lane_tile(m_new, tk))
            alpha = jnp.exp2(m_old - m_new)
            acc_s[j] = _lane_tile(alpha, 2 * LANES) * acc_s[j] + _dot(p.astype(BF16), vt)
            m_s[j] = m_new

    def step(t, cur_s, nxt_s):
        st = status_ref[base + jnp.minimum(t, nk - 1)]
        live = t < n_hi
        more = t + 1 < n_hi

        @pl.when(jnp.logical_and(st == 0, more))
        def _():
            logits(t + 1, nxt_s)

        for code, masked in ((1, False), (2, True)):
            hit = jnp.logical_and(live, st == code)

            @pl.when(jnp.logical_and(hit, more))
            def _():
                tile(t, masked, cur_s, nxt_s)

            @pl.when(jnp.logical_and(hit, jnp.logical_not(more)))
            def _():
                tile(t, masked, cur_s, None)

    logits(0, s0_s)

    def body(u, carry):
        step(2 * u, s0_s, s1_s)
        step(2 * u + 1, s1_s, s0_s)
        return carry

    lax.fori_loop(0, (n_hi + 1) // 2, body, 0)

    lam_p = lam_ref[...]
    lam = (jnp.exp(jnp.sum(lam_p[0:1] * lam_p[1:2], axis=1, keepdims=True))
           - jnp.exp(jnp.sum(lam_p[2:3] * lam_p[3:4], axis=1, keepdims=True)) + lam_init)
    o = (acc_s[0, :, :LANES] / acc_s[0, :, LANES:]
         - lam * (acc_s[1, :, :LANES] / acc_s[1, :, LANES:]))
    o = _rms(o, gsub_ref[...]) * (1.0 - lam_init)
    o_ref[...] = o.astype(o_ref.dtype)


def _diff_attention(q, k, v, cq3, ck3, status, nhi, lam_p, gsub, b, s, tq, tk, lam_init):
    nq, nk = s // tq, s // tk
    kern = functools.partial(_diff_attn_kernel, tq=tq, tk=tk, nq=nq, nk=nk, lam_init=lam_init)
    grid_spec = pltpu.PrefetchScalarGridSpec(
        num_scalar_prefetch=2,
        grid=(b, DIFF_HEADS, nq),
        in_specs=[
            pl.BlockSpec((tq, LANES), lambda bb, h, i, *_: (bb * nq + i, h)),
            pl.BlockSpec((s, LANES), lambda bb, h, i, *_: (bb, h)),
            pl.BlockSpec((s, LANES), lambda bb, h, i, *_: (bb, h)),
            pl.BlockSpec((1, tq, 1), lambda bb, h, i, *_: (bb * nq + i, 0, 0)),
            pl.BlockSpec((1, 1, s), lambda bb, h, i, *_: (bb, 0, 0)),
            pl.BlockSpec((4, HEAD_DIM), lambda bb, h, i, *_: (0, 0)),
            pl.BlockSpec((1, LANES), lambda bb, h, i, *_: (0, 0)),
        ],
        out_specs=pl.BlockSpec((tq, LANES), lambda bb, h, i, *_: (bb * nq + i, h)),
        scratch_shapes=[pltpu.VMEM((2, tq, LANES), F32),
                        pltpu.VMEM((2, tq, 2 * LANES), F32),
                        pltpu.VMEM((2, tq, tk), F32),
                        pltpu.VMEM((2, tq, tk), F32)],
    )
    return pl.pallas_call(
        kern, grid_spec=grid_spec,
        out_shape=jax.ShapeDtypeStruct((b * s, SEQ_WIDTH), BF16),
        compiler_params=_params(("parallel", "parallel", "arbitrary")), name="diff_attn",
    )(status, nhi, q, k, v, cq3, ck3, lam_p, gsub)


def _dsa_kernel(status_ref, nhi_ref, q_ref, iq_ref, iw_ref, kk_ref, vv_ref, ikk_ref, cq_ref, ck_ref,
                o_ref, keys_s, cand_s, qst_s, iqst_s, iwb_s, m_s, acc_s, thr_s, thf_s, jst_s,
                *, tq, tk, nq, nk, s_len, top_k):
    b = pl.program_id(0)
    i = pl.program_id(1)
    base = (b * nq + i) * nk
    n_hi = nhi_ref[b * nq + i]
    cq = cq_ref[0]
    lane = lax.broadcasted_iota(I32, (tq, LANES), 1)
    first = lane < HEAD_DIM

    for h in range(DSA_HEADS):
        slab = q_ref[:, (h // 2) * LANES:(h // 2 + 1) * LANES]
        keep = first if h % 2 == 0 else jnp.logical_not(first)
        qst_s[h * tq:(h + 1) * tq, :] = jnp.where(keep, slab, jnp.zeros_like(slab))
    for h in range(IDX_HEADS):
        slab = iq_ref[:, (h // 2) * LANES:(h // 2 + 1) * LANES]
        keep = first if h % 2 == 0 else jnp.logical_not(first)
        iqst_s[h * tq:(h + 1) * tq, :] = jnp.where(keep, slab, jnp.zeros_like(slab))
        iwb_s[h * tq:(h + 1) * tq, :] = jnp.broadcast_to(iw_ref[:, h:h + 1], (tq, LANES))

    cand_s[...] = jnp.full(cand_s.shape, -jnp.inf, F32)

    def fold_candidates(sc):
        for g in range(tq // CAND_ROWS):
            rows = slice(g * CAND_ROWS, (g + 1) * CAND_ROWS)
            tops = [cand_s[r, rows, :] for r in range(TOP_PER_LANE)]
            for c in range(tk // LANES):
                x = sc[rows, c * LANES:(c + 1) * LANES]
                for r in range(TOP_PER_LANE):
                    tops[r], x = jnp.maximum(tops[r], x), jnp.minimum(tops[r], x)
            for r in range(TOP_PER_LANE):
                cand_s[r, rows, :] = tops[r]

    def score_body(t, carry):
        st = status_ref[base + t]
        off = pl.multiple_of(t * tk, tk)

        @pl.when(st == 0)
        def _():
            keys_s[:, pl.ds(off, tk)] = jnp.full((tq, tk), -jnp.inf, F32)

        def scores(masked):
            ikt = ikk_ref[pl.ds(off, tk), :]
            logits = _dot_nt(iqst_s[...], ikt)
            r = jnp.maximum(logits, 0.0) * _lane_tile(iwb_s[...], tk)
            sc = r[0:tq]
            for h in range(1, IDX_HEADS):
                sc = sc + r[h * tq:(h + 1) * tq]
            sc = jnp.where(sc == 0.0, 0.0, sc)
            if masked:
                ck = ck_ref[0, :, pl.ds(off, tk)]
                sc = jnp.where(cq >= ck, sc, -jnp.inf)
            keys_s[:, pl.ds(off, tk)] = sc
            fold_candidates(sc)

        @pl.when(st == 1)
        def _():
            scores(False)

        @pl.when(st == 2)
        def _():
            scores(True)

        return carry

    lax.fori_loop(0, n_hi, score_body, 0)

    rg = min(tq, LANES)
    lane_rg = lax.broadcasted_iota(I32, (rg, LANES), 1)

    def count_tiles(preds, *operands):
        zero = jnp.zeros((rg, LANES), I32)
        accs = []
        for r in range(tq // rg):
            rows = slice(r * rg, (r + 1) * rg)
            ops = [o[rows] for o in operands]

            def body(t, acc, rows=rows, ops=ops):
                off = pl.multiple_of(t * tk, tk)
                acc = list(acc)
                for c in range(tk // LANES):
                    slab = keys_s[rows, pl.ds(off + c * LANES, LANES)]
                    for n, pred in enumerate(preds):
                        acc[n] = acc[n] + jnp.where(pred(slab, off + c * LANES, *ops), 1, 0).astype(I32)
                return tuple(acc)
            accs.append(lax.fori_loop(0, n_hi, body, (zero,) * len(preds)))
        outs = []
        for n in range(len(preds)):
            acc = accs[0][n] if len(accs) == 1 else jnp.concatenate([a[n] for a in accs], axis=0)
            outs.append(jnp.broadcast_to(jnp.sum(acc, axis=1, keepdims=True), (tq, LANES)))
        return outs

    ge = lambda slab, col0, c: slab >= c
    gt = lambda slab, col0, c: slab > c
    eq = lambda slab, col0, c: slab == c

    def count_cand(rows, c):
        n_groups = TOP_PER_LANE // CAND_GROUP
        full = [cand_s[CAND_GROUP * i + CAND_GROUP - 1, rows, :] >= c for i in range(n_groups)]
        base = jnp.full(c.shape, TOP_PER_LANE, I32)
        rest = [jnp.full(c.shape, -jnp.inf, F32)] * (CAND_GROUP - 1)
        for i in reversed(range(n_groups)):
            base = jnp.where(full[i], base, CAND_GROUP * i)
            rest = [jnp.where(full[i], rest[k], cand_s[CAND_GROUP * i + k, rows, :])
                    for k in range(CAND_GROUP - 1)]
        for x in rest:
            base = base + jnp.where(x >= c, 1, 0).astype(I32)
        return base

    def bisect(count):
        def bit_body(j, thr):
            cand = thr + lax.shift_left(jnp.int32(1), 31 - j)
            cnt, = count([ge], _key_to_f32(cand))
            return jnp.where(cnt >= top_k, cand, thr)
        return lax.fori_loop(0, 32, bit_body, jnp.full((tq, LANES), INT_MIN, I32))

    def bisect_candidates():
        half = tq // 2
        halves = (slice(0, half), slice(half, tq))

        def raw_count(rows, key):
            return count_cand(rows, _key_to_f32(key))

        def decide(acc, cand, thr):
            cnt = jnp.broadcast_to(jnp.sum(acc, axis=1, keepdims=True), (half, LANES))
            return jnp.where(cnt >= top_k, cand, thr)

        bit = lambda j: lax.shift_left(jnp.int32(1), 31 - j)
        start = jnp.full((half, LANES), INT_MIN, I32)

        def body(j, carry):
            thr_a, cand_a, acc_a, thr_b = carry
            cand_b = thr_b + bit(j)
            acc_b = raw_count(halves[1], cand_b)
            thr_a = decide(acc_a, cand_a, thr_a)
            cand_a = thr_a + bit(j + 1)
            acc_a = raw_count(halves[0], cand_a)
            thr_b = decide(acc_b, cand_b, thr_b)
            return thr_a, cand_a, acc_a, thr_b

        cand_a = start + bit(0)
        carry = (start, cand_a, raw_count(halves[0], cand_a), start)
        thr_a, cand_a, acc_a, thr_b = lax.fori_loop(0, 31, body, carry)
        thr_a = decide(acc_a, cand_a, thr_a)
        cand_b = thr_b + bit(31)
        thr_b = decide(raw_count(halves[1], cand_b), cand_b, thr_b)
        return jnp.concatenate([thr_a, thr_b], axis=0)

    thr = bisect_candidates()
    floor = _key_to_f32(jnp.maximum(thr, MASKED_KEY + 1))
    n_cand = jnp.broadcast_to(
        jnp.sum(count_cand(slice(0, tq), floor), axis=1, keepdims=True), (tq, LANES))
    n_all, = count_tiles([ge], floor)
    thr_s[...] = thr

    @pl.when(jnp.max((n_all != n_cand).astype(I32)) > 0)
    def _():
        thr_s[...] = bisect(count_tiles)

    thr_key = thr_s[...]
    real = thr_key > MASKED_KEY
    thr = jnp.where(real, _key_to_f32(thr_key), -jnp.inf)
    thf_s[...] = thr
    cnt_gt, cnt_eq = count_tiles([gt, eq], thr)
    need = top_k - cnt_gt
    excess = jnp.logical_and(real, cnt_eq > need)
    jst_s[...] = jnp.where(real, jnp.int32(s_len), jnp.int32(-1))

    @pl.when(jnp.max(excess.astype(I32)) > 0)
    def _():
        def idx_body(j, jcur):
            cand = jcur + lax.shift_left(jnp.int32(1), (s_len.bit_length() - 2) - j)
            cnt, = count_tiles(
                [lambda slab, col0, th, c: jnp.logical_and(slab == th, (col0 + lane_rg) < c)], thr, cand)
            return jnp.where(cnt < need, cand, jcur)
        jbest = lax.fori_loop(0, s_len.bit_length() - 1, idx_body, jnp.zeros((tq, LANES), I32))
        jst_s[...] = jnp.where(excess, jbest, jst_s[...])

    m_s[...] = jnp.full(m_s.shape, NEG, F32)
    acc_s[...] = jnp.zeros(acc_s.shape, F32)
    lane_k = lax.broadcasted_iota(I32, (tq, tk), 1)
    lane_v = lax.broadcasted_iota(I32, (tk, LANES), 1)

    def attn_body(t, carry):
        st = status_ref[base + t]

        @pl.when(st != 0)
        def _():
            off = pl.multiple_of(t * tk, tk)
            key = keys_s[:, pl.ds(off, tk)]
            th = _lane_tile(thf_s[...], tk)
            jst = _lane_tile(jst_s[...], tk)
            sel = jnp.logical_or(key > th, jnp.logical_and(key == th, (lane_k + off) <= jst))
            bias = jnp.where(sel, 0.0, NEG)
            kt = kk_ref[pl.ds(off, tk), :]
            vt = vv_ref[pl.ds(off, tk), :]
            vt = jnp.where(lane_v < HEAD_DIM, vt, jnp.ones_like(vt))
            s = _dot_nt(qst_s[...], kt)
            s = (s.reshape(DSA_HEADS, tq, tk) + bias[None]).reshape(DSA_HEADS * tq, tk)
            m_old = m_s[...]
            m_new = jnp.maximum(m_old, jnp.max(s, axis=1, keepdims=True))
            p = jnp.exp2(s - _lane_tile(m_new, tk))
            alpha = jnp.exp2(m_old - m_new)
            acc_s[...] = alpha * acc_s[...] + _dot(p.astype(BF16), vt)
            m_s[...] = m_new

        return carry

    lax.fori_loop(0, n_hi, attn_body, 0)

    for j in range(DSA_HEADS // 2):
        acc_a = acc_s[2 * j * tq:(2 * j + 1) * tq]
        acc_b = acc_s[(2 * j + 1) * tq:(2 * j + 2) * tq]
        swap_a = pltpu.roll(acc_a, HEAD_DIM, 1)
        swap_b = pltpu.roll(acc_b, HEAD_DIM, 1)
        o_ref[:, j * LANES:(j + 1) * LANES] = jnp.where(first, acc_a / swap_a,
                                                        swap_b / acc_b).astype(o_ref.dtype)


def _dsa_attention(q, iq, iw, kk, vv, ikk, cq3, ck3, status, nhi, b, s, tq, tk, top_k):
    nq, nk = s // tq, s // tk
    kern = functools.partial(_dsa_kernel, tq=tq, tk=tk, nq=nq, nk=nk, s_len=s, top_k=top_k)
    rowblk = lambda bb, i, *_: (bb * nq + i, 0)
    perb = lambda bb, i, *_: (bb, 0)
    grid_spec = pltpu.PrefetchScalarGridSpec(
        num_scalar_prefetch=2,
        grid=(b, nq),
        in_specs=[
            pl.BlockSpec((tq, SEQ_WIDTH), rowblk),
            pl.BlockSpec((tq, IDX_HEADS * IDX_DIM), rowblk),
            pl.BlockSpec((tq, LANES), rowblk),
            pl.BlockSpec((s, LANES), perb),
            pl.BlockSpec((s, LANES), perb),
            pl.BlockSpec((s, LANES), perb),
            pl.BlockSpec((1, tq, 1), lambda bb, i, *_: (bb * nq + i, 0, 0)),
            pl.BlockSpec((1, 1, s), lambda bb, i, *_: (bb, 0, 0)),
        ],
        out_specs=pl.BlockSpec((tq, SEQ_WIDTH), rowblk),
        scratch_shapes=[
            pltpu.VMEM((tq, s), F32),
            pltpu.VMEM((TOP_PER_LANE, tq, LANES), F32),
            pltpu.VMEM((DSA_HEADS * tq, LANES), BF16),
            pltpu.VMEM((IDX_HEADS * tq, LANES), BF16),
            pltpu.VMEM((IDX_HEADS * tq, LANES), F32),
            pltpu.VMEM((DSA_HEADS * tq, LANES), F32),
            pltpu.VMEM((DSA_HEADS * tq, LANES), F32),
            pltpu.VMEM((tq, LANES), I32),
            pltpu.VMEM((tq, LANES), F32),
            pltpu.VMEM((tq, LANES), I32),
        ],
    )
    return pl.pallas_call(
        kern, grid_spec=grid_spec,
        out_shape=jax.ShapeDtypeStruct((b * s, SEQ_WIDTH), BF16),
        compiler_params=_params(("parallel", "arbitrary")), name="dsa_attn",
    )(status, nhi, q, iq, iw, kk, vv, ikk, cq3, ck3)


def _mix_mlp_kernel(x_ref, oseq_ref, qm_ref, km_ref, vm_ref, wo_ref, gmix_ref, gpre_ref, wup_ref, wdn_ref,
                    gpost_ref, o_ref, h_s, acc_s):
    j = pl.program_id(1)

    @pl.when(j == 0)
    def _():
        qm = qm_ref[...]
        km = km_ref[...]
        vm = vm_ref[...]
        lane = lax.broadcasted_iota(I32, qm.shape, 1)
        o_mem = jnp.zeros(qm.shape, F32)
        for h in range(MEM_HEADS):
            mine = (lane // HEAD_DIM) == h
            s = _dot_nt(jnp.where(mine, qm, jnp.zeros_like(qm)), km)
            p = jnp.exp(s - jnp.max(s, axis=1, keepdims=True))
            p = p / jnp.sum(p, axis=1, keepdims=True)
            o_mem = o_mem + jnp.where(mine, _dot(p.astype(BF16), vm), 0.0)
        y = (_dot(oseq_ref[...], wo_ref[0:SEQ_WIDTH, :])
             + _dot(o_mem.astype(BF16), wo_ref[SEQ_WIDTH:, :]))
        x_mid = x_ref[...] + _rms(y, gmix_ref[...])
        o_ref[...] = x_mid
        h_s[...] = _rms(x_mid, gpre_ref[...]).astype(BF16)
        acc_s[...] = jnp.zeros(acc_s.shape, F32)

    u = jnp.maximum(_dot(h_s[...], wup_ref[...]), 0.0)
    acc_s[...] += _dot((u * u).astype(BF16), wdn_ref[...])

    @pl.when(j == pl.num_programs(1) - 1)
    def _():
        o_ref[...] = o_ref[...] + _rms(acc_s[...], gpost_ref[...])


def _mix_mlp(x2d, oseq, qm, kv, w_out, g_mix, g_pre, w_up, w_dn, g_post, s, n_mem, tm, tf):
    m = x2d.shape[0]
    nt = s // tm
    row = lambda i, j: (i, 0)
    fixed = lambda i, j: (0, 0)
    vec = lambda g: g.reshape(1, D_MODEL)
    return pl.pallas_call(
        _mix_mlp_kernel, grid=(m // tm, D_FF // tf),
        in_specs=[
            pl.BlockSpec((tm, D_MODEL), row),
            pl.BlockSpec((tm, SEQ_WIDTH), row),
            pl.BlockSpec((tm, MEM_WIDTH), row),
            pl.BlockSpec((n_mem, MEM_WIDTH), lambda i, j: (i // nt, 0)),
            pl.BlockSpec((n_mem, MEM_WIDTH), lambda i, j: (i // nt, 1)),
            pl.BlockSpec((D_MODEL, D_MODEL), fixed),
            pl.BlockSpec((1, D_MODEL), fixed),
            pl.BlockSpec((1, D_MODEL), fixed),
            pl.BlockSpec((D_MODEL, tf), lambda i, j: (0, j)),
            pl.BlockSpec((tf, D_MODEL), lambda i, j: (j, 0)),
            pl.BlockSpec((1, D_MODEL), fixed),
        ],
        out_specs=pl.BlockSpec((tm, D_MODEL), row),
        out_shape=jax.ShapeDtypeStruct(x2d.shape, F32),
        scratch_shapes=[pltpu.VMEM((tm, D_MODEL), BF16), pltpu.VMEM((tm, D_MODEL), F32)],
        compiler_params=_params(("parallel", "arbitrary")), name="mix_mlp",
    )(x2d, oseq, qm, kv, kv, w_out, vec(g_mix), vec(g_pre), w_up, w_dn, vec(g_post))


def _rope_tables(positions):
    half = HEAD_DIM // 2
    inv = ROPE_THETA ** (-jnp.arange(0, HEAD_DIM, 2, dtype=F32) / HEAD_DIM)
    ang = positions.astype(F32).reshape(-1, 1) * inv
    cos, sin = jnp.cos(ang), jnp.sin(ang)
    reps = LANES // HEAD_DIM
    cos_t = jnp.tile(cos, (1, 2 * reps))
    sin_t = jnp.tile(jnp.concatenate([-sin, sin], axis=1), (1, reps))
    assert cos_t.shape[1] == LANES and half * 2 == HEAD_DIM
    return cos_t, sin_t


def _visibility(chunk_id, tq, tk):
    b, s = chunk_id.shape
    cq = chunk_id.reshape(b, s // tq, tq)
    ck = chunk_id.reshape(b, s // tk, tk)
    qmin, qmax = cq.min(-1)[:, :, None], cq.max(-1)[:, :, None]
    kmin, kmax = ck.min(-1)[:, None, :], ck.max(-1)[:, None, :]
    status = jnp.where(kmin > qmax, 0, jnp.where(kmax <= qmin, 1, 2)).astype(I32)
    tiles = jnp.arange(s // tk, dtype=I32)[None, None, :]
    nhi = jnp.max(jnp.where(status != 0, tiles + 1, 0), axis=-1).astype(I32)
    return status.reshape(-1), nhi.reshape(-1)


def _dup(w):
    return jnp.concatenate([w, w], axis=1)


def kernel(x, mem, positions, g_pre_mix, g_post_mix, g_mem, w_mem_kv, w_out, g_pre_mlp, g_post_mlp,
           w_mlp_up, w_mlp_down, w_in_diff, lambda_q1, lambda_k1, lambda_q2, lambda_k2, g_diff_subln,
           w_in_dsa):
    b, s, d = x.shape
    n_mem = mem.shape[1]
    depth = g_pre_mix.shape[0]
    m = b * s
    scale = HEAD_DIM ** -0.5
    scale2 = scale * math.log2(math.e)
    tm_proj = min(512, s)
    tq_diff, tk_diff = min(512, s), min(1024, s)
    tq_dsa, tk_dsa = min(256, s), min(512, s)
    tm_mlp, tf_mlp = min(1024, s), 1024
    top_k = min(DSA_TOPK_MAX, s // 4)
    assert tk_dsa >= top_k, "the threshold search needs one key tile to hold top_k candidates"

    cos_t, sin_t = _rope_tables(positions)
    chunk_id = positions // CHUNK
    cq3 = chunk_id.reshape(-1, 1)
    ck3 = chunk_id.reshape(b, 1, s)
    vis_diff = _visibility(chunk_id, tq_diff, tk_diff)
    vis_dsa = _visibility(chunk_id, tq_dsa, tk_dsa)

    x2d = x.reshape(m, d)
    mem2d = mem.reshape(b * n_mem, d)
    for i in range(depth):
        j = i // 2
        kv = _inproj(mem2d, g_mem[i], None, None, None, w_mem_kv[i].astype(BF16), (),
                     ((2 * MEM_WIDTH, 1.0, BF16),), min(256, b * n_mem))[0]
        if i % 2 == 0:
            w = w_in_diff[j].astype(BF16)
            nqk = 2 * DIFF_HEADS * HEAD_DIM
            q, k, v, qm = _inproj(
                x2d, g_pre_mix[i], cos_t, sin_t, w[:, :2 * nqk], w[:, 2 * nqk:],
                ((nqk, scale2, BF16), (nqk, 1.0, BF16)),
                ((SEQ_WIDTH, 1.0, BF16), (MEM_WIDTH, scale, BF16)), tm_proj)
            lam_p = jnp.stack([lambda_q1[j], lambda_k1[j], lambda_q2[j], lambda_k2[j]]).astype(F32)
            lam_init = 0.8 - 0.6 * math.exp(-0.3 * i)
            o_seq = _diff_attention(
                q, k, v, cq3.reshape(m // tq_diff, tq_diff, 1), ck3, vis_diff[0], vis_diff[1],
                lam_p, g_diff_subln[j].reshape(1, LANES), b, s, tq_diff, tk_diff, lam_init)
        else:
            w = w_in_dsa[j]
            o0 = 0
            parts = []
            for width in (DSA_HEADS * HEAD_DIM, HEAD_DIM, HEAD_DIM, IDX_HEADS * IDX_DIM, IDX_DIM,
                          IDX_HEADS, MEM_WIDTH):
                parts.append(w[:, o0:o0 + width])
                o0 += width
            wq, wk, wv, wiq, wik, wiw, wqm = parts
            wiw = jnp.pad(wiw, ((0, 0), (0, LANES - IDX_HEADS)))
            w_rope = jnp.concatenate([wq, wiq, _dup(wk), _dup(wik)], axis=1).astype(BF16)
            w_plain = jnp.concatenate([_dup(wv), wiw, wqm], axis=1).astype(BF16)
            iw_scale = IDX_HEADS ** -0.5 * IDX_DIM ** -0.5
            q, iq, kk, ikk, vv, iw, qm = _inproj(
                x2d, g_pre_mix[i], cos_t, sin_t, w_rope, w_plain,
                ((DSA_HEADS * HEAD_DIM, scale2, BF16), (IDX_HEADS * IDX_DIM, 1.0, BF16),
                 (LANES, 1.0, BF16), (LANES, 1.0, BF16)),
                ((LANES, 1.0, BF16), (LANES, iw_scale, F32), (MEM_WIDTH, scale, BF16)), tm_proj)
            o_seq = _dsa_attention(
                q, iq, iw, kk, vv, ikk, cq3.reshape(m // tq_dsa, tq_dsa, 1), ck3,
                vis_dsa[0], vis_dsa[1], b, s, tq_dsa, tk_dsa, top_k)
        x2d = _mix_mlp(x2d, o_seq, qm, kv, w_out[i].astype(BF16), g_post_mix[i], g_pre_mlp[i],
                       w_mlp_up[i].astype(BF16), w_mlp_down[i].astype(BF16), g_post_mlp[i],
                       s, n_mem, tm_mlp, tf_mlp)
    return x2d.reshape(b, s, d)
```

```python
import functools
import math

import jax
import jax.numpy as jnp
from jax import lax
from jax.experimental import pallas as pl
from jax.experimental.pallas import tpu as pltpu

D_MODEL = 1024
CHUNK = 64
HEAD_DIM = 64
LANES = 128
ROPE_THETA = 10000.0
EPS = 1e-6
MEM_HEADS = 4
MEM_WIDTH = MEM_HEADS * HEAD_DIM
SEQ_WIDTH = D_MODEL - MEM_WIDTH
DIFF_HEADS = SEQ_WIDTH // (2 * HEAD_DIM)
DSA_HEADS = SEQ_WIDTH // HEAD_DIM
IDX_HEADS = 8
IDX_DIM = 64
DSA_TOPK_MAX = 256
D_FF = 4 * D_MODEL

NEG = -1e30
INT_MIN = -(2 ** 31)
MASKED_KEY = -2139095041
TOP_PER_LANE = 12
CAND_ROWS = 32
CAND_GROUP = 4
assert TOP_PER_LANE % CAND_GROUP == 0
VMEM_LIMIT = 56 * 1024 * 1024

F32 = jnp.float32
BF16 = jnp.bfloat16
I32 = jnp.int32


def _dot(a, b):
    return jnp.dot(a, b, preferred_element_type=F32)


def _dot_nt(a, b):
    return lax.dot_general(a, b, (((1,), (1,)), ((), ())), preferred_element_type=F32)


def _lane_tile(x, width):
    return jnp.concatenate([x] * (width // LANES), axis=1)


def _key_to_f32(key):
    return lax.bitcast_convert_type(jnp.where(key < 0, key ^ 0x7FFFFFFF, key), F32)


def _rms(x, g):
    return x * lax.rsqrt(jnp.mean(x * x, axis=-1, keepdims=True) + EPS) * g


def _params(sem):
    return pltpu.CompilerParams(dimension_semantics=sem, vmem_limit_bytes=VMEM_LIMIT)


def _inproj_kernel(*refs, rope_outs, plain_outs):
    n_r, n_p = len(rope_outs), len(plain_outs)
    x_ref, g_ref = refs[0], refs[1]
    pos = 2
    if n_r:
        cos_ref, sin_ref, wr_ref = refs[2], refs[3], refs[4]
        pos = 5
    if n_p:
        wp_ref = refs[pos]
        pos += 1
    out_refs = refs[pos:]
    x = x_ref[...]
    h = _rms(x, g_ref[...]).astype(BF16)
    oi = 0
    if n_r:
        cos = cos_ref[...]
        sin = sin_ref[...]
        lane = lax.broadcasted_iota(I32, cos.shape, 1)
        low = (lane % HEAD_DIM) < (HEAD_DIM // 2)
        c0 = 0
        for width, scale in rope_outs:
            o_ref = out_refs[oi]
            oi += 1
            y_all = _dot(h, wr_ref[:, c0:c0 + width])
            for s in range(width // LANES):
                y = y_all[:, s * LANES:(s + 1) * LANES]
                swapped = jnp.where(low, pltpu.roll(y, LANES - HEAD_DIM // 2, 1),
                                    pltpu.roll(y, HEAD_DIM // 2, 1))
                r = y * cos + swapped * sin
                if scale != 1.0:
                    r = r * scale
                o_ref[:, s * LANES:(s + 1) * LANES] = r.astype(o_ref.dtype)
            c0 += width
    c0 = 0
    for width, scale in plain_outs:
        o_ref = out_refs[oi]
        oi += 1
        y = _dot(h, wp_ref[:, c0:c0 + width])
        if scale != 1.0:
            y = y * scale
        o_ref[...] = y.astype(o_ref.dtype)
        c0 += width


def _inproj(x2d, g, cos, sin, w_rope, w_plain, rope_outs, plain_outs, tm):
    m = x2d.shape[0]
    row = lambda i: (i, 0)
    fixed = lambda i: (0, 0)
    in_specs = [pl.BlockSpec((tm, D_MODEL), row), pl.BlockSpec((1, D_MODEL), fixed)]
    args = [x2d, g.reshape(1, D_MODEL)]
    if rope_outs:
        in_specs += [pl.BlockSpec((tm, LANES), row), pl.BlockSpec((tm, LANES), row),
                     pl.BlockSpec(w_rope.shape, fixed)]
        args += [cos, sin, w_rope]
    if plain_outs:
        in_specs += [pl.BlockSpec(w_plain.shape, fixed)]
        args += [w_plain]
    outs = tuple(rope_outs) + tuple(plain_outs)
    out_shape = [jax.ShapeDtypeStruct((m, w), dt) for w, _, dt in outs]
    out_specs = [pl.BlockSpec((tm, w), row) for w, _, _ in outs]
    kern = functools.partial(_inproj_kernel,
                             rope_outs=tuple((w, s) for w, s, _ in rope_outs),
                             plain_outs=tuple((w, s) for w, s, _ in plain_outs))
    return pl.pallas_call(
        kern, grid=(m // tm,), in_specs=in_specs, out_specs=out_specs, out_shape=out_shape,
        compiler_params=_params(("parallel",)), name="inproj")(*args)


def _diff_attn_kernel(status_ref, nhi_ref, q_ref, k_ref, v_ref, cq_ref, ck_ref, lam_ref, gsub_ref,
                      o_ref, m_s, acc_s, s0_s, s1_s, *, tq, tk, nq, nk, lam_init):
    b = pl.program_id(0)
    i = pl.program_id(2)
    q = q_ref[...]
    lane = lax.broadcasted_iota(I32, q.shape, 1)
    zero = jnp.zeros_like(q)
    qs = (jnp.where(lane < HEAD_DIM, q, zero), jnp.where(lane >= HEAD_DIM, q, zero))
    cq = cq_ref[0]
    m_s[...] = jnp.full(m_s.shape, NEG, F32)
    acc_s[...] = jnp.zeros(acc_s.shape, F32)
    base = (b * nq + i) * nk
    n_hi = nhi_ref[b * nq + i]

    def logits(t, dst_s):
        kt = k_ref[pl.ds(pl.multiple_of(t * tk, tk), tk), :]
        for j in range(2):
            dst_s[j] = _dot_nt(qs[j], kt)

    def tile(t, masked, cur_s, nxt_s):
        if nxt_s is not None:
            logits(t + 1, nxt_s)
        off = pl.multiple_of(t * tk, tk)
        vt = jnp.concatenate([v_ref[pl.ds(off, tk), :], jnp.ones((tk, LANES), BF16)], axis=1)
        if masked:
            ck = ck_ref[0, :, pl.ds(off, tk)]
            bias = jnp.where(cq >= ck, 0.0, NEG)
        for j in range(2):
            s = cur_s[j] + bias if masked else cur_s[j]
            m_old = m_s[j]
            m_new = jnp.maximum(m_old, jnp.max(s, axis=1, keepdims=True))
            p = jnp.exp2(s - _lane_tile(m_new, tk))
            alpha = jnp.exp2(m_old - m_new)
            acc_s[j] = _lane_tile(alpha, 2 * LANES) * acc_s[j] + _dot(p.astype(BF16), vt)
            m_s[j] = m_new

    def step(t, cur_s, nxt_s):
        st = status_ref[base + jnp.minimum(t, nk - 1)]
        live = t < n_hi
        more = t + 1 < n_hi

        @pl.when(jnp.logical_and(st == 0, more))
        def _():
            logits(t + 1, nxt_s)

        for code, masked in ((1, False), (2, True)):
            hit = jnp.logical_and(live, st == code)

            @pl.when(jnp.logical_and(hit, more))
            def _():
                tile(t, masked, cur_s, nxt_s)

            @pl.when(jnp.logical_and(hit, jnp.logical_not(more)))
            def _():
                tile(t, masked, cur_s, None)

    logits(0, s0_s)

    def body(u, carry):
        step(2 * u, s0_s, s1_s)
        step(2 * u + 1, s1_s, s0_s)
        return carry

    lax.fori_loop(0, (n_hi + 1) // 2, body, 0)

    lam_p = lam_ref[...]
    lam = (jnp.exp(jnp.sum(lam_p[0:1] * lam_p[1:2], axis=1, keepdims=True))
           - jnp.exp(jnp.sum(lam_p[2:3] * lam_p[3:4], axis=1, keepdims=True)) + lam_init)
    o = (acc_s[0, :, :LANES] / acc_s[0, :, LANES:]
         - lam * (acc_s[1, :, :LANES] / acc_s[1, :, LANES:]))
    o = _rms(o, gsub_ref[...]) * (1.0 - lam_init)
    o_ref[...] = o.astype(o_ref.dtype)


def _diff_attention(q, k, v, cq3, ck3, status, nhi, lam_p, gsub, b, s, tq, tk, lam_init):
    nq, nk = s // tq, s // tk
    kern = functools.partial(_diff_attn_kernel, tq=tq, tk=tk, nq=nq, nk=nk, lam_init=lam_init)
    grid_spec = pltpu.PrefetchScalarGridSpec(
        num_scalar_prefetch=2,
        grid=(b, DIFF_HEADS, nq),
        in_specs=[
            pl.BlockSpec((tq, LANES), lambda bb, h, i, *_: (bb * nq + i, h)),
            pl.BlockSpec((s, LANES), lambda bb, h, i, *_: (bb, h)),
            pl.BlockSpec((s, LANES), lambda bb, h, i, *_: (bb, h)),
            pl.BlockSpec((1, tq, 1), lambda bb, h, i, *_: (bb * nq + i, 0, 0)),
            pl.BlockSpec((1, 1, s), lambda bb, h, i, *_: (bb, 0, 0)),
            pl.BlockSpec((4, HEAD_DIM), lambda bb, h, i, *_: (0, 0)),
            pl.BlockSpec((1, LANES), lambda bb, h, i, *_: (0, 0)),
        ],
        out_specs=pl.BlockSpec((tq, LANES), lambda bb, h, i, *_: (bb * nq + i, h)),
        scratch_shapes=[pltpu.VMEM((2, tq, LANES), F32),
                        pltpu.VMEM((2, tq, 2 * LANES), F32),
                        pltpu.VMEM((2, tq, tk), F32),
                        pltpu.VMEM((2, tq, tk), F32)],
    )
    return pl.pallas_call(
        kern, grid_spec=grid_spec,
        out_shape=jax.ShapeDtypeStruct((b * s, SEQ_WIDTH), BF16),
        compiler_params=_params(("parallel", "parallel", "arbitrary")), name="diff_attn",
    )(status, nhi, q, k, v, cq3, ck3, lam_p, gsub)


def _dsa_kernel(status_ref, nhi_ref, q_ref, iq_ref, iw_ref, kk_ref, vv_ref, ikk_ref, cq_ref, ck_ref,
                o_ref, keys_s, cand_s, qst_s, iqst_s, iwb_s, m_s, acc_s, thr_s, thf_s, jst_s,
                *, tq, tk, nq, nk, s_len, top_k):
    b = pl.program_id(0)
    i = pl.program_id(1)
    base = (b * nq + i) * nk
    n_hi = nhi_ref[b * nq + i]
    cq = cq_ref[0]
    lane = lax.broadcasted_iota(I32, (tq, LANES), 1)
    first = lane < HEAD_DIM
    spt = tk // LANES

    for h in range(DSA_HEADS):
        slab = q_ref[:, (h // 2) * LANES:(h // 2 + 1) * LANES]
        keep = first if h % 2 == 0 else jnp.logical_not(first)
        qst_s[h * tq:(h + 1) * tq, :] = jnp.where(keep, slab, jnp.zeros_like(slab))
    for h in range(IDX_HEADS):
        slab = iq_ref[:, (h // 2) * LANES:(h // 2 + 1) * LANES]
        keep = first if h % 2 == 0 else jnp.logical_not(first)
        iqst_s[h * tq:(h + 1) * tq, :] = jnp.where(keep, slab, jnp.zeros_like(slab))
        iwb_s[h * tq:(h + 1) * tq, :] = jnp.broadcast_to(iw_ref[:, h:h + 1], (tq, LANES))

    cand_s[...] = jnp.full(cand_s.shape, -jnp.inf, F32)

    def fold_candidates(sc):
        for g in range(tq // CAND_ROWS):
            rows = slice(g * CAND_ROWS, (g + 1) * CAND_ROWS)
            tops = [cand_s[r, rows, :] for r in range(TOP_PER_LANE)]
            for c in range(tk // LANES):
                x = sc[rows, c * LANES:(c + 1) * LANES]
                for r in range(TOP_PER_LANE):
                    tops[r], x = jnp.maximum(tops[r], x), jnp.minimum(tops[r], x)
            for r in range(TOP_PER_LANE):
                cand_s[r, rows, :] = tops[r]

    def score_body(t, carry):
        st = status_ref[base + t]
        off = pl.multiple_of(t * tk, tk)

        @pl.when(st == 0)
        def _():
            for c in range(spt):
                keys_s[t * spt + c] = jnp.full((tq, LANES), -jnp.inf, F32)

        def scores(masked):
            ikt = ikk_ref[pl.ds(off, tk), :]
            logits = _dot_nt(iqst_s[...], ikt)
            r = jnp.maximum(logits, 0.0) * _lane_tile(iwb_s[...], tk)
            sc = r[0:tq]
            for h in range(1, IDX_HEADS):
                sc = sc + r[h * tq:(h + 1) * tq]
            sc = jnp.where(sc == 0.0, 0.0, sc)
            if masked:
                ck = ck_ref[0, :, pl.ds(off, tk)]
                sc = jnp.where(cq >= ck, sc, -jnp.inf)
            for c in range(spt):
                keys_s[t * spt + c] = sc[:, c * LANES:(c + 1) * LANES]
            fold_candidates(sc)

        @pl.when(st == 1)
        def _():
            scores(False)

        @pl.when(st == 2)
        def _():
            scores(True)

        return carry

    lax.fori_loop(0, n_hi, score_body, 0)

    rg = min(tq, LANES)
    lane_rg = lax.broadcasted_iota(I32, (rg, LANES), 1)

    def count_tiles(preds, *operands):
        zero = jnp.zeros((rg, LANES), I32)
        accs = []
        for r in range(tq // rg):
            rows = slice(r * rg, (r + 1) * rg)
            ops = [o[rows] for o in operands]

            def body(t, acc, rows=rows, ops=ops):
                off = pl.multiple_of(t * tk, tk)
                acc = list(acc)
                for c in range(spt):
                    slab = keys_s[t * spt + c, rows, :]
                    for n, pred in enumerate(preds):
                        acc[n] = acc[n] + jnp.where(pred(slab, off + c * LANES, *ops), 1, 0).astype(I32)
                return tuple(acc)
            accs.append(lax.fori_loop(0, n_hi, body, (zero,) * len(preds)))
        outs = []
        for n in range(len(preds)):
            acc = accs[0][n] if len(accs) == 1 else jnp.concatenate([a[n] for a in accs], axis=0)
            outs.append(jnp.broadcast_to(jnp.sum(acc, axis=1, keepdims=True), (tq, LANES)))
        return outs

    ge = lambda slab, col0, c: slab >= c
    gt = lambda slab, col0, c: slab > c
    eq = lambda slab, col0, c: slab == c

    def count_cand(rows, c):
        n_groups = TOP_PER_LANE // CAND_GROUP
        full = [cand_s[CAND_GROUP * i + CAND_GROUP - 1, rows, :] >= c for i in range(n_groups)]
        base = jnp.full(c.shape, TOP_PER_LANE, I32)
        rest = [jnp.full(c.shape, -jnp.inf, F32)] * (CAND_GROUP - 1)
        for i in reversed(range(n_groups)):
            base = jnp.where(full[i], base, CAND_GROUP * i)
            rest = [jnp.where(full[i], rest[k], cand_s[CAND_GROUP * i + k, rows, :])
                    for k in range(CAND_GROUP - 1)]
        for x in rest:
            base = base + jnp.where(x >= c, 1, 0).astype(I32)
        return base

    def bisect(count):
        def bit_body(j, thr):
            cand = thr + lax.shift_left(jnp.int32(1), 31 - j)
            cnt, = count([ge], _key_to_f32(cand))
            return jnp.where(cnt >= top_k, cand, thr)
        return lax.fori_loop(0, 32, bit_body, jnp.full((tq, LANES), INT_MIN, I32))

    def bisect_candidates():
        half = tq // 2
        halves = (slice(0, half), slice(half, tq))

        def raw_count(rows, key):
            return count_cand(rows, _key_to_f32(key))

        def decide(acc, cand, thr):
            cnt = jnp.broadcast_to(jnp.sum(acc, axis=1, keepdims=True), (half, LANES))
            return jnp.where(cnt >= top_k, cand, thr)

        bit = lambda j: lax.shift_left(jnp.int32(1), 31 - j)
        start = jnp.full((half, LANES), INT_MIN, I32)

        def body(j, carry):
            thr_a, cand_a, acc_a, thr_b = carry
            cand_b = thr_b + bit(j)
            acc_b = raw_count(halves[1], cand_b)
            thr_a = decide(acc_a, cand_a, thr_a)
            cand_a = thr_a + bit(j + 1)
            acc_a = raw_count(halves[0], cand_a)
            thr_b = decide(acc_b, cand_b, thr_b)
            return thr_a, cand_a, acc_a, thr_b

        cand_a = start + bit(0)
        carry = (start, cand_a, raw_count(halves[0], cand_a), start)
        thr_a, cand_a, acc_a, thr_b = lax.fori_loop(0, 31, body, carry)
        thr_a = decide(acc_a, cand_a, thr_a)
        cand_b = thr_b + bit(31)
        thr_b = decide(raw_count(halves[1], cand_b), cand_b, thr_b)
        return jnp.concatenate([thr_a, thr_b], axis=0)

    thr = bisect_candidates()
    floor = _key_to_f32(jnp.maximum(thr, MASKED_KEY + 1))
    n_cand = jnp.broadcast_to(
        jnp.sum(count_cand(slice(0, tq), floor), axis=1, keepdims=True), (tq, LANES))
    n_all, = count_tiles([ge], floor)
    thr_s[...] = thr

    @pl.when(jnp.max((n_all != n_cand).astype(I32)) > 0)
    def _():
        thr_s[...] = bisect(count_tiles)

    thr_key = thr_s[...]
    real = thr_key > MASKED_KEY
    thr = jnp.where(real, _key_to_f32(thr_key), -jnp.inf)
    thf_s[...] = thr
    cnt_gt, cnt_eq = count_tiles([gt, eq], thr)
    need = top_k - cnt_gt
    excess = jnp.logical_and(real, cnt_eq > need)
    jst_s[...] = jnp.where(real, jnp.int32(s_len), jnp.int32(-1))

    @pl.when(jnp.max(excess.astype(I32)) > 0)
    def _():
        def idx_body(j, jcur):
            cand = jcur + lax.shift_left(jnp.int32(1), (s_len.bit_length() - 2) - j)
            cnt, = count_tiles(
                [lambda slab, col0, th, c: jnp.logical_and(slab == th, (col0 + lane_rg) < c)], thr, cand)
            return jnp.where(cnt < need, cand, jcur)
        jbest = lax.fori_loop(0, s_len.bit_length() - 1, idx_body, jnp.zeros((tq, LANES), I32))
        jst_s[...] = jnp.where(excess, jbest, jst_s[...])

    m_s[...] = jnp.full(m_s.shape, NEG, F32)
    acc_s[...] = jnp.zeros(acc_s.shape, F32)
    lane_k = lax.broadcasted_iota(I32, (tq, tk), 1)
    lane_v = lax.broadcasted_iota(I32, (tk, LANES), 1)

    def attn_body(t, carry):
        st = status_ref[base + t]

        @pl.when(st != 0)
        def _():
            off = pl.multiple_of(t * tk, tk)
            key = jnp.concatenate([keys_s[t * spt + c] for c in range(spt)], axis=1)
            th = _lane_tile(thf_s[...], tk)
            jst = _lane_tile(jst_s[...], tk)
            sel = jnp.logical_or(key > th, jnp.logical_and(key == th, (lane_k + off) <= jst))
            bias = jnp.where(sel, 0.0, NEG)
            kt = kk_ref[pl.ds(off, tk), :]
            vt = vv_ref[pl.ds(off, tk), :]
            vt = jnp.where(lane_v < HEAD_DIM, vt, jnp.ones_like(vt))
            s = _dot_nt(qst_s[...], kt)
            s = (s.reshape(DSA_HEADS, tq, tk) + bias[None]).reshape(DSA_HEADS * tq, tk)
            m_old = m_s[...]
            m_new = jnp.maximum(m_old, jnp.max(s, axis=1, keepdims=True))
            p = jnp.exp2(s - _lane_tile(m_new, tk))
            alpha = jnp.exp2(m_old - m_new)
            acc_s[...] = alpha * acc_s[...] + _dot(p.astype(BF16), vt)
            m_s[...] = m_new

        return carry

    lax.fori_loop(0, n_hi, attn_body, 0)

    for j in range(DSA_HEADS // 2):
        acc_a = acc_s[2 * j * tq:(2 * j + 1) * tq]
        acc_b = acc_s[(2 * j + 1) * tq:(2 * j + 2) * tq]
        swap_a = pltpu.roll(acc_a, HEAD_DIM, 1)
        swap_b = pltpu.roll(acc_b, HEAD_DIM, 1)
        o_ref[:, j * LANES:(j + 1) * LANES] = jnp.where(first, acc_a / swap_a,
                                                        swap_b / acc_b).astype(o_ref.dtype)


def _dsa_attention(q, iq, iw, kk, vv, ikk, cq3, ck3, status, nhi, b, s, tq, tk, top_k):
    nq, nk = s // tq, s // tk
    kern = functools.partial(_dsa_kernel, tq=tq, tk=tk, nq=nq, nk=nk, s_len=s, top_k=top_k)
    rowblk = lambda bb, i, *_: (bb * nq + i, 0)
    perb = lambda bb, i, *_: (bb, 0)
    grid_spec = pltpu.PrefetchScalarGridSpec(
        num_scalar_prefetch=2,
        grid=(b, nq),
        in_specs=[
            pl.BlockSpec((tq, SEQ_WIDTH), rowblk),
            pl.BlockSpec((tq, IDX_HEADS * IDX_DIM), rowblk),
            pl.BlockSpec((tq, LANES), rowblk),
            pl.BlockSpec((s, LANES), perb),
            pl.BlockSpec((s, LANES), perb),
            pl.BlockSpec((s, LANES), perb),
            pl.BlockSpec((1, tq, 1), lambda bb, i, *_: (bb * nq + i, 0, 0)),
            pl.BlockSpec((1, 1, s), lambda bb, i, *_: (bb, 0, 0)),
        ],
        out_specs=pl.BlockSpec((tq, SEQ_WIDTH), rowblk),
        scratch_shapes=[
            pltpu.VMEM((s // LANES, tq, LANES), F32),
            pltpu.VMEM((TOP_PER_LANE, tq, LANES), F32),
            pltpu.VMEM((DSA_HEADS * tq, LANES), BF16),
            pltpu.VMEM((IDX_HEADS * tq, LANES), BF16),
            pltpu.VMEM((IDX_HEADS * tq, LANES), F32),
            pltpu.VMEM((DSA_HEADS * tq, LANES), F32),
            pltpu.VMEM((DSA_HEADS * tq, LANES), F32),
            pltpu.VMEM((tq, LANES), I32),
            pltpu.VMEM((tq, LANES), F32),
            pltpu.VMEM((tq, LANES), I32),
        ],
    )
    return pl.pallas_call(
        kern, grid_spec=grid_spec,
        out_shape=jax.ShapeDtypeStruct((b * s, SEQ_WIDTH), BF16),
        compiler_params=_params(("parallel", "arbitrary")), name="dsa_attn",
    )(status, nhi, q, iq, iw, kk, vv, ikk, cq3, ck3)


def _mix_mlp_kernel(x_ref, oseq_ref, qm_ref, km_ref, vm_ref, wo_ref, gmix_ref, gpre_ref, wup_ref, wdn_ref,
                    gpost_ref, o_ref, h_s, acc_s):
    j = pl.program_id(1)

    @pl.when(j == 0)
    def _():
        qm = qm_ref[...]
        km = km_ref[...]
        vm = vm_ref[...]
        lane = lax.broadcasted_iota(I32, qm.shape, 1)
        o_mem = jnp.zeros(qm.shape, F32)
        for h in range(MEM_HEADS):
            mine = (lane // HEAD_DIM) == h
            s = _dot_nt(jnp.where(mine, qm, jnp.zeros_like(qm)), km)
            p = jnp.exp(s - jnp.max(s, axis=1, keepdims=True))
            p = p / jnp.sum(p, axis=1, keepdims=True)
            o_mem = o_mem + jnp.where(mine, _dot(p.astype(BF16), vm), 0.0)
        y = (_dot(oseq_ref[...], wo_ref[0:SEQ_WIDTH, :])
             + _dot(o_mem.astype(BF16), wo_ref[SEQ_WIDTH:, :]))
        x_mid = x_ref[...] + _rms(y, gmix_ref[...])
        o_ref[...] = x_mid
        h_s[...] = _rms(x_mid, gpre_ref[...]).astype(BF16)
        acc_s[...] = jnp.zeros(acc_s.shape, F32)

    u = jnp.maximum(_dot(h_s[...], wup_ref[...]), 0.0)
    acc_s[...] += _dot((u * u).astype(BF16), wdn_ref[...])

    @pl.when(j == pl.num_programs(1) - 1)
    def _():
        o_ref[...] = o_ref[...] + _rms(acc_s[...], gpost_ref[...])


def _mix_mlp(x2d, oseq, qm, kv, w_out, g_mix, g_pre, w_up, w_dn, g_post, s, n_mem, tm, tf):
    m = x2d.shape[0]
    nt = s // tm
    row = lambda i, j: (i, 0)
    fixed = lambda i, j: (0, 0)
    vec = lambda g: g.reshape(1, D_MODEL)
    return pl.pallas_call(
        _mix_mlp_kernel, grid=(m // tm, D_FF // tf),
        in_specs=[
            pl.BlockSpec((tm, D_MODEL), row),
            pl.BlockSpec((tm, SEQ_WIDTH), row),
            pl.BlockSpec((tm, MEM_WIDTH), row),
            pl.BlockSpec((n_mem, MEM_WIDTH), lambda i, j: (i // nt, 0)),
            pl.BlockSpec((n_mem, MEM_WIDTH), lambda i, j: (i // nt, 1)),
            pl.BlockSpec((D_MODEL, D_MODEL), fixed),
            pl.BlockSpec((1, D_MODEL), fixed),
            pl.BlockSpec((1, D_MODEL), fixed),
            pl.BlockSpec((D_MODEL, tf), lambda i, j: (0, j)),
            pl.BlockSpec((tf, D_MODEL), lambda i, j: (j, 0)),
            pl.BlockSpec((1, D_MODEL), fixed),
        ],
        out_specs=pl.BlockSpec((tm, D_MODEL), row),
        out_shape=jax.ShapeDtypeStruct(x2d.shape, F32),
        scratch_shapes=[pltpu.VMEM((tm, D_MODEL), BF16), pltpu.VMEM((tm, D_MODEL), F32)],
        compiler_params=_params(("parallel", "arbitrary")), name="mix_mlp",
    )(x2d, oseq, qm, kv, kv, w_out, vec(g_mix), vec(g_pre), w_up, w_dn, vec(g_post))


def _rope_tables(positions):
    half = HEAD_DIM // 2
    inv = ROPE_THETA ** (-jnp.arange(0, HEAD_DIM, 2, dtype=F32) / HEAD_DIM)
    ang = positions.astype(F32).reshape(-1, 1) * inv
    cos, sin = jnp.cos(ang), jnp.sin(ang)
    reps = LANES // HEAD_DIM
    cos_t = jnp.tile(cos, (1, 2 * reps))
    sin_t = jnp.tile(jnp.concatenate([-sin, sin], axis=1), (1, reps))
    assert cos_t.shape[1] == LANES and half * 2 == HEAD_DIM
    return cos_t, sin_t


def _visibility(chunk_id, tq, tk):
    b, s = chunk_id.shape
    cq = chunk_id.reshape(b, s // tq, tq)
    ck = chunk_id.reshape(b, s // tk, tk)
    qmin, qmax = cq.min(-1)[:, :, None], cq.max(-1)[:, :, None]
    kmin, kmax = ck.min(-1)[:, None, :], ck.max(-1)[:, None, :]
    status = jnp.where(kmin > qmax, 0, jnp.where(kmax <= qmin, 1, 2)).astype(I32)
    tiles = jnp.arange(s // tk, dtype=I32)[None, None, :]
    nhi = jnp.max(jnp.where(status != 0, tiles + 1, 0), axis=-1).astype(I32)
    return status.reshape(-1), nhi.reshape(-1)


def _dup(w):
    return jnp.concatenate([w, w], axis=1)


def kernel(x, mem, positions, g_pre_mix, g_post_mix, g_mem, w_mem_kv, w_out, g_pre_mlp, g_post_mlp,
           w_mlp_up, w_mlp_down, w_in_diff, lambda_q1, lambda_k1, lambda_q2, lambda_k2, g_diff_subln,
           w_in_dsa):
    b, s, d = x.shape
    n_mem = mem.shape[1]
    depth = g_pre_mix.shape[0]
    m = b * s
    scale = HEAD_DIM ** -0.5
    scale2 = scale * math.log2(math.e)
    tm_proj = min(512, s)
    tq_diff, tk_diff = min(512, s), min(1024, s)
    tq_dsa, tk_dsa = min(256, s), min(512, s)
    tm_mlp, tf_mlp = min(1024, s), 1024
    top_k = min(DSA_TOPK_MAX, s // 4)
    assert tk_dsa >= top_k, "the threshold search needs one key tile to hold top_k candidates"

    cos_t, sin_t = _rope_tables(positions)
    chunk_id = positions // CHUNK
    cq3 = chunk_id.reshape(-1, 1)
    ck3 = chunk_id.reshape(b, 1, s)
    vis_diff = _visibility(chunk_id, tq_diff, tk_diff)
    vis_dsa = _visibility(chunk_id, tq_dsa, tk_dsa)

    x2d = x.reshape(m, d)
    mem2d = mem.reshape(b * n_mem, d)
    for i in range(depth):
        j = i // 2
        kv = _inproj(mem2d, g_mem[i], None, None, None, w_mem_kv[i].astype(BF16), (),
                     ((2 * MEM_WIDTH, 1.0, BF16),), min(256, b * n_mem))[0]
        if i % 2 == 0:
            w = w_in_diff[j].astype(BF16)
            nqk = 2 * DIFF_HEADS * HEAD_DIM
            q, k, v, qm = _inproj(
                x2d, g_pre_mix[i], cos_t, sin_t, w[:, :2 * nqk], w[:, 2 * nqk:],
                ((nqk, scale2, BF16), (nqk, 1.0, BF16)),
                ((SEQ_WIDTH, 1.0, BF16), (MEM_WIDTH, scale, BF16)), tm_proj)
            lam_p = jnp.stack([lambda_q1[j], lambda_k1[j], lambda_q2[j], lambda_k2[j]]).astype(F32)
            lam_init = 0.8 - 0.6 * math.exp(-0.3 * i)
            o_seq = _diff_attention(
                q, k, v, cq3.reshape(m // tq_diff, tq_diff, 1), ck3, vis_diff[0], vis_diff[1],
                lam_p, g_diff_subln[j].reshape(1, LANES), b, s, tq_diff, tk_diff, lam_init)
        else:
            w = w_in_dsa[j]
            o0 = 0
            parts = []
            for width in (DSA_HEADS * HEAD_DIM, HEAD_DIM, HEAD_DIM, IDX_HEADS * IDX_DIM, IDX_DIM,
                          IDX_HEADS, MEM_WIDTH):
                parts.append(w[:, o0:o0 + width])
                o0 += width
            wq, wk, wv, wiq, wik, wiw, wqm = parts
            wiw = jnp.pad(wiw, ((0, 0), (0, LANES - IDX_HEADS)))
            w_rope = jnp.concatenate([wq, wiq, _dup(wk), _dup(wik)], axis=1).astype(BF16)
            w_plain = jnp.concatenate([_dup(wv), wiw, wqm], axis=1).astype(BF16)
            iw_scale = IDX_HEADS ** -0.5 * IDX_DIM ** -0.5
            q, iq, kk, ikk, vv, iw, qm = _inproj(
                x2d, g_pre_mix[i], cos_t, sin_t, w_rope, w_plain,
                ((DSA_HEADS * HEAD_DIM, scale2, BF16), (IDX_HEADS * IDX_DIM, 1.0, BF16),
                 (LANES, 1.0, BF16), (LANES, 1.0, BF16)),
                ((LANES, 1.0, BF16), (LANES, iw_scale, F32), (MEM_WIDTH, scale, BF16)), tm_proj)
            o_seq = _dsa_attention(
                q, iq, iw, kk, vv, ikk, cq3.reshape(m // tq_dsa, tq_dsa, 1), ck3,
                vis_dsa[0], vis_dsa[1], b, s, tq_dsa, tk_dsa, top_k)
        x2d = _mix_mlp(x2d, o_seq, qm, kv, w_out[i].astype(BF16), g_post_mix[i], g_pre_mlp[i],
                       w_mlp_up[i].astype(BF16), w_mlp_down[i].astype(BF16), g_post_mlp[i],
                       s, n_mem, tm_mlp, tf_mlp)
    return x2d.reshape(b, s, d)
```

```python
import functools
import math

import jax
import jax.numpy as jnp
from jax import lax
from jax.experimental import pallas as pl
from jax.experimental.pallas import tpu as pltpu

D_MODEL = 1024
CHUNK = 64
HEAD_DIM = 64
LANES = 128
ROPE_THETA = 10000.0
EPS = 1e-6
MEM_HEADS = 4
MEM_WIDTH = MEM_HEADS * HEAD_DIM
SEQ_WIDTH = D_MODEL - MEM_WIDTH
DIFF_HEADS = SEQ_WIDTH // (2 * HEAD_DIM)
DSA_HEADS = SEQ_WIDTH // HEAD_DIM
IDX_HEADS = 8
IDX_DIM = 64
DSA_TOPK_MAX = 256
D_FF = 4 * D_MODEL

NEG = -1e30
INT_MIN = -(2 ** 31)
MASKED_KEY = -2139095041
TOP_PER_LANE = 12
CAND_ROWS = 32
CAND_GROUP = 4
assert TOP_PER_LANE % CAND_GROUP == 0
VMEM_LIMIT = 56 * 1024 * 1024

F32 = jnp.float32
BF16 = jnp.bfloat16
I32 = jnp.int32


def _dot(a, b):
    return jnp.dot(a, b, preferred_element_type=F32)


def _dot_nt(a, b):
    return lax.dot_general(a, b, (((1,), (1,)), ((), ())), preferred_element_type=F32)


def _lane_tile(x, width):
    return jnp.concatenate([x] * (width // LANES), axis=1)


def _key_to_f32(key):
    return lax.bitcast_convert_type(jnp.where(key < 0, key ^ 0x7FFFFFFF, key), F32)


def _rms(x, g):
    return x * lax.rsqrt(jnp.mean(x * x, axis=-1, keepdims=True) + EPS) * g


def _params(sem):
    return pltpu.CompilerParams(dimension_semantics=sem, vmem_limit_bytes=VMEM_LIMIT)


def _inproj_kernel(*refs, rope_outs, plain_outs):
    n_r, n_p = len(rope_outs), len(plain_outs)
    x_ref, g_ref = refs[0], refs[1]
    pos = 2
    if n_r:
        cos_ref, sin_ref, wr_ref = refs[2], refs[3], refs[4]
        pos = 5
    if n_p:
        wp_ref = refs[pos]
        pos += 1
    out_refs = refs[pos:]
    x = x_ref[...]
    h = _rms(x, g_ref[...]).astype(BF16)
    oi = 0
    if n_r:
        cos = cos_ref[...]
        sin = sin_ref[...]
        lane = lax.broadcasted_iota(I32, cos.shape, 1)
        low = (lane % HEAD_DIM) < (HEAD_DIM // 2)
        c0 = 0
        for width, scale in rope_outs:
            o_ref = out_refs[oi]
            oi += 1
            y_all = _dot(h, wr_ref[:, c0:c0 + width])
            for s in range(width // LANES):
                y = y_all[:, s * LANES:(s + 1) * LANES]
                swapped = jnp.where(low, pltpu.roll(y, LANES - HEAD_DIM // 2, 1),
                                    pltpu.roll(y, HEAD_DIM // 2, 1))
                r = y * cos + swapped * sin
                if scale != 1.0:
                    r = r * scale
                o_ref[:, s * LANES:(s + 1) * LANES] = r.astype(o_ref.dtype)
            c0 += width
    c0 = 0
    for width, scale in plain_outs:
        o_ref = out_refs[oi]
        oi += 1
        y = _dot(h, wp_ref[:, c0:c0 + width])
        if scale != 1.0:
            y = y * scale
        o_ref[...] = y.astype(o_ref.dtype)
        c0 += width


def _inproj(x2d, g, cos, sin, w_rope, w_plain, rope_outs, plain_outs, tm):
    m = x2d.shape[0]
    row = lambda i: (i, 0)
    fixed = lambda i: (0, 0)
    in_specs = [pl.BlockSpec((tm, D_MODEL), row), pl.BlockSpec((1, D_MODEL), fixed)]
    args = [x2d, g.reshape(1, D_MODEL)]
    if rope_outs:
        in_specs += [pl.BlockSpec((tm, LANES), row), pl.BlockSpec((tm, LANES), row),
                     pl.BlockSpec(w_rope.shape, fixed)]
        args += [cos, sin, w_rope]
    if plain_outs:
        in_specs += [pl.BlockSpec(w_plain.shape, fixed)]
        args += [w_plain]
    outs = tuple(rope_outs) + tuple(plain_outs)
    out_shape = [jax.ShapeDtypeStruct((m, w), dt) for w, _, dt in outs]
    out_specs = [pl.BlockSpec((tm, w), row) for w, _, _ in outs]
    kern = functools.partial(_inproj_kernel,
                             rope_outs=tuple((w, s) for w, s, _ in rope_outs),
                             plain_outs=tuple((w, s) for w, s, _ in plain_outs))
    return pl.pallas_call(
        kern, grid=(m // tm,), in_specs=in_specs, out_specs=out_specs, out_shape=out_shape,
        compiler_params=_params(("parallel",)), name="inproj")(*args)


def _diff_attn_kernel(status_ref, nhi_ref, q_ref, k_ref, v_ref, cq_ref, ck_ref, lam_ref, gsub_ref,
                      o_ref, m_s, acc_s, s0_s, s1_s, *, tq, tk, nq, nk, lam_init):
    b = pl.program_id(0)
    i = pl.program_id(2)
    q = q_ref[...]
    lane = lax.broadcasted_iota(I32, q.shape, 1)
    zero = jnp.zeros_like(q)
    qs = (jnp.where(lane < HEAD_DIM, q, zero), jnp.where(lane >= HEAD_DIM, q, zero))
    cq = cq_ref[0]
    m_s[...] = jnp.full(m_s.shape, NEG, F32)
    acc_s[...] = jnp.zeros(acc_s.shape, F32)
    base = (b * nq + i) * nk
    n_hi = nhi_ref[b * nq + i]

    def logits(t, dst_s):
        kt = k_ref[pl.ds(pl.multiple_of(t * tk, tk), tk), :]
        for j in range(2):
            dst_s[j] = _dot_nt(qs[j], kt)

    def tile(t, masked, cur_s, nxt_s):
        if nxt_s is not None:
            logits(t - 1, nxt_s)
        off = pl.multiple_of(t * tk, tk)
        vt = jnp.concatenate([v_ref[pl.ds(off, tk), :], jnp.ones((tk, LANES), BF16)], axis=1)
        if masked:
            ck = ck_ref[0, :, pl.ds(off, tk)]
            bias = jnp.where(cq >= ck, 0.0, NEG)
        for j in range(2):
            s = cur_s[j] + bias if masked else cur_s[j]
            m_old = m_s[j]
            m_new = jnp.maximum(m_old, jnp.max(s, axis=1, keepdims=True))
            p = jnp.exp2(s - _lane_tile(m_new, tk))
            alpha = jnp.exp2(m_old - m_new)
            acc_s[j] = _lane_tile(alpha, 2 * LANES) * acc_s[j] + _dot(p.astype(BF16), vt)
            m_s[j] = m_new

    def step(p, cur_s, nxt_s):
        t = n_hi - 1 - p
        st = status_ref[base + jnp.maximum(t, 0)]
        live = t >= 0
        more = t >= 1

        @pl.when(jnp.logical_and(st == 0, more))
        def _():
            logits(t - 1, nxt_s)

        for code, masked in ((1, False), (2, True)):
            hit = jnp.logical_and(live, st == code)

            @pl.when(jnp.logical_and(hit, more))
            def _():
                tile(t, masked, cur_s, nxt_s)

            @pl.when(jnp.logical_and(hit, jnp.logical_not(more)))
            def _():
                tile(t, masked, cur_s, None)

    logits(n_hi - 1, s0_s)

    def body(u, carry):
        step(2 * u, s0_s, s1_s)
        step(2 * u + 1, s1_s, s0_s)
        return carry

    lax.fori_loop(0, (n_hi + 1) // 2, body, 0)

    lam_p = lam_ref[...]
    lam = (jnp.exp(jnp.sum(lam_p[0:1] * lam_p[1:2], axis=1, keepdims=True))
           - jnp.exp(jnp.sum(lam_p[2:3] * lam_p[3:4], axis=1, keepdims=True)) + lam_init)
    o = (acc_s[0, :, :LANES] / acc_s[0, :, LANES:]
         - lam * (acc_s[1, :, :LANES] / acc_s[1, :, LANES:]))
    o = _rms(o, gsub_ref[...]) * (1.0 - lam_init)
    o_ref[...] = o.astype(o_ref.dtype)


def _diff_attention(q, k, v, cq3, ck3, status, nhi, lam_p, gsub, b, s, tq, tk, lam_init):
    nq, nk = s // tq, s // tk
    kern = functools.partial(_diff_attn_kernel, tq=tq, tk=tk, nq=nq, nk=nk, lam_init=lam_init)
    grid_spec = pltpu.PrefetchScalarGridSpec(
        num_scalar_prefetch=2,
        grid=(b, DIFF_HEADS, nq),
        in_specs=[
            pl.BlockSpec((tq, LANES), lambda bb, h, i, *_: (bb * nq + i, h)),
            pl.BlockSpec((s, LANES), lambda bb, h, i, *_: (bb, h)),
            pl.BlockSpec((s, LANES), lambda bb, h, i, *_: (bb, h)),
            pl.BlockSpec((1, tq, 1), lambda bb, h, i, *_: (bb * nq + i, 0, 0)),
            pl.BlockSpec((1, 1, s), lambda bb, h, i, *_: (bb, 0, 0)),
            pl.BlockSpec((4, HEAD_DIM), lambda bb, h, i, *_: (0, 0)),
            pl.BlockSpec((1, LANES), lambda bb, h, i, *_: (0, 0)),
        ],
        out_specs=pl.BlockSpec((tq, LANES), lambda bb, h, i, *_: (bb * nq + i, h)),
        scratch_shapes=[pltpu.VMEM((2, tq, LANES), F32),
                        pltpu.VMEM((2, tq, 2 * LANES), F32),
                        pltpu.VMEM((2, tq, tk), F32),
                        pltpu.VMEM((2, tq, tk), F32)],
    )
    return pl.pallas_call(
        kern, grid_spec=grid_spec,
        out_shape=jax.ShapeDtypeStruct((b * s, SEQ_WIDTH), BF16),
        compiler_params=_params(("parallel", "parallel", "arbitrary")), name="diff_attn",
    )(status, nhi, q, k, v, cq3, ck3, lam_p, gsub)


def _dsa_kernel(status_ref, nhi_ref, q_ref, iq_ref, iw_ref, kk_ref, vv_ref, ikk_ref, cq_ref, ck_ref,
                o_ref, keys_s, cand_s, qst_s, iqst_s, iwb_s, m_s, acc_s, thr_s, thf_s, jst_s,
                *, tq, tk, nq, nk, s_len, top_k):
    b = pl.program_id(0)
    i = pl.program_id(1)
    base = (b * nq + i) * nk
    n_hi = nhi_ref[b * nq + i]
    cq = cq_ref[0]
    lane = lax.broadcasted_iota(I32, (tq, LANES), 1)
    first = lane < HEAD_DIM
    spt = tk // LANES

    for h in range(DSA_HEADS):
        slab = q_ref[:, (h // 2) * LANES:(h // 2 + 1) * LANES]
        keep = first if h % 2 == 0 else jnp.logical_not(first)
        qst_s[h * tq:(h + 1) * tq, :] = jnp.where(keep, slab, jnp.zeros_like(slab))
    for h in range(IDX_HEADS):
        slab = iq_ref[:, (h // 2) * LANES:(h // 2 + 1) * LANES]
        keep = first if h % 2 == 0 else jnp.logical_not(first)
        iqst_s[h * tq:(h + 1) * tq, :] = jnp.where(keep, slab, jnp.zeros_like(slab))
        iwb_s[h * tq:(h + 1) * tq, :] = jnp.broadcast_to(iw_ref[:, h:h + 1], (tq, LANES))

    cand_s[...] = jnp.full(cand_s.shape, -jnp.inf, F32)

    def fold_candidates(sc):
        for g in range(tq // CAND_ROWS):
            rows = slice(g * CAND_ROWS, (g + 1) * CAND_ROWS)
            tops = [cand_s[r, rows, :] for r in range(TOP_PER_LANE)]
            for c in range(tk // LANES):
                x = sc[rows, c * LANES:(c + 1) * LANES]
                for r in range(TOP_PER_LANE):
                    tops[r], x = jnp.maximum(tops[r], x), jnp.minimum(tops[r], x)
            for r in range(TOP_PER_LANE):
                cand_s[r, rows, :] = tops[r]

    def score_body(t, carry):
        st = status_ref[base + t]
        off = pl.multiple_of(t * tk, tk)

        @pl.when(st == 0)
        def _():
            for c in range(spt):
                keys_s[t * spt + c] = jnp.full((tq, LANES), -jnp.inf, F32)

        def scores(masked):
            ikt = ikk_ref[pl.ds(off, tk), :]
            logits = _dot_nt(iqst_s[...], ikt)
            r = jnp.maximum(logits, 0.0) * _lane_tile(iwb_s[...], tk)
            sc = r[0:tq]
            for h in range(1, IDX_HEADS):
                sc = sc + r[h * tq:(h + 1) * tq]
            sc = jnp.where(sc == 0.0, 0.0, sc)
            if masked:
                ck = ck_ref[0, :, pl.ds(off, tk)]
                sc = jnp.where(cq >= ck, sc, -jnp.inf)
            for c in range(spt):
                keys_s[t * spt + c] = sc[:, c * LANES:(c + 1) * LANES]
            fold_candidates(sc)

        @pl.when(st == 1)
        def _():
            scores(False)

        @pl.when(st == 2)
        def _():
            scores(True)

        return carry

    lax.fori_loop(0, n_hi, score_body, 0)

    rg = min(tq, LANES)
    lane_rg = lax.broadcasted_iota(I32, (rg, LANES), 1)

    def count_tiles(preds, *operands):
        zero = jnp.zeros((rg, LANES), I32)
        accs = []
        for r in range(tq // rg):
            rows = slice(r * rg, (r + 1) * rg)
            ops = [o[rows] for o in operands]

            def body(t, acc, rows=rows, ops=ops):
                off = pl.multiple_of(t * tk, tk)
                acc = list(acc)
                for c in range(spt):
                    slab = keys_s[t * spt + c, rows, :]
                    for n, pred in enumerate(preds):
                        acc[n] = acc[n] + jnp.where(pred(slab, off + c * LANES, *ops), 1, 0).astype(I32)
                return tuple(acc)
            accs.append(lax.fori_loop(0, n_hi, body, (zero,) * len(preds)))
        outs = []
        for n in range(len(preds)):
            acc = accs[0][n] if len(accs) == 1 else jnp.concatenate([a[n] for a in accs], axis=0)
            outs.append(jnp.broadcast_to(jnp.sum(acc, axis=1, keepdims=True), (tq, LANES)))
        return outs

    ge = lambda slab, col0, c: slab >= c
    gt = lambda slab, col0, c: slab > c
    eq = lambda slab, col0, c: slab == c

    def count_cand(rows, c):
        n_groups = TOP_PER_LANE // CAND_GROUP
        full = [cand_s[CAND_GROUP * i + CAND_GROUP - 1, rows, :] >= c for i in range(n_groups)]
        base = jnp.full(c.shape, TOP_PER_LANE, I32)
        rest = [jnp.full(c.shape, -jnp.inf, F32)] * (CAND_GROUP - 1)
        for i in reversed(range(n_groups)):
            base = jnp.where(full[i], base, CAND_GROUP * i)
            rest = [jnp.where(full[i], rest[k], cand_s[CAND_GROUP * i + k, rows, :])
                    for k in range(CAND_GROUP - 1)]
        for x in rest:
            base = base + jnp.where(x >= c, 1, 0).astype(I32)
        return base

    def bisect(count):
        def bit_body(j, thr):
            cand = thr + lax.shift_left(jnp.int32(1), 31 - j)
            cnt, = count([ge], _key_to_f32(cand))
            return jnp.where(cnt >= top_k, cand, thr)
        return lax.fori_loop(0, 32, bit_body, jnp.full((tq, LANES), INT_MIN, I32))

    def bisect_candidates():
        half = tq // 2
        halves = (slice(0, half), slice(half, tq))

        def raw_count(rows, key):
            return count_cand(rows, _key_to_f32(key))

        def decide(acc, cand, thr):
            cnt = jnp.broadcast_to(jnp.sum(acc, axis=1, keepdims=True), (half, LANES))
            return jnp.where(cnt >= top_k, cand, thr)

        bit = lambda j: lax.shift_left(jnp.int32(1), 31 - j)
        start = jnp.full((half, LANES), INT_MIN, I32)

        def body(j, carry):
            thr_a, cand_a, acc_a, thr_b = carry
            cand_b = thr_b + bit(j)
            acc_b = raw_count(halves[1], cand_b)
            thr_a = decide(acc_a, cand_a, thr_a)
            cand_a = thr_a + bit(j + 1)
            acc_a = raw_count(halves[0], cand_a)
            thr_b = decide(acc_b, cand_b, thr_b)
            return thr_a, cand_a, acc_a, thr_b

        cand_a = start + bit(0)
        carry = (start, cand_a, raw_count(halves[0], cand_a), start)
        thr_a, cand_a, acc_a, thr_b = lax.fori_loop(0, 31, body, carry)
        thr_a = decide(acc_a, cand_a, thr_a)
        cand_b = thr_b + bit(31)
        thr_b = decide(raw_count(halves[1], cand_b), cand_b, thr_b)
        return jnp.concatenate([thr_a, thr_b], axis=0)

    thr = bisect_candidates()
    floor = _key_to_f32(jnp.maximum(thr, MASKED_KEY + 1))
    n_cand = jnp.broadcast_to(
        jnp.sum(count_cand(slice(0, tq), floor), axis=1, keepdims=True), (tq, LANES))
    n_all, = count_tiles([ge], floor)
    thr_s[...] = thr

    @pl.when(jnp.max((n_all != n_cand).astype(I32)) > 0)
    def _():
        thr_s[...] = bisect(count_tiles)

    thr_key = thr_s[...]
    real = thr_key > MASKED_KEY
    thr = jnp.where(real, _key_to_f32(thr_key), -jnp.inf)
    thf_s[...] = thr
    cnt_gt, cnt_eq = count_tiles([gt, eq], thr)
    need = top_k - cnt_gt
    excess = jnp.logical_and(real, cnt_eq > need)
    jst_s[...] = jnp.where(real, jnp.int32(s_len), jnp.int32(-1))

    @pl.when(jnp.max(excess.astype(I32)) > 0)
    def _():
        def idx_body(j, jcur):
            cand = jcur + lax.shift_left(jnp.int32(1), (s_len.bit_length() - 2) - j)
            cnt, = count_tiles(
                [lambda slab, col0, th, c: jnp.logical_and(slab == th, (col0 + lane_rg) < c)], thr, cand)
            return jnp.where(cnt < need, cand, jcur)
        jbest = lax.fori_loop(0, s_len.bit_length() - 1, idx_body, jnp.zeros((tq, LANES), I32))
        jst_s[...] = jnp.where(excess, jbest, jst_s[...])

    m_s[...] = jnp.full(m_s.shape, NEG, F32)
    acc_s[...] = jnp.zeros(acc_s.shape, F32)
    lane_k = lax.broadcasted_iota(I32, (tq, tk), 1)
    lane_v = lax.broadcasted_iota(I32, (tk, LANES), 1)

    def attn_body(t, carry):
        st = status_ref[base + t]

        @pl.when(st != 0)
        def _():
            off = pl.multiple_of(t * tk, tk)
            key = jnp.concatenate([keys_s[t * spt + c] for c in range(spt)], axis=1)
            th = _lane_tile(thf_s[...], tk)
            jst = _lane_tile(jst_s[...], tk)
            sel = jnp.logical_or(key > th, jnp.logical_and(key == th, (lane_k + off) <= jst))
            bias = jnp.where(sel, 0.0, NEG)
            kt = kk_ref[pl.ds(off, tk), :]
            vt = vv_ref[pl.ds(off, tk), :]
            vt = jnp.where(lane_v < HEAD_DIM, vt, jnp.ones_like(vt))
            s = _dot_nt(qst_s[...], kt)
            s = (s.reshape(DSA_HEADS, tq, tk) + bias[None]).reshape(DSA_HEADS * tq, tk)
            m_old = m_s[...]
            m_new = jnp.maximum(m_old, jnp.max(s, axis=1, keepdims=True))
            p = jnp.exp2(s - _lane_tile(m_new, tk))
            alpha = jnp.exp2(m_old - m_new)
            acc_s[...] = alpha * acc_s[...] + _dot(p.astype(BF16), vt)
            m_s[...] = m_new

        return carry

    lax.fori_loop(0, n_hi, attn_body, 0)

    for j in range(DSA_HEADS // 2):
        acc_a = acc_s[2 * j * tq:(2 * j + 1) * tq]
        acc_b = acc_s[(2 * j + 1) * tq:(2 * j + 2) * tq]
        swap_a = pltpu.roll(acc_a, HEAD_DIM, 1)
        swap_b = pltpu.roll(acc_b, HEAD_DIM, 1)
        o_ref[:, j * LANES:(j + 1) * LANES] = jnp.where(first, acc_a / swap_a,
                                                        swap_b / acc_b).astype(o_ref.dtype)


def _dsa_attention(q, iq, iw, kk, vv, ikk, cq3, ck3, status, nhi, b, s, tq, tk, top_k):
    nq, nk = s // tq, s // tk
    kern = functools.partial(_dsa_kernel, tq=tq, tk=tk, nq=nq, nk=nk, s_len=s, top_k=top_k)
    rowblk = lambda bb, i, *_: (bb * nq + i, 0)
    perb = lambda bb, i, *_: (bb, 0)
    grid_spec = pltpu.PrefetchScalarGridSpec(
        num_scalar_prefetch=2,
        grid=(b, nq),
        in_specs=[
            pl.BlockSpec((tq, SEQ_WIDTH), rowblk),
            pl.BlockSpec((tq, IDX_HEADS * IDX_DIM), rowblk),
            pl.BlockSpec((tq, LANES), rowblk),
            pl.BlockSpec((s, LANES), perb),
            pl.BlockSpec((s, LANES), perb),
            pl.BlockSpec((s, LANES), perb),
            pl.BlockSpec((1, tq, 1), lambda bb, i, *_: (bb * nq + i, 0, 0)),
            pl.BlockSpec((1, 1, s), lambda bb, i, *_: (bb, 0, 0)),
        ],
        out_specs=pl.BlockSpec((tq, SEQ_WIDTH), rowblk),
        scratch_shapes=[
            pltpu.VMEM((s // LANES, tq, LANES), F32),
            pltpu.VMEM((TOP_PER_LANE, tq, LANES), F32),
            pltpu.VMEM((DSA_HEADS * tq, LANES), BF16),
            pltpu.VMEM((IDX_HEADS * tq, LANES), BF16),
            pltpu.VMEM((IDX_HEADS * tq, LANES), F32),
            pltpu.VMEM((DSA_HEADS * tq, LANES), F32),
            pltpu.VMEM((DSA_HEADS * tq, LANES), F32),
            pltpu.VMEM((tq, LANES), I32),
            pltpu.VMEM((tq, LANES), F32),
            pltpu.VMEM((tq, LANES), I32),
        ],
    )
    return pl.pallas_call(
        kern, grid_spec=grid_spec,
        out_shape=jax.ShapeDtypeStruct((b * s, SEQ_WIDTH), BF16),
        compiler_params=_params(("parallel", "arbitrary")), name="dsa_attn",
    )(status, nhi, q, iq, iw, kk, vv, ikk, cq3, ck3)


def _mix_mlp_kernel(x_ref, oseq_ref, qm_ref, km_ref, vm_ref, wo_ref, gmix_ref, gpre_ref, wup_ref, wdn_ref,
                    gpost_ref, o_ref, h_s, acc_s):
    j = pl.program_id(1)

    @pl.when(j == 0)
    def _():
        qm = qm_ref[...]
        km = km_ref[...]
        vm = vm_ref[...]
        lane = lax.broadcasted_iota(I32, qm.shape, 1)
        o_mem = jnp.zeros(qm.shape, F32)
        for h in range(MEM_HEADS):
            mine = (lane // HEAD_DIM) == h
            s = _dot_nt(jnp.where(mine, qm, jnp.zeros_like(qm)), km)
            p = jnp.exp(s - jnp.max(s, axis=1, keepdims=True))
            p = p / jnp.sum(p, axis=1, keepdims=True)
            o_mem = o_mem + jnp.where(mine, _dot(p.astype(BF16), vm), 0.0)
        y = (_dot(oseq_ref[...], wo_ref[0:SEQ_WIDTH, :])
             + _dot(o_mem.astype(BF16), wo_ref[SEQ_WIDTH:, :]))
        x_mid = x_ref[...] + _rms(y, gmix_ref[...])
        o_ref[...] = x_mid
        h_s[...] = _rms(x_mid, gpre_ref[...]).astype(BF16)
        acc_s[...] = jnp.zeros(acc_s.shape, F32)

    u = jnp.maximum(_dot(h_s[...], wup_ref[...]), 0.0)
    acc_s[...] += _dot((u * u).astype(BF16), wdn_ref[...])

    @pl.when(j == pl.num_programs(1) - 1)
    def _():
        o_ref[...] = o_ref[...] + _rms(acc_s[...], gpost_ref[...])


def _mix_mlp(x2d, oseq, qm, kv, w_out, g_mix, g_pre, w_up, w_dn, g_post, s, n_mem, tm, tf):
    m = x2d.shape[0]
    nt = s // tm
    row = lambda i, j: (i, 0)
    fixed = lambda i, j: (0, 0)
    vec = lambda g: g.reshape(1, D_MODEL)
    return pl.pallas_call(
        _mix_mlp_kernel, grid=(m // tm, D_FF // tf),
        in_specs=[
            pl.BlockSpec((tm, D_MODEL), row),
            pl.BlockSpec((tm, SEQ_WIDTH), row),
            pl.BlockSpec((tm, MEM_WIDTH), row),
            pl.BlockSpec((n_mem, MEM_WIDTH), lambda i, j: (i // nt, 0)),
            pl.BlockSpec((n_mem, MEM_WIDTH), lambda i, j: (i // nt, 1)),
            pl.BlockSpec((D_MODEL, D_MODEL), fixed),
            pl.BlockSpec((1, D_MODEL), fixed),
            pl.BlockSpec((1, D_MODEL), fixed),
            pl.BlockSpec((D_MODEL, tf), lambda i, j: (0, j)),
            pl.BlockSpec((tf, D_MODEL), lambda i, j: (j, 0)),
            pl.BlockSpec((1, D_MODEL), fixed),
        ],
        out_specs=pl.BlockSpec((tm, D_MODEL), row),
        out_shape=jax.ShapeDtypeStruct(x2d.shape, F32),
        scratch_shapes=[pltpu.VMEM((tm, D_MODEL), BF16), pltpu.VMEM((tm, D_MODEL), F32)],
        compiler_params=_params(("parallel", "arbitrary")), name="mix_mlp",
    )(x2d, oseq, qm, kv, kv, w_out, vec(g_mix), vec(g_pre), w_up, w_dn, vec(g_post))


def _rope_tables(positions):
    half = HEAD_DIM // 2
    inv = ROPE_THETA ** (-jnp.arange(0, HEAD_DIM, 2, dtype=F32) / HEAD_DIM)
    ang = positions.astype(F32).reshape(-1, 1) * inv
    cos, sin = jnp.cos(ang), jnp.sin(ang)
    reps = LANES // HEAD_DIM
    cos_t = jnp.tile(cos, (1, 2 * reps))
    sin_t = jnp.tile(jnp.concatenate([-sin, sin], axis=1), (1, reps))
    assert cos_t.shape[1] == LANES and half * 2 == HEAD_DIM
    return cos_t, sin_t


def _visibility(chunk_id, tq, tk):
    b, s = chunk_id.shape
    cq = chunk_id.reshape(b, s // tq, tq)
    ck = chunk_id.reshape(b, s // tk, tk)
    qmin, qmax = cq.min(-1)[:, :, None], cq.max(-1)[:, :, None]
    kmin, kmax = ck.min(-1)[:, None, :], ck.max(-1)[:, None, :]
    status = jnp.where(kmin > qmax, 0, jnp.where(kmax <= qmin, 1, 2)).astype(I32)
    tiles = jnp.arange(s // tk, dtype=I32)[None, None, :]
    nhi = jnp.max(jnp.where(status != 0, tiles + 1, 0), axis=-1).astype(I32)
    return status.reshape(-1), nhi.reshape(-1)


def _dup(w):
    return jnp.concatenate([w, w], axis=1)


def kernel(x, mem, positions, g_pre_mix, g_post_mix, g_mem, w_mem_kv, w_out, g_pre_mlp, g_post_mlp,
           w_mlp_up, w_mlp_down, w_in_diff, lambda_q1, lambda_k1, lambda_q2, lambda_k2, g_diff_subln,
           w_in_dsa):
    b, s, d = x.shape
    n_mem = mem.shape[1]
    depth = g_pre_mix.shape[0]
    m = b * s
    scale = HEAD_DIM ** -0.5
    scale2 = scale * math.log2(math.e)
    tm_proj = min(512, s)
    tq_diff, tk_diff = min(512, s), min(1024, s)
    tq_dsa, tk_dsa = min(256, s), min(512, s)
    tm_mlp, tf_mlp = min(1024, s), 1024
    top_k = min(DSA_TOPK_MAX, s // 4)
    assert tk_dsa >= top_k, "the threshold search needs one key tile to hold top_k candidates"

    cos_t, sin_t = _rope_tables(positions)
    chunk_id = positions // CHUNK
    cq3 = chunk_id.reshape(-1, 1)
    ck3 = chunk_id.reshape(b, 1, s)
    vis_diff = _visibility(chunk_id, tq_diff, tk_diff)
    vis_dsa = _visibility(chunk_id, tq_dsa, tk_dsa)

    x2d = x.reshape(m, d)
    mem2d = mem.reshape(b * n_mem, d)
    for i in range(depth):
        j = i // 2
        kv = _inproj(mem2d, g_mem[i], None, None, None, w_mem_kv[i].astype(BF16), (),
                     ((2 * MEM_WIDTH, 1.0, BF16),), min(256, b * n_mem))[0]
        if i % 2 == 0:
            w = w_in_diff[j].astype(BF16)
            nqk = 2 * DIFF_HEADS * HEAD_DIM
            q, k, v, qm = _inproj(
                x2d, g_pre_mix[i], cos_t, sin_t, w[:, :2 * nqk], w[:, 2 * nqk:],
                ((nqk, scale2, BF16), (nqk, 1.0, BF16)),
                ((SEQ_WIDTH, 1.0, BF16), (MEM_WIDTH, scale, BF16)), tm_proj)
            lam_p = jnp.stack([lambda_q1[j], lambda_k1[j], lambda_q2[j], lambda_k2[j]]).astype(F32)
            lam_init = 0.8 - 0.6 * math.exp(-0.3 * i)
            o_seq = _diff_attention(
                q, k, v, cq3.reshape(m // tq_diff, tq_diff, 1), ck3, vis_diff[0], vis_diff[1],
                lam_p, g_diff_subln[j].reshape(1, LANES), b, s, tq_diff, tk_diff, lam_init)
        else:
            w = w_in_dsa[j]
            o0 = 0
            parts = []
            for width in (DSA_HEADS * HEAD_DIM, HEAD_DIM, HEAD_DIM, IDX_HEADS * IDX_DIM, IDX_DIM,
                          IDX_HEADS, MEM_WIDTH):
                parts.append(w[:, o0:o0 + width])
                o0 += width
            wq, wk, wv, wiq, wik, wiw, wqm = parts
            wiw = jnp.pad(wiw, ((0, 0), (0, LANES - IDX_HEADS)))
            w_rope = jnp.concatenate([wq, wiq, _dup(wk), _dup(wik)], axis=1).astype(BF16)
            w_plain = jnp.concatenate([_dup(wv), wiw, wqm], axis=1).astype(BF16)
            iw_scale = IDX_HEADS ** -0.5 * IDX_DIM ** -0.5
            q, iq, kk, ikk, vv, iw, qm = _inproj(
                x2d, g_pre_mix[i], cos_t, sin_t, w_rope, w_plain,
                ((DSA_HEADS * HEAD_DIM, scale2, BF16), (IDX_HEADS * IDX_DIM, 1.0, BF16),
                 (LANES, 1.0, BF16), (LANES, 1.0, BF16)),
                ((LANES, 1.0, BF16), (LANES, iw_scale, F32), (MEM_WIDTH, scale, BF16)), tm_proj)
            o_seq = _dsa_attention(
                q, iq, iw, kk, vv, ikk, cq3.reshape(m // tq_dsa, tq_dsa, 1), ck3,
                vis_dsa[0], vis_dsa[1], b, s, tq_dsa, tk_dsa, top_k)
        x2d = _mix_mlp(x2d, o_seq, qm, kv, w_out[i].astype(BF16), g_post_mix[i], g_pre_mlp[i],
                       w_mlp_up[i].astype(BF16), w_mlp_down[i].astype(BF16), g_post_mlp[i],
                       s, n_mem, tm_mlp, tf_mlp)
    return x2d.reshape(b, s, d)
```

```python
import functools
import math

import jax
import jax.numpy as jnp
from jax import lax
from jax.experimental import pallas as pl
from jax.experimental.pallas import tpu as pltpu

D_MODEL = 1024
CHUNK = 64
HEAD_DIM = 64
LANES = 128
ROPE_THETA = 10000.0
EPS = 1e-6
MEM_HEADS = 4
MEM_WIDTH = MEM_HEADS * HEAD_DIM
SEQ_WIDTH = D_MODEL - MEM_WIDTH
DIFF_HEADS = SEQ_WIDTH // (2 * HEAD_DIM)
DSA_HEADS = SEQ_WIDTH // HEAD_DIM
IDX_HEADS = 8
IDX_DIM = 64
DSA_TOPK_MAX = 256
D_FF = 4 * D_MODEL

NEG = -1e30
INT_MIN = -(2 ** 31)
MASKED_KEY = -2139095041
TOP_PER_LANE = 12
CAND_ROWS = 32
CAND_GROUP = 2
assert TOP_PER_LANE % CAND_GROUP == 0
VMEM_LIMIT = 56 * 1024 * 1024

F32 = jnp.float32
BF16 = jnp.bfloat16
I32 = jnp.int32


def _dot(a, b):
    return jnp.dot(a, b, preferred_element_type=F32)


def _dot_nt(a, b):
    return lax.dot_general(a, b, (((1,), (1,)), ((), ())), preferred_element_type=F32)


def _lane_tile(x, width):
    return jnp.concatenate([x] * (width // LANES), axis=1)


def _row_total(counts):
    total = jnp.sum(counts.astype(F32), axis=1, keepdims=True)
    return jnp.broadcast_to(total, counts.shape).astype(I32)


def _key_to_f32(key):
    return lax.bitcast_convert_type(jnp.where(key < 0, key ^ 0x7FFFFFFF, key), F32)


def _rms(x, g):
    return x * lax.rsqrt(jnp.mean(x * x, axis=-1, keepdims=True) + EPS) * g


def _params(sem):
    return pltpu.CompilerParams(dimension_semantics=sem, vmem_limit_bytes=VMEM_LIMIT)


def _inproj_kernel(*refs, rope_outs, plain_outs):
    n_r, n_p = len(rope_outs), len(plain_outs)
    x_ref, g_ref = refs[0], refs[1]
    pos = 2
    if n_r:
        cos_ref, sin_ref, wr_ref = refs[2], refs[3], refs[4]
        pos = 5
    if n_p:
        wp_ref = refs[pos]
        pos += 1
    out_refs = refs[pos:]
    x = x_ref[...]
    h = _rms(x, g_ref[...]).astype(BF16)
    oi = 0
    if n_r:
        cos = cos_ref[...]
        sin = sin_ref[...]
        lane = lax.broadcasted_iota(I32, cos.shape, 1)
        low = (lane % HEAD_DIM) < (HEAD_DIM // 2)
        c0 = 0
        for width, scale in rope_outs:
            o_ref = out_refs[oi]
            oi += 1
            y_all = _dot(h, wr_ref[:, c0:c0 + width])
            for s in range(width // LANES):
                y = y_all[:, s * LANES:(s + 1) * LANES]
                swapped = jnp.where(low, pltpu.roll(y, LANES - HEAD_DIM // 2, 1),
                                    pltpu.roll(y, HEAD_DIM // 2, 1))
                r = y * cos + swapped * sin
                if scale != 1.0:
                    r = r * scale
                o_ref[:, s * LANES:(s + 1) * LANES] = r.astype(o_ref.dtype)
            c0 += width
    c0 = 0
    for width, scale in plain_outs:
        o_ref = out_refs[oi]
        oi += 1
        y = _dot(h, wp_ref[:, c0:c0 + width])
        if scale != 1.0:
            y = y * scale
        o_ref[...] = y.astype(o_ref.dtype)
        c0 += width


def _inproj(x2d, g, cos, sin, w_rope, w_plain, rope_outs, plain_outs, tm):
    m = x2d.shape[0]
    row = lambda i: (i, 0)
    fixed = lambda i: (0, 0)
    in_specs = [pl.BlockSpec((tm, D_MODEL), row), pl.BlockSpec((1, D_MODEL), fixed)]
    args = [x2d, g.reshape(1, D_MODEL)]
    if rope_outs:
        in_specs += [pl.BlockSpec((tm, LANES), row), pl.BlockSpec((tm, LANES), row),
                     pl.BlockSpec(w_rope.shape, fixed)]
        args += [cos, sin, w_rope]
    if plain_outs:
        in_specs += [pl.BlockSpec(w_plain.shape, fixed)]
        args += [w_plain]
    outs = tuple(rope_outs) + tuple(plain_outs)
    out_shape = [jax.ShapeDtypeStruct((m, w), dt) for w, _, dt in outs]
    out_specs = [pl.BlockSpec((tm, w), row) for w, _, _ in outs]
    kern = functools.partial(_inproj_kernel,
                             rope_outs=tuple((w, s) for w, s, _ in rope_outs),
                             plain_outs=tuple((w, s) for w, s, _ in plain_outs))
    return pl.pallas_call(
        kern, grid=(m // tm,), in_specs=in_specs, out_specs=out_specs, out_shape=out_shape,
        compiler_params=_params(("parallel",)), name="inproj")(*args)


def _diff_attn_kernel(status_ref, nhi_ref, q_ref, k_ref, v_ref, cq_ref, ck_ref, lam_ref, gsub_ref,
                      o_ref, m_s, acc_s, s0_s, s1_s, *, tq, tk, nq, nk, lam_init):
    b = pl.program_id(0)
    i = pl.program_id(2)
    q = q_ref[...]
    lane = lax.broadcasted_iota(I32, q.shape, 1)
    zero = jnp.zeros_like(q)
    qs = (jnp.where(lane < HEAD_DIM, q, zero), jnp.where(lane >= HEAD_DIM, q, zero))
    cq = cq_ref[0]
    m_s[...] = jnp.full(m_s.shape, NEG, F32)
    acc_s[...] = jnp.zeros(acc_s.shape, F32)
    base = (b * nq + i) * nk
    n_hi = nhi_ref[b * nq + i]

    def logits(t, dst_s):
        kt = k_ref[pl.ds(pl.multiple_of(t * tk, tk), tk), :]
        for j in range(2):
            dst_s[j] = _dot_nt(qs[j], kt)

    def tile(t, masked, cur_s, nxt_s):
        if nxt_s is not None:
            logits(t - 1, nxt_s)
        off = pl.multiple_of(t * tk, tk)
        vt = jnp.concatenate([v_ref[pl.ds(off, tk), :], jnp.ones((tk, LANES), BF16)], axis=1)
        if masked:
            ck = ck_ref[0, :, pl.ds(off, tk)]
            bias = jnp.where(cq >= ck, 0.0, NEG)
        for j in range(2):
            s = cur_s[j] + bias if masked else cur_s[j]
            m_old = m_s[j]
            m_new = jnp.maximum(m_old, jnp.max(s, axis=1, keepdims=True))
            p = jnp.exp2(s - _lane_tile(m_new, tk))
            alpha = jnp.exp2(m_old - m_new)
            acc_s[j] = _lane_tile(alpha, 2 * LANES) * acc_s[j] + _dot(p.astype(BF16), vt)
            m_s[j] = m_new

    def step(p, cur_s, nxt_s):
        t = n_hi - 1 - p
        st = status_ref[base + jnp.maximum(t, 0)]
        live = t >= 0
        more = t >= 1

        @pl.when(jnp.logical_and(st == 0, more))
        def _():
            logits(t - 1, nxt_s)

        for code, masked in ((1, False), (2, True)):
            hit = jnp.logical_and(live, st == code)

            @pl.when(jnp.logical_and(hit, more))
            def _():
                tile(t, masked, cur_s, nxt_s)

            @pl.when(jnp.logical_and(hit, jnp.logical_not(more)))
            def _():
                tile(t, masked, cur_s, None)

    logits(n_hi - 1, s0_s)

    def body(u, carry):
        step(2 * u, s0_s, s1_s)
        step(2 * u + 1, s1_s, s0_s)
        return carry

    lax.fori_loop(0, (n_hi + 1) // 2, body, 0)

    lam_p = lam_ref[...]
    lam = (jnp.exp(jnp.sum(lam_p[0:1] * lam_p[1:2], axis=1, keepdims=True))
           - jnp.exp(jnp.sum(lam_p[2:3] * lam_p[3:4], axis=1, keepdims=True)) + lam_init)
    o = (acc_s[0, :, :LANES] / acc_s[0, :, LANES:]
         - lam * (acc_s[1, :, :LANES] / acc_s[1, :, LANES:]))
    o = _rms(o, gsub_ref[...]) * (1.0 - lam_init)
    o_ref[...] = o.astype(o_ref.dtype)


def _diff_attention(q, k, v, cq3, ck3, status, nhi, lam_p, gsub, b, s, tq, tk, lam_init):
    nq, nk = s // tq, s // tk
    kern = functools.partial(_diff_attn_kernel, tq=tq, tk=tk, nq=nq, nk=nk, lam_init=lam_init)
    grid_spec = pltpu.PrefetchScalarGridSpec(
        num_scalar_prefetch=2,
        grid=(b, DIFF_HEADS, nq),
        in_specs=[
            pl.BlockSpec((tq, LANES), lambda bb, h, i, *_: (bb * nq + i, h)),
            pl.BlockSpec((s, LANES), lambda bb, h, i, *_: (bb, h)),
            pl.BlockSpec((s, LANES), lambda bb, h, i, *_: (bb, h)),
            pl.BlockSpec((1, tq, 1), lambda bb, h, i, *_: (bb * nq + i, 0, 0)),
            pl.BlockSpec((1, 1, s), lambda bb, h, i, *_: (bb, 0, 0)),
            pl.BlockSpec((4, HEAD_DIM), lambda bb, h, i, *_: (0, 0)),
            pl.BlockSpec((1, LANES), lambda bb, h, i, *_: (0, 0)),
        ],
        out_specs=pl.BlockSpec((tq, LANES), lambda bb, h, i, *_: (bb * nq + i, h)),
        scratch_shapes=[pltpu.VMEM((2, tq, LANES), F32),
                        pltpu.VMEM((2, tq, 2 * LANES), F32),
                        pltpu.VMEM((2, tq, tk), F32),
                        pltpu.VMEM((2, tq, tk), F32)],
    )
    return pl.pallas_call(
        kern, grid_spec=grid_spec,
        out_shape=jax.ShapeDtypeStruct((b * s, SEQ_WIDTH), BF16),
        compiler_params=_params(("parallel", "parallel", "arbitrary")), name="diff_attn",
    )(status, nhi, q, k, v, cq3, ck3, lam_p, gsub)


def _dsa_kernel(status_ref, nhi_ref, q_ref, iq_ref, iw_ref, kk_ref, vv_ref, ikk_ref, cq_ref, ck_ref,
                o_ref, keys_s, cand_s, qst_s, iqst_s, iwb_s, m_s, acc_s, thr_s, thf_s, jst_s,
                *, tq, tk, nq, nk, s_len, top_k):
    b = pl.program_id(0)
    i = pl.program_id(1)
    base = (b * nq + i) * nk
    n_hi = nhi_ref[b * nq + i]
    cq = cq_ref[0]
    lane = lax.broadcasted_iota(I32, (tq, LANES), 1)
    first = lane < HEAD_DIM
    spt = tk // LANES

    for h in range(DSA_HEADS):
        slab = q_ref[:, (h // 2) * LANES:(h // 2 + 1) * LANES]
        keep = first if h % 2 == 0 else jnp.logical_not(first)
        qst_s[h * tq:(h + 1) * tq, :] = jnp.where(keep, slab, jnp.zeros_like(slab))
    for h in range(IDX_HEADS):
        slab = iq_ref[:, (h // 2) * LANES:(h // 2 + 1) * LANES]
        keep = first if h % 2 == 0 else jnp.logical_not(first)
        iqst_s[h * tq:(h + 1) * tq, :] = jnp.where(keep, slab, jnp.zeros_like(slab))
        iwb_s[h * tq:(h + 1) * tq, :] = jnp.broadcast_to(iw_ref[:, h:h + 1], (tq, LANES))

    cand_s[...] = jnp.full(cand_s.shape, -jnp.inf, F32)

    def fold_candidates(sc):
        for g in range(tq // CAND_ROWS):
            rows = slice(g * CAND_ROWS, (g + 1) * CAND_ROWS)
            tops = [cand_s[r, rows, :] for r in range(TOP_PER_LANE)]
            for c in range(tk // LANES):
                x = sc[rows, c * LANES:(c + 1) * LANES]
                for r in range(TOP_PER_LANE):
                    tops[r], x = jnp.maximum(tops[r], x), jnp.minimum(tops[r], x)
            for r in range(TOP_PER_LANE):
                cand_s[r, rows, :] = tops[r]

    def score_body(t, carry):
        st = status_ref[base + t]
        off = pl.multiple_of(t * tk, tk)

        @pl.when(st == 0)
        def _():
            for c in range(spt):
                keys_s[t * spt + c] = jnp.full((tq, LANES), -jnp.inf, F32)

        def scores(masked):
            ikt = ikk_ref[pl.ds(off, tk), :]
            logits = _dot_nt(iqst_s[...], ikt)
            r = jnp.maximum(logits, 0.0) * _lane_tile(iwb_s[...], tk)
            sc = r[0:tq]
            for h in range(1, IDX_HEADS):
                sc = sc + r[h * tq:(h + 1) * tq]
            sc = jnp.where(sc == 0.0, 0.0, sc)
            if masked:
                ck = ck_ref[0, :, pl.ds(off, tk)]
                sc = jnp.where(cq >= ck, sc, -jnp.inf)
            for c in range(spt):
                keys_s[t * spt + c] = sc[:, c * LANES:(c + 1) * LANES]
            fold_candidates(sc)

        @pl.when(st == 1)
        def _():
            scores(False)

        @pl.when(st == 2)
        def _():
            scores(True)

        return carry

    lax.fori_loop(0, n_hi, score_body, 0)

    rg = min(tq, LANES)
    lane_rg = lax.broadcasted_iota(I32, (rg, LANES), 1)

    def count_tiles(preds, *operands):
        zero = jnp.zeros((rg, LANES), I32)
        accs = []
        for r in range(tq // rg):
            rows = slice(r * rg, (r + 1) * rg)
            ops = [o[rows] for o in operands]

            def body(t, acc, rows=rows, ops=ops):
                off = pl.multiple_of(t * tk, tk)
                acc = list(acc)
                for c in range(spt):
                    slab = keys_s[t * spt + c, rows, :]
                    for n, pred in enumerate(preds):
                        acc[n] = acc[n] + jnp.where(pred(slab, off + c * LANES, *ops), 1, 0).astype(I32)
                return tuple(acc)
            accs.append(lax.fori_loop(0, n_hi, body, (zero,) * len(preds)))
        outs = []
        for n in range(len(preds)):
            acc = accs[0][n] if len(accs) == 1 else jnp.concatenate([a[n] for a in accs], axis=0)
            outs.append(_row_total(acc))
        return outs

    ge = lambda slab, col0, c: slab >= c
    gt = lambda slab, col0, c: slab > c
    eq = lambda slab, col0, c: slab == c

    def count_cand(rows, c):
        n_groups = TOP_PER_LANE // CAND_GROUP
        full = [cand_s[CAND_GROUP * i + CAND_GROUP - 1, rows, :] >= c for i in range(n_groups)]
        base = jnp.full(c.shape, TOP_PER_LANE, I32)
        rest = [jnp.full(c.shape, -jnp.inf, F32)] * (CAND_GROUP - 1)
        for i in reversed(range(n_groups)):
            base = jnp.where(full[i], base, CAND_GROUP * i)
            rest = [jnp.where(full[i], rest[k], cand_s[CAND_GROUP * i + k, rows, :])
                    for k in range(CAND_GROUP - 1)]
        for x in rest:
            base = base + jnp.where(x >= c, 1, 0).astype(I32)
        return base

    def bisect(count):
        def bit_body(j, thr):
            cand = thr + lax.shift_left(jnp.int32(1), 31 - j)
            cnt, = count([ge], _key_to_f32(cand))
            return jnp.where(cnt >= top_k, cand, thr)
        return lax.fori_loop(0, 32, bit_body, jnp.full((tq, LANES), INT_MIN, I32))

    def bisect_candidates():
        half = tq // 2
        halves = (slice(0, half), slice(half, tq))

        def raw_count(rows, key):
            return count_cand(rows, _key_to_f32(key))

        def decide(acc, cand, thr):
            return jnp.where(_row_total(acc) >= top_k, cand, thr)

        bit = lambda j: lax.shift_left(jnp.int32(1), 31 - j)
        start = jnp.full((half, LANES), INT_MIN, I32)

        def body(j, carry):
            thr_a, cand_a, acc_a, thr_b = carry
            cand_b = thr_b + bit(j)
            acc_b = raw_count(halves[1], cand_b)
            thr_a = decide(acc_a, cand_a, thr_a)
            cand_a = thr_a + bit(j + 1)
            acc_a = raw_count(halves[0], cand_a)
            thr_b = decide(acc_b, cand_b, thr_b)
            return thr_a, cand_a, acc_a, thr_b

        cand_a = start + bit(0)
        carry = (start, cand_a, raw_count(halves[0], cand_a), start)
        thr_a, cand_a, acc_a, thr_b = lax.fori_loop(0, 31, body, carry)
        thr_a = decide(acc_a, cand_a, thr_a)
        cand_b = thr_b + bit(31)
        thr_b = decide(raw_count(halves[1], cand_b), cand_b, thr_b)
        return jnp.concatenate([thr_a, thr_b], axis=0)

    thr = bisect_candidates()
    floor = _key_to_f32(jnp.maximum(thr, MASKED_KEY + 1))
    n_cand = _row_total(count_cand(slice(0, tq), floor))
    n_all, = count_tiles([ge], floor)
    thr_s[...] = thr

    @pl.when(jnp.max((n_all != n_cand).astype(I32)) > 0)
    def _():
        thr_s[...] = bisect(count_tiles)

    thr_key = thr_s[...]
    real = thr_key > MASKED_KEY
    thr = jnp.where(real, _key_to_f32(thr_key), -jnp.inf)
    thf_s[...] = thr
    cnt_gt, cnt_eq = count_tiles([gt, eq], thr)
    need = top_k - cnt_gt
    excess = jnp.logical_and(real, cnt_eq > need)
    jst_s[...] = jnp.where(real, jnp.int32(s_len), jnp.int32(-1))

    @pl.when(jnp.max(excess.astype(I32)) > 0)
    def _():
        def idx_body(j, jcur):
            cand = jcur + lax.shift_left(jnp.int32(1), (s_len.bit_length() - 2) - j)
            cnt, = count_tiles(
                [lambda slab, col0, th, c: jnp.logical_and(slab == th, (col0 + lane_rg) < c)], thr, cand)
            return jnp.where(cnt < need, cand, jcur)
        jbest = lax.fori_loop(0, s_len.bit_length() - 1, idx_body, jnp.zeros((tq, LANES), I32))
        jst_s[...] = jnp.where(excess, jbest, jst_s[...])

    m_s[...] = jnp.full(m_s.shape, NEG, F32)
    acc_s[...] = jnp.zeros(acc_s.shape, F32)
    lane_k = lax.broadcasted_iota(I32, (tq, tk), 1)
    lane_v = lax.broadcasted_iota(I32, (tk, LANES), 1)

    def attn_body(t, carry):
        st = status_ref[base + t]

        @pl.when(st != 0)
        def _():
            off = pl.multiple_of(t * tk, tk)
            key = jnp.concatenate([keys_s[t * spt + c] for c in range(spt)], axis=1)
            th = _lane_tile(thf_s[...], tk)
            jst = _lane_tile(jst_s[...], tk)
            sel = jnp.logical_or(key > th, jnp.logical_and(key == th, (lane_k + off) <= jst))
            bias = jnp.where(sel, 0.0, NEG)
            kt = kk_ref[pl.ds(off, tk), :]
            vt = vv_ref[pl.ds(off, tk), :]
            vt = jnp.where(lane_v < HEAD_DIM, vt, jnp.ones_like(vt))
            s = _dot_nt(qst_s[...], kt)
            s = (s.reshape(DSA_HEADS, tq, tk) + bias[None]).reshape(DSA_HEADS * tq, tk)
            m_old = m_s[...]
            m_new = jnp.maximum(m_old, jnp.max(s, axis=1, keepdims=True))
            p = jnp.exp2(s - _lane_tile(m_new, tk))
            alpha = jnp.exp2(m_old - m_new)
            acc_s[...] = alpha * acc_s[...] + _dot(p.astype(BF16), vt)
            m_s[...] = m_new

        return carry

    lax.fori_loop(0, n_hi, attn_body, 0)

    for j in range(DSA_HEADS // 2):
        acc_a = acc_s[2 * j * tq:(2 * j + 1) * tq]
        acc_b = acc_s[(2 * j + 1) * tq:(2 * j + 2) * tq]
        swap_a = pltpu.roll(acc_a, HEAD_DIM, 1)
        swap_b = pltpu.roll(acc_b, HEAD_DIM, 1)
        o_ref[:, j * LANES:(j + 1) * LANES] = jnp.where(first, acc_a / swap_a,
                                                        swap_b / acc_b).astype(o_ref.dtype)


def _dsa_attention(q, iq, iw, kk, vv, ikk, cq3, ck3, status, nhi, b, s, tq, tk, top_k):
    nq, nk = s // tq, s // tk
    kern = functools.partial(_dsa_kernel, tq=tq, tk=tk, nq=nq, nk=nk, s_len=s, top_k=top_k)
    rowblk = lambda bb, i, *_: (bb * nq + i, 0)
    perb = lambda bb, i, *_: (bb, 0)
    grid_spec = pltpu.PrefetchScalarGridSpec(
        num_scalar_prefetch=2,
        grid=(b, nq),
        in_specs=[
            pl.BlockSpec((tq, SEQ_WIDTH), rowblk),
            pl.BlockSpec((tq, IDX_HEADS * IDX_DIM), rowblk),
            pl.BlockSpec((tq, LANES), rowblk),
            pl.BlockSpec((s, LANES), perb),
            pl.BlockSpec((s, LANES), perb),
            pl.BlockSpec((s, LANES), perb),
            pl.BlockSpec((1, tq, 1), lambda bb, i, *_: (bb * nq + i, 0, 0)),
            pl.BlockSpec((1, 1, s), lambda bb, i, *_: (bb, 0, 0)),
        ],
        out_specs=pl.BlockSpec((tq, SEQ_WIDTH), rowblk),
        scratch_shapes=[
            pltpu.VMEM((s // LANES, tq, LANES), F32),
            pltpu.VMEM((TOP_PER_LANE, tq, LANES), F32),
            pltpu.VMEM((DSA_HEADS * tq, LANES), BF16),
            pltpu.VMEM((IDX_HEADS * tq, LANES), BF16),
            pltpu.VMEM((IDX_HEADS * tq, LANES), F32),
            pltpu.VMEM((DSA_HEADS * tq, LANES), F32),
            pltpu.VMEM((DSA_HEADS * tq, LANES), F32),
            pltpu.VMEM((tq, LANES), I32),
            pltpu.VMEM((tq, LANES), F32),
            pltpu.VMEM((tq, LANES), I32),
        ],
    )
    return pl.pallas_call(
        kern, grid_spec=grid_spec,
        out_shape=jax.ShapeDtypeStruct((b * s, SEQ_WIDTH), BF16),
        compiler_params=_params(("parallel", "arbitrary")), name="dsa_attn",
    )(status, nhi, q, iq, iw, kk, vv, ikk, cq3, ck3)


def _mix_mlp_kernel(x_ref, oseq_ref, qm_ref, km_ref, vm_ref, wo_ref, gmix_ref, gpre_ref, wup_ref, wdn_ref,
                    gpost_ref, o_ref, h_s, acc_s):
    j = pl.program_id(1)

    @pl.when(j == 0)
    def _():
        qm = qm_ref[...]
        km = km_ref[...]
        vm = vm_ref[...]
        lane = lax.broadcasted_iota(I32, qm.shape, 1)
        o_mem = jnp.zeros(qm.shape, F32)
        for h in range(MEM_HEADS):
            mine = (lane // HEAD_DIM) == h
            s = _dot_nt(jnp.where(mine, qm, jnp.zeros_like(qm)), km)
            p = jnp.exp(s - jnp.max(s, axis=1, keepdims=True))
            p = p / jnp.sum(p, axis=1, keepdims=True)
            o_mem = o_mem + jnp.where(mine, _dot(p.astype(BF16), vm), 0.0)
        y = (_dot(oseq_ref[...], wo_ref[0:SEQ_WIDTH, :])
             + _dot(o_mem.astype(BF16), wo_ref[SEQ_WIDTH:, :]))
        x_mid = x_ref[...] + _rms(y, gmix_ref[...])
        o_ref[...] = x_mid
        h_s[...] = _rms(x_mid, gpre_ref[...]).astype(BF16)
        acc_s[...] = jnp.zeros(acc_s.shape, F32)

    u = jnp.maximum(_dot(h_s[...], wup_ref[...]), 0.0)
    acc_s[...] += _dot((u * u).astype(BF16), wdn_ref[...])

    @pl.when(j == pl.num_programs(1) - 1)
    def _():
        o_ref[...] = o_ref[...] + _rms(acc_s[...], gpost_ref[...])


def _mix_mlp(x2d, oseq, qm, kv, w_out, g_mix, g_pre, w_up, w_dn, g_post, s, n_mem, tm, tf):
    m = x2d.shape[0]
    nt = s // tm
    row = lambda i, j: (i, 0)
    fixed = lambda i, j: (0, 0)
    vec = lambda g: g.reshape(1, D_MODEL)
    return pl.pallas_call(
        _mix_mlp_kernel, grid=(m // tm, D_FF // tf),
        in_specs=[
            pl.BlockSpec((tm, D_MODEL), row),
            pl.BlockSpec((tm, SEQ_WIDTH), row),
            pl.BlockSpec((tm, MEM_WIDTH), row),
            pl.BlockSpec((n_mem, MEM_WIDTH), lambda i, j: (i // nt, 0)),
            pl.BlockSpec((n_mem, MEM_WIDTH), lambda i, j: (i // nt, 1)),
            pl.BlockSpec((D_MODEL, D_MODEL), fixed),
            pl.BlockSpec((1, D_MODEL), fixed),
            pl.BlockSpec((1, D_MODEL), fixed),
            pl.BlockSpec((D_MODEL, tf), lambda i, j: (0, j)),
            pl.BlockSpec((tf, D_MODEL), lambda i, j: (j, 0)),
            pl.BlockSpec((1, D_MODEL), fixed),
        ],
        out_specs=pl.BlockSpec((tm, D_MODEL), row),
        out_shape=jax.ShapeDtypeStruct(x2d.shape, F32),
        scratch_shapes=[pltpu.VMEM((tm, D_MODEL), BF16), pltpu.VMEM((tm, D_MODEL), F32)],
        compiler_params=_params(("parallel", "arbitrary")), name="mix_mlp",
    )(x2d, oseq, qm, kv, kv, w_out, vec(g_mix), vec(g_pre), w_up, w_dn, vec(g_post))


def _rope_tables(positions):
    half = HEAD_DIM // 2
    inv = ROPE_THETA ** (-jnp.arange(0, HEAD_DIM, 2, dtype=F32) / HEAD_DIM)
    ang = positions.astype(F32).reshape(-1, 1) * inv
    cos, sin = jnp.cos(ang), jnp.sin(ang)
    reps = LANES // HEAD_DIM
    cos_t = jnp.tile(cos, (1, 2 * reps))
    sin_t = jnp.tile(jnp.concatenate([-sin, sin], axis=1), (1, reps))
    assert cos_t.shape[1] == LANES and half * 2 == HEAD_DIM
    return cos_t, sin_t


def _visibility(chunk_id, tq, tk):
    b, s = chunk_id.shape
    cq = chunk_id.reshape(b, s // tq, tq)
    ck = chunk_id.reshape(b, s // tk, tk)
    qmin, qmax = cq.min(-1)[:, :, None], cq.max(-1)[:, :, None]
    kmin, kmax = ck.min(-1)[:, None, :], ck.max(-1)[:, None, :]
    status = jnp.where(kmin > qmax, 0, jnp.where(kmax <= qmin, 1, 2)).astype(I32)
    tiles = jnp.arange(s // tk, dtype=I32)[None, None, :]
    nhi = jnp.max(jnp.where(status != 0, tiles + 1, 0), axis=-1).astype(I32)
    return status.reshape(-1), nhi.reshape(-1)


def _dup(w):
    return jnp.concatenate([w, w], axis=1)


def kernel(x, mem, positions, g_pre_mix, g_post_mix, g_mem, w_mem_kv, w_out, g_pre_mlp, g_post_mlp,
           w_mlp_up, w_mlp_down, w_in_diff, lambda_q1, lambda_k1, lambda_q2, lambda_k2, g_diff_subln,
           w_in_dsa):
    b, s, d = x.shape
    n_mem = mem.shape[1]
    depth = g_pre_mix.shape[0]
    m = b * s
    scale = HEAD_DIM ** -0.5
    scale2 = scale * math.log2(math.e)
    tm_proj = min(512, s)
    tq_diff, tk_diff = min(512, s), min(1024, s)
    tq_dsa, tk_dsa = min(256, s), min(512, s)
    tm_mlp, tf_mlp = min(1024, s), 1024
    top_k = min(DSA_TOPK_MAX, s // 4)
    assert tk_dsa >= top_k, "the threshold search needs one key tile to hold top_k candidates"

    cos_t, sin_t = _rope_tables(positions)
    chunk_id = positions // CHUNK
    cq3 = chunk_id.reshape(-1, 1)
    ck3 = chunk_id.reshape(b, 1, s)
    vis_diff = _visibility(chunk_id, tq_diff, tk_diff)
    vis_dsa = _visibility(chunk_id, tq_dsa, tk_dsa)

    x2d = x.reshape(m, d)
    mem2d = mem.reshape(b * n_mem, d)
    for i in range(depth):
        j = i // 2
        kv = _inproj(mem2d, g_mem[i], None, None, None, w_mem_kv[i].astype(BF16), (),
                     ((2 * MEM_WIDTH, 1.0, BF16),), min(256, b * n_mem))[0]
        if i % 2 == 0:
            w = w_in_diff[j].astype(BF16)
            nqk = 2 * DIFF_HEADS * HEAD_DIM
            q, k, v, qm = _inproj(
                x2d, g_pre_mix[i], cos_t, sin_t, w[:, :2 * nqk], w[:, 2 * nqk:],
                ((nqk, scale2, BF16), (nqk, 1.0, BF16)),
                ((SEQ_WIDTH, 1.0, BF16), (MEM_WIDTH, scale, BF16)), tm_proj)
            lam_p = jnp.stack([lambda_q1[j], lambda_k1[j], lambda_q2[j], lambda_k2[j]]).astype(F32)
            lam_init = 0.8 - 0.6 * math.exp(-0.3 * i)
            o_seq = _diff_attention(
                q, k, v, cq3.reshape(m // tq_diff, tq_diff, 1), ck3, vis_diff[0], vis_diff[1],
                lam_p, g_diff_subln[j].reshape(1, LANES), b, s, tq_diff, tk_diff, lam_init)
        else:
            w = w_in_dsa[j]
            o0 = 0
            parts = []
            for width in (DSA_HEADS * HEAD_DIM, HEAD_DIM, HEAD_DIM, IDX_HEADS * IDX_DIM, IDX_DIM,
                          IDX_HEADS, MEM_WIDTH):
                parts.append(w[:, o0:o0 + width])
                o0 += width
            wq, wk, wv, wiq, wik, wiw, wqm = parts
            wiw = jnp.pad(wiw, ((0, 0), (0, LANES - IDX_HEADS)))
            w_rope = jnp.concatenate([wq, wiq, _dup(wk), _dup(wik)], axis=1).astype(BF16)
            w_plain = jnp.concatenate([_dup(wv), wiw, wqm], axis=1).astype(BF16)
            iw_scale = IDX_HEADS ** -0.5 * IDX_DIM ** -0.5
            q, iq, kk, ikk, vv, iw, qm = _inproj(
                x2d, g_pre_mix[i], cos_t, sin_t, w_rope, w_plain,
                ((DSA_HEADS * HEAD_DIM, scale2, BF16), (IDX_HEADS * IDX_DIM, 1.0, BF16),
                 (LANES, 1.0, BF16), (LANES, 1.0, BF16)),
                ((LANES, 1.0, BF16), (LANES, iw_scale, F32), (MEM_WIDTH, scale, BF16)), tm_proj)
            o_seq = _dsa_attention(
                q, iq, iw, kk, vv, ikk, cq3.reshape(m // tq_dsa, tq_dsa, 1), ck3,
                vis_dsa[0], vis_dsa[1], b, s, tq_dsa, tk_dsa, top_k)
        x2d = _mix_mlp(x2d, o_seq, qm, kv, w_out[i].astype(BF16), g_post_mix[i], g_pre_mlp[i],
                       w_mlp_up[i].astype(BF16), w_mlp_down[i].astype(BF16), g_post_mlp[i],
                       s, n_mem, tm_mlp, tf_mlp)
    return x2d.reshape(b, s, d)
```

```python
import functools
import math

import jax
import jax.numpy as jnp
from jax import lax
from jax.experimental import pallas as pl
from jax.experimental.pallas import tpu as pltpu

D_MODEL = 1024
CHUNK = 64
HEAD_DIM = 64
LANES = 128
ROPE_THETA = 10000.0
EPS = 1e-6
MEM_HEADS = 4
MEM_WIDTH = MEM_HEADS * HEAD_DIM
SEQ_WIDTH = D_MODEL - MEM_WIDTH
DIFF_HEADS = SEQ_WIDTH // (2 * HEAD_DIM)
DSA_HEADS = SEQ_WIDTH // HEAD_DIM
IDX_HEADS = 8
IDX_DIM = 64
DSA_TOPK_MAX = 256
D_FF = 4 * D_MODEL

NEG = -1e30
INT_MIN = -(2 ** 31)
MASKED_KEY = -2139095041
TOP_PER_LANE = 12
CAND_ROWS = 32
CAND_GROUP = 2
assert TOP_PER_LANE % CAND_GROUP == 0
VMEM_LIMIT = 56 * 1024 * 1024

F32 = jnp.float32
BF16 = jnp.bfloat16
I32 = jnp.int32


def _dot(a, b):
    return jnp.dot(a, b, preferred_element_type=F32)


def _dot_nt(a, b):
    return lax.dot_general(a, b, (((1,), (1,)), ((), ())), preferred_element_type=F32)


def _lane_tile(x, width):
    return jnp.concatenate([x] * (width // LANES), axis=1)


def _row_total(counts):
    total = jnp.sum(counts.astype(F32), axis=1, keepdims=True)
    return jnp.broadcast_to(total, counts.shape).astype(I32)


def _key_to_f32(key):
    return lax.bitcast_convert_type(jnp.where(key < 0, key ^ 0x7FFFFFFF, key), F32)


def _rms(x, g):
    return x * lax.rsqrt(jnp.mean(x * x, axis=-1, keepdims=True) + EPS) * g


def _params(sem):
    return pltpu.CompilerParams(dimension_semantics=sem, vmem_limit_bytes=VMEM_LIMIT)


def _inproj_kernel(*refs, rope_outs, plain_outs):
    n_r, n_p = len(rope_outs), len(plain_outs)
    x_ref, g_ref = refs[0], refs[1]
    pos = 2
    if n_r:
        cos_ref, sin_ref, wr_ref = refs[2], refs[3], refs[4]
        pos = 5
    if n_p:
        wp_ref = refs[pos]
        pos += 1
    out_refs = refs[pos:]
    x = x_ref[...]
    h = _rms(x, g_ref[...]).astype(BF16)
    oi = 0
    if n_r:
        cos = cos_ref[...]
        sin = sin_ref[...]
        lane = lax.broadcasted_iota(I32, cos.shape, 1)
        low = (lane % HEAD_DIM) < (HEAD_DIM // 2)
        c0 = 0
        for width, scale in rope_outs:
            o_ref = out_refs[oi]
            oi += 1
            y_all = _dot(h, wr_ref[:, c0:c0 + width])
            for s in range(width // LANES):
                y = y_all[:, s * LANES:(s + 1) * LANES]
                swapped = jnp.where(low, pltpu.roll(y, LANES - HEAD_DIM // 2, 1),
                                    pltpu.roll(y, HEAD_DIM // 2, 1))
                r = y * cos + swapped * sin
                if scale != 1.0:
                    r = r * scale
                o_ref[:, s * LANES:(s + 1) * LANES] = r.astype(o_ref.dtype)
            c0 += width
    c0 = 0
    for width, scale in plain_outs:
        o_ref = out_refs[oi]
        oi += 1
        y = _dot(h, wp_ref[:, c0:c0 + width])
        if scale != 1.0:
            y = y * scale
        o_ref[...] = y.astype(o_ref.dtype)
        c0 += width


def _inproj(x2d, g, cos, sin, w_rope, w_plain, rope_outs, plain_outs, tm):
    m = x2d.shape[0]
    row = lambda i: (i, 0)
    fixed = lambda i: (0, 0)
    in_specs = [pl.BlockSpec((tm, D_MODEL), row), pl.BlockSpec((1, D_MODEL), fixed)]
    args = [x2d, g.reshape(1, D_MODEL)]
    if rope_outs:
        in_specs += [pl.BlockSpec((tm, LANES), row), pl.BlockSpec((tm, LANES), row),
                     pl.BlockSpec(w_rope.shape, fixed)]
        args += [cos, sin, w_rope]
    if plain_outs:
        in_specs += [pl.BlockSpec(w_plain.shape, fixed)]
        args += [w_plain]
    outs = tuple(rope_outs) + tuple(plain_outs)
    out_shape = [jax.ShapeDtypeStruct((m, w), dt) for w, _, dt in outs]
    out_specs = [pl.BlockSpec((tm, w), row) for w, _, _ in outs]
    kern = functools.partial(_inproj_kernel,
                             rope_outs=tuple((w, s) for w, s, _ in rope_outs),
                             plain_outs=tuple((w, s) for w, s, _ in plain_outs))
    return pl.pallas_call(
        kern, grid=(m // tm,), in_specs=in_specs, out_specs=out_specs, out_shape=out_shape,
        compiler_params=_params(("parallel",)), name="inproj")(*args)


def _diff_attn_kernel(status_ref, nhi_ref, q_ref, k_ref, v_ref, cq_ref, ck_ref, lam_ref, gsub_ref,
                      o_ref, m_s, acc_s, s0_s, s1_s, *, tq, tk, nq, nk, lam_init):
    b = pl.program_id(0)
    lane = lax.broadcasted_iota(I32, (tq, LANES), 1)
    lam_p = lam_ref[...]
    lam = (jnp.exp(jnp.sum(lam_p[0:1] * lam_p[1:2], axis=1, keepdims=True))
           - jnp.exp(jnp.sum(lam_p[2:3] * lam_p[3:4], axis=1, keepdims=True)) + lam_init)

    def logits(blk, t, dst_s):
        q = q_ref[pl.ds(pl.multiple_of(blk * tq, tq), tq), :]
        zero = jnp.zeros_like(q)
        kt = k_ref[pl.ds(pl.multiple_of(t * tk, tk), tk), :]
        dst_s[0] = _dot_nt(jnp.where(lane < HEAD_DIM, q, zero), kt)
        dst_s[1] = _dot_nt(jnp.where(lane >= HEAD_DIM, q, zero), kt)

    def tile(blk, t, masked, cur_s, nxt):
        if nxt is not None:
            logits(*nxt)
        off = pl.multiple_of(t * tk, tk)
        vt = jnp.concatenate([v_ref[pl.ds(off, tk), :], jnp.ones((tk, LANES), BF16)], axis=1)
        if masked:
            ck = ck_ref[0, :, pl.ds(off, tk)]
            bias = jnp.where(cq_ref[blk] >= ck, 0.0, NEG)
        for j in range(2):
            s = cur_s[j] + bias if masked else cur_s[j]
            m_old = m_s[j]
            m_new = jnp.maximum(m_old, jnp.max(s, axis=1, keepdims=True))
            p = jnp.exp2(s - _lane_tile(m_new, tk))
            alpha = jnp.exp2(m_old - m_new)
            acc_s[j] = _lane_tile(alpha, 2 * LANES) * acc_s[j] + _dot(p.astype(BF16), vt)
            m_s[j] = m_new

    def step(blk, n_hi, first, g, cur_s, nxt_s):
        t = n_hi - 1 - (g - first)
        live = jnp.logical_and(g >= first, t >= 0)
        st = status_ref[(b * nq + blk) * nk + jnp.clip(t, 0, nk - 1)]
        here = t >= 1
        n_next = nhi_ref[b * nq + jnp.minimum(blk + 1, nq - 1)]
        nxt = (jnp.where(here, blk, blk + 1), jnp.where(here, t - 1, n_next - 1), nxt_s)
        some = jnp.logical_or(here, blk + 1 < nq)

        @pl.when(jnp.logical_and(jnp.logical_and(live, st == 0), some))
        def _():
            logits(*nxt)

        for code, masked in ((1, False), (2, True)):
            hit = jnp.logical_and(live, st == code)

            @pl.when(jnp.logical_and(hit, some))
            def _():
                tile(blk, t, masked, cur_s, nxt)

            @pl.when(jnp.logical_and(hit, jnp.logical_not(some)))
            def _():
                tile(blk, t, masked, cur_s, None)

    def block(blk, first):
        n_hi = nhi_ref[b * nq + blk]
        m_s[...] = jnp.full(m_s.shape, NEG, F32)
        acc_s[...] = jnp.zeros(acc_s.shape, F32)

        def pair(u, carry):
            step(blk, n_hi, first, 2 * u, s0_s, s1_s)
            step(blk, n_hi, first, 2 * u + 1, s1_s, s0_s)
            return carry

        lax.fori_loop(first // 2, (first + n_hi + 1) // 2, pair, 0)
        o = (acc_s[0, :, :LANES] / acc_s[0, :, LANES:]
             - lam * (acc_s[1, :, :LANES] / acc_s[1, :, LANES:]))
        o = _rms(o, gsub_ref[...]) * (1.0 - lam_init)
        o_ref[pl.ds(pl.multiple_of(blk * tq, tq), tq), :] = o.astype(o_ref.dtype)
        return first + n_hi

    logits(0, nhi_ref[b * nq] - 1, s0_s)
    lax.fori_loop(0, nq, block, jnp.int32(0))


def _diff_attention(q, k, v, cq3, ck3, status, nhi, lam_p, gsub, b, s, tq, tk, lam_init):
    nq, nk = s // tq, s // tk
    kern = functools.partial(_diff_attn_kernel, tq=tq, tk=tk, nq=nq, nk=nk, lam_init=lam_init)
    per_head = lambda bb, h, *_: (bb, h)
    grid_spec = pltpu.PrefetchScalarGridSpec(
        num_scalar_prefetch=2,
        grid=(b, DIFF_HEADS),
        in_specs=[
            pl.BlockSpec((s, LANES), per_head),
            pl.BlockSpec((s, LANES), per_head),
            pl.BlockSpec((s, LANES), per_head),
            pl.BlockSpec((nq, tq, 1), lambda bb, h, *_: (bb, 0, 0)),
            pl.BlockSpec((1, 1, s), lambda bb, h, *_: (bb, 0, 0)),
            pl.BlockSpec((4, HEAD_DIM), lambda bb, h, *_: (0, 0)),
            pl.BlockSpec((1, LANES), lambda bb, h, *_: (0, 0)),
        ],
        out_specs=pl.BlockSpec((s, LANES), per_head),
        scratch_shapes=[pltpu.VMEM((2, tq, LANES), F32),
                        pltpu.VMEM((2, tq, 2 * LANES), F32),
                        pltpu.VMEM((2, tq, tk), F32),
                        pltpu.VMEM((2, tq, tk), F32)],
    )
    return pl.pallas_call(
        kern, grid_spec=grid_spec,
        out_shape=jax.ShapeDtypeStruct((b * s, SEQ_WIDTH), BF16),
        compiler_params=_params(("parallel", "parallel")), name="diff_attn",
    )(status, nhi, q, k, v, cq3, ck3, lam_p, gsub)


def _dsa_kernel(status_ref, nhi_ref, q_ref, iq_ref, iw_ref, kk_ref, vv_ref, ikk_ref, cq_ref, ck_ref,
                o_ref, keys_s, cand_s, qst_s, iqst_s, iwb_s, m_s, acc_s, thr_s, thf_s, jst_s,
                *, tq, tk, nq, nk, s_len, top_k):
    b = pl.program_id(0)
    i = pl.program_id(1)
    base = (b * nq + i) * nk
    n_hi = nhi_ref[b * nq + i]
    cq = cq_ref[0]
    lane = lax.broadcasted_iota(I32, (tq, LANES), 1)
    first = lane < HEAD_DIM
    spt = tk // LANES

    for h in range(DSA_HEADS):
        slab = q_ref[:, (h // 2) * LANES:(h // 2 + 1) * LANES]
        keep = first if h % 2 == 0 else jnp.logical_not(first)
        qst_s[h * tq:(h + 1) * tq, :] = jnp.where(keep, slab, jnp.zeros_like(slab))
    for h in range(IDX_HEADS):
        slab = iq_ref[:, (h // 2) * LANES:(h // 2 + 1) * LANES]
        keep = first if h % 2 == 0 else jnp.logical_not(first)
        iqst_s[h * tq:(h + 1) * tq, :] = jnp.where(keep, slab, jnp.zeros_like(slab))
        iwb_s[h * tq:(h + 1) * tq, :] = jnp.broadcast_to(iw_ref[:, h:h + 1], (tq, LANES))

    cand_s[...] = jnp.full(cand_s.shape, -jnp.inf, F32)

    def fold_candidates(sc):
        for g in range(tq // CAND_ROWS):
            rows = slice(g * CAND_ROWS, (g + 1) * CAND_ROWS)
            tops = [cand_s[r, rows, :] for r in range(TOP_PER_LANE)]
            for c in range(tk // LANES):
                x = sc[rows, c * LANES:(c + 1) * LANES]
                for r in range(TOP_PER_LANE):
                    tops[r], x = jnp.maximum(tops[r], x), jnp.minimum(tops[r], x)
            for r in range(TOP_PER_LANE):
                cand_s[r, rows, :] = tops[r]

    def score_body(t, carry):
        st = status_ref[base + t]
        off = pl.multiple_of(t * tk, tk)

        @pl.when(st == 0)
        def _():
            for c in range(spt):
                keys_s[t * spt + c] = jnp.full((tq, LANES), -jnp.inf, F32)

        def scores(masked):
            ikt = ikk_ref[pl.ds(off, tk), :]
            logits = _dot_nt(iqst_s[...], ikt)
            r = jnp.maximum(logits, 0.0) * _lane_tile(iwb_s[...], tk)
            sc = r[0:tq]
            for h in range(1, IDX_HEADS):
                sc = sc + r[h * tq:(h + 1) * tq]
            sc = jnp.where(sc == 0.0, 0.0, sc)
            if masked:
                ck = ck_ref[0, :, pl.ds(off, tk)]
                sc = jnp.where(cq >= ck, sc, -jnp.inf)
            for c in range(spt):
                keys_s[t * spt + c] = sc[:, c * LANES:(c + 1) * LANES]
            fold_candidates(sc)

        @pl.when(st == 1)
        def _():
            scores(False)

        @pl.when(st == 2)
        def _():
            scores(True)

        return carry

    lax.fori_loop(0, n_hi, score_body, 0)

    rg = min(tq, LANES)
    lane_rg = lax.broadcasted_iota(I32, (rg, LANES), 1)

    def count_tiles(preds, *operands):
        zero = jnp.zeros((rg, LANES), I32)
        accs = []
        for r in range(tq // rg):
            rows = slice(r * rg, (r + 1) * rg)
            ops = [o[rows] for o in operands]

            def body(t, acc, rows=rows, ops=ops):
                off = pl.multiple_of(t * tk, tk)
                acc = list(acc)
                for c in range(spt):
                    slab = keys_s[t * spt + c, rows, :]
                    for n, pred in enumerate(preds):
                        acc[n] = acc[n] + jnp.where(pred(slab, off + c * LANES, *ops), 1, 0).astype(I32)
                return tuple(acc)
            accs.append(lax.fori_loop(0, n_hi, body, (zero,) * len(preds)))
        outs = []
        for n in range(len(preds)):
            acc = accs[0][n] if len(accs) == 1 else jnp.concatenate([a[n] for a in accs], axis=0)
            outs.append(_row_total(acc))
        return outs

    ge = lambda slab, col0, c: slab >= c
    gt = lambda slab, col0, c: slab > c
    eq = lambda slab, col0, c: slab == c

    def count_cand(rows, c):
        n_groups = TOP_PER_LANE // CAND_GROUP
        full = [cand_s[CAND_GROUP * i + CAND_GROUP - 1, rows, :] >= c for i in range(n_groups)]
        base = jnp.full(c.shape, TOP_PER_LANE, I32)
        rest = [jnp.full(c.shape, -jnp.inf, F32)] * (CAND_GROUP - 1)
        for i in reversed(range(n_groups)):
            base = jnp.where(full[i], base, CAND_GROUP * i)
            rest = [jnp.where(full[i], rest[k], cand_s[CAND_GROUP * i + k, rows, :])
                    for k in range(CAND_GROUP - 1)]
        for x in rest:
            base = base + jnp.where(x >= c, 1, 0).astype(I32)
        return base

    def bisect(count):
        def bit_body(j, thr):
            cand = thr + lax.shift_left(jnp.int32(1), 31 - j)
            cnt, = count([ge], _key_to_f32(cand))
            return jnp.where(cnt >= top_k, cand, thr)
        return lax.fori_loop(0, 32, bit_body, jnp.full((tq, LANES), INT_MIN, I32))

    def bisect_candidates():
        half = tq // 2
        halves = (slice(0, half), slice(half, tq))

        def raw_count(rows, key):
            return count_cand(rows, _key_to_f32(key))

        def decide(acc, cand, thr):
            return jnp.where(_row_total(acc) >= top_k, cand, thr)

        bit = lambda j: lax.shift_left(jnp.int32(1), 31 - j)
        start = jnp.full((half, LANES), INT_MIN, I32)

        def body(j, carry):
            thr_a, cand_a, acc_a, thr_b = carry
            cand_b = thr_b + bit(j)
            acc_b = raw_count(halves[1], cand_b)
            thr_a = decide(acc_a, cand_a, thr_a)
            cand_a = thr_a + bit(j + 1)
            acc_a = raw_count(halves[0], cand_a)
            thr_b = decide(acc_b, cand_b, thr_b)
            return thr_a, cand_a, acc_a, thr_b

        cand_a = start + bit(0)
        carry = (start, cand_a, raw_count(halves[0], cand_a), start)
        thr_a, cand_a, acc_a, thr_b = lax.fori_loop(0, 31, body, carry)
        thr_a = decide(acc_a, cand_a, thr_a)
        cand_b = thr_b + bit(31)
        thr_b = decide(raw_count(halves[1], cand_b), cand_b, thr_b)
        return jnp.concatenate([thr_a, thr_b], axis=0)

    thr = bisect_candidates()
    floor = _key_to_f32(jnp.maximum(thr, MASKED_KEY + 1))
    n_cand = _row_total(count_cand(slice(0, tq), floor))
    n_all, = count_tiles([ge], floor)
    thr_s[...] = thr

    @pl.when(jnp.max((n_all != n_cand).astype(I32)) > 0)
    def _():
        thr_s[...] = bisect(count_tiles)

    thr_key = thr_s[...]
    real = thr_key > MASKED_KEY
    thr = jnp.where(real, _key_to_f32(thr_key), -jnp.inf)
    thf_s[...] = thr
    cnt_gt, cnt_eq = count_tiles([gt, eq], thr)
    need = top_k - cnt_gt
    excess = jnp.logical_and(real, cnt_eq > need)
    jst_s[...] = jnp.where(real, jnp.int32(s_len), jnp.int32(-1))

    @pl.when(jnp.max(excess.astype(I32)) > 0)
    def _():
        def idx_body(j, jcur):
            cand = jcur + lax.shift_left(jnp.int32(1), (s_len.bit_length() - 2) - j)
            cnt, = count_tiles(
                [lambda slab, col0, th, c: jnp.logical_and(slab == th, (col0 + lane_rg) < c)], thr, cand)
            return jnp.where(cnt < need, cand, jcur)
        jbest = lax.fori_loop(0, s_len.bit_length() - 1, idx_body, jnp.zeros((tq, LANES), I32))
        jst_s[...] = jnp.where(excess, jbest, jst_s[...])

    m_s[...] = jnp.full(m_s.shape, NEG, F32)
    acc_s[...] = jnp.zeros(acc_s.shape, F32)
    lane_k = lax.broadcasted_iota(I32, (tq, tk), 1)
    lane_v = lax.broadcasted_iota(I32, (tk, LANES), 1)

    def attn_body(t, carry):
        st = status_ref[base + t]

        @pl.when(st != 0)
        def _():
            off = pl.multiple_of(t * tk, tk)
            key = jnp.concatenate([keys_s[t * spt + c] for c in range(spt)], axis=1)
            th = _lane_tile(thf_s[...], tk)
            jst = _lane_tile(jst_s[...], tk)
            sel = jnp.logical_or(key > th, jnp.logical_and(key == th, (lane_k + off) <= jst))
            bias = jnp.where(sel, 0.0, NEG)
            kt = kk_ref[pl.ds(off, tk), :]
            vt = vv_ref[pl.ds(off, tk), :]
            vt = jnp.where(lane_v < HEAD_DIM, vt, jnp.ones_like(vt))
            s = _dot_nt(qst_s[...], kt)
            s = (s.reshape(DSA_HEADS, tq, tk) + bias[None]).reshape(DSA_HEADS * tq, tk)
            m_old = m_s[...]
            m_new = jnp.maximum(m_old, jnp.max(s, axis=1, keepdims=True))
            p = jnp.exp2(s - _lane_tile(m_new, tk))
            alpha = jnp.exp2(m_old - m_new)
            acc_s[...] = alpha * acc_s[...] + _dot(p.astype(BF16), vt)
            m_s[...] = m_new

        return carry

    lax.fori_loop(0, n_hi, attn_body, 0)

    for j in range(DSA_HEADS // 2):
        acc_a = acc_s[2 * j * tq:(2 * j + 1) * tq]
        acc_b = acc_s[(2 * j + 1) * tq:(2 * j + 2) * tq]
        swap_a = pltpu.roll(acc_a, HEAD_DIM, 1)
        swap_b = pltpu.roll(acc_b, HEAD_DIM, 1)
        o_ref[:, j * LANES:(j + 1) * LANES] = jnp.where(first, acc_a / swap_a,
                                                        swap_b / acc_b).astype(o_ref.dtype)


def _dsa_attention(q, iq, iw, kk, vv, ikk, cq3, ck3, status, nhi, b, s, tq, tk, top_k):
    nq, nk = s // tq, s // tk
    kern = functools.partial(_dsa_kernel, tq=tq, tk=tk, nq=nq, nk=nk, s_len=s, top_k=top_k)
    rowblk = lambda bb, i, *_: (bb * nq + i, 0)
    perb = lambda bb, i, *_: (bb, 0)
    grid_spec = pltpu.PrefetchScalarGridSpec(
        num_scalar_prefetch=2,
        grid=(b, nq),
        in_specs=[
            pl.BlockSpec((tq, SEQ_WIDTH), rowblk),
            pl.BlockSpec((tq, IDX_HEADS * IDX_DIM), rowblk),
            pl.BlockSpec((tq, LANES), rowblk),
            pl.BlockSpec((s, LANES), perb),
            pl.BlockSpec((s, LANES), perb),
            pl.BlockSpec((s, LANES), perb),
            pl.BlockSpec((1, tq, 1), lambda bb, i, *_: (bb * nq + i, 0, 0)),
            pl.BlockSpec((1, 1, s), lambda bb, i, *_: (bb, 0, 0)),
        ],
        out_specs=pl.BlockSpec((tq, SEQ_WIDTH), rowblk),
        scratch_shapes=[
            pltpu.VMEM((s // LANES, tq, LANES), F32),
            pltpu.VMEM((TOP_PER_LANE, tq, LANES), F32),
            pltpu.VMEM((DSA_HEADS * tq, LANES), BF16),
            pltpu.VMEM((IDX_HEADS * tq, LANES), BF16),
            pltpu.VMEM((IDX_HEADS * tq, LANES), F32),
            pltpu.VMEM((DSA_HEADS * tq, LANES), F32),
            pltpu.VMEM((DSA_HEADS * tq, LANES), F32),
            pltpu.VMEM((tq, LANES), I32),
            pltpu.VMEM((tq, LANES), F32),
            pltpu.VMEM((tq, LANES), I32),
        ],
    )
    return pl.pallas_call(
        kern, grid_spec=grid_spec,
        out_shape=jax.ShapeDtypeStruct((b * s, SEQ_WIDTH), BF16),
        compiler_params=_params(("parallel", "arbitrary")), name="dsa_attn",
    )(status, nhi, q, iq, iw, kk, vv, ikk, cq3, ck3)


def _mix_mlp_kernel(x_ref, oseq_ref, qm_ref, km_ref, vm_ref, wo_ref, gmix_ref, gpre_ref, wup_ref, wdn_ref,
                    gpost_ref, o_ref, h_s, acc_s):
    j = pl.program_id(1)

    @pl.when(j == 0)
    def _():
        qm = qm_ref[...]
        km = km_ref[...]
        vm = vm_ref[...]
        lane = lax.broadcasted_iota(I32, qm.shape, 1)
        o_mem = jnp.zeros(qm.shape, F32)
        for h in range(MEM_HEADS):
            mine = (lane // HEAD_DIM) == h
            s = _dot_nt(jnp.where(mine, qm, jnp.zeros_like(qm)), km)
            p = jnp.exp(s - jnp.max(s, axis=1, keepdims=True))
            p = p / jnp.sum(p, axis=1, keepdims=True)
            o_mem = o_mem + jnp.where(mine, _dot(p.astype(BF16), vm), 0.0)
        y = (_dot(oseq_ref[...], wo_ref[0:SEQ_WIDTH, :])
             + _dot(o_mem.astype(BF16), wo_ref[SEQ_WIDTH:, :]))
        x_mid = x_ref[...] + _rms(y, gmix_ref[...])
        o_ref[...] = x_mid
        h_s[...] = _rms(x_mid, gpre_ref[...]).astype(BF16)
        acc_s[...] = jnp.zeros(acc_s.shape, F32)

    u = jnp.maximum(_dot(h_s[...], wup_ref[...]), 0.0)
    acc_s[...] += _dot((u * u).astype(BF16), wdn_ref[...])

    @pl.when(j == pl.num_programs(1) - 1)
    def _():
        o_ref[...] = o_ref[...] + _rms(acc_s[...], gpost_ref[...])


def _mix_mlp(x2d, oseq, qm, kv, w_out, g_mix, g_pre, w_up, w_dn, g_post, s, n_mem, tm, tf):
    m = x2d.shape[0]
    nt = s // tm
    row = lambda i, j: (i, 0)
    fixed = lambda i, j: (0, 0)
    vec = lambda g: g.reshape(1, D_MODEL)
    return pl.pallas_call(
        _mix_mlp_kernel, grid=(m // tm, D_FF // tf),
        in_specs=[
            pl.BlockSpec((tm, D_MODEL), row),
            pl.BlockSpec((tm, SEQ_WIDTH), row),
            pl.BlockSpec((tm, MEM_WIDTH), row),
            pl.BlockSpec((n_mem, MEM_WIDTH), lambda i, j: (i // nt, 0)),
            pl.BlockSpec((n_mem, MEM_WIDTH), lambda i, j: (i // nt, 1)),
            pl.BlockSpec((D_MODEL, D_MODEL), fixed),
            pl.BlockSpec((1, D_MODEL), fixed),
            pl.BlockSpec((1, D_MODEL), fixed),
            pl.BlockSpec((D_MODEL, tf), lambda i, j: (0, j)),
            pl.BlockSpec((tf, D_MODEL), lambda i, j: (j, 0)),
            pl.BlockSpec((1, D_MODEL), fixed),
        ],
        out_specs=pl.BlockSpec((tm, D_MODEL), row),
        out_shape=jax.ShapeDtypeStruct(x2d.shape, F32),
        scratch_shapes=[pltpu.VMEM((tm, D_MODEL), BF16), pltpu.VMEM((tm, D_MODEL), F32)],
        compiler_params=_params(("parallel", "arbitrary")), name="mix_mlp",
    )(x2d, oseq, qm, kv, kv, w_out, vec(g_mix), vec(g_pre), w_up, w_dn, vec(g_post))


def _rope_tables(positions):
    half = HEAD_DIM // 2
    inv = ROPE_THETA ** (-jnp.arange(0, HEAD_DIM, 2, dtype=F32) / HEAD_DIM)
    ang = positions.astype(F32).reshape(-1, 1) * inv
    cos, sin = jnp.cos(ang), jnp.sin(ang)
    reps = LANES // HEAD_DIM
    cos_t = jnp.tile(cos, (1, 2 * reps))
    sin_t = jnp.tile(jnp.concatenate([-sin, sin], axis=1), (1, reps))
    assert cos_t.shape[1] == LANES and half * 2 == HEAD_DIM
    return cos_t, sin_t


def _visibility(chunk_id, tq, tk):
    b, s = chunk_id.shape
    cq = chunk_id.reshape(b, s // tq, tq)
    ck = chunk_id.reshape(b, s // tk, tk)
    qmin, qmax = cq.min(-1)[:, :, None], cq.max(-1)[:, :, None]
    kmin, kmax = ck.min(-1)[:, None, :], ck.max(-1)[:, None, :]
    status = jnp.where(kmin > qmax, 0, jnp.where(kmax <= qmin, 1, 2)).astype(I32)
    tiles = jnp.arange(s // tk, dtype=I32)[None, None, :]
    nhi = jnp.max(jnp.where(status != 0, tiles + 1, 0), axis=-1).astype(I32)
    return status.reshape(-1), nhi.reshape(-1)


def _dup(w):
    return jnp.concatenate([w, w], axis=1)


def kernel(x, mem, positions, g_pre_mix, g_post_mix, g_mem, w_mem_kv, w_out, g_pre_mlp, g_post_mlp,
           w_mlp_up, w_mlp_down, w_in_diff, lambda_q1, lambda_k1, lambda_q2, lambda_k2, g_diff_subln,
           w_in_dsa):
    b, s, d = x.shape
    n_mem = mem.shape[1]
    depth = g_pre_mix.shape[0]
    m = b * s
    scale = HEAD_DIM ** -0.5
    scale2 = scale * math.log2(math.e)
    tm_proj = min(512, s)
    tq_diff, tk_diff = min(512, s), min(1024, s)
    tq_dsa, tk_dsa = min(256, s), min(512, s)
    tm_mlp, tf_mlp = min(1024, s), 1024
    top_k = min(DSA_TOPK_MAX, s // 4)
    assert tk_dsa >= top_k, "the threshold search needs one key tile to hold top_k candidates"

    cos_t, sin_t = _rope_tables(positions)
    chunk_id = positions // CHUNK
    cq3 = chunk_id.reshape(-1, 1)
    ck3 = chunk_id.reshape(b, 1, s)
    vis_diff = _visibility(chunk_id, tq_diff, tk_diff)
    vis_dsa = _visibility(chunk_id, tq_dsa, tk_dsa)

    x2d = x.reshape(m, d)
    mem2d = mem.reshape(b * n_mem, d)
    for i in range(depth):
        j = i // 2
        kv = _inproj(mem2d, g_mem[i], None, None, None, w_mem_kv[i].astype(BF16), (),
                     ((2 * MEM_WIDTH, 1.0, BF16),), min(256, b * n_mem))[0]
        if i % 2 == 0:
            w = w_in_diff[j].astype(BF16)
            nqk = 2 * DIFF_HEADS * HEAD_DIM
            q, k, v, qm = _inproj(
                x2d, g_pre_mix[i], cos_t, sin_t, w[:, :2 * nqk], w[:, 2 * nqk:],
                ((nqk, scale2, BF16), (nqk, 1.0, BF16)),
                ((SEQ_WIDTH, 1.0, BF16), (MEM_WIDTH, scale, BF16)), tm_proj)
            lam_p = jnp.stack([lambda_q1[j], lambda_k1[j], lambda_q2[j], lambda_k2[j]]).astype(F32)
            lam_init = 0.8 - 0.6 * math.exp(-0.3 * i)
            o_seq = _diff_attention(
                q, k, v, cq3.reshape(m // tq_diff, tq_diff, 1), ck3, vis_diff[0], vis_diff[1],
                lam_p, g_diff_subln[j].reshape(1, LANES), b, s, tq_diff, tk_diff, lam_init)
        else:
            w = w_in_dsa[j]
            o0 = 0
            parts = []
            for width in (DSA_HEADS * HEAD_DIM, HEAD_DIM, HEAD_DIM, IDX_HEADS * IDX_DIM, IDX_DIM,
                          IDX_HEADS, MEM_WIDTH):
                parts.append(w[:, o0:o0 + width])
                o0 += width
            wq, wk, wv, wiq, wik, wiw, wqm = parts
            wiw = jnp.pad(wiw, ((0, 0), (0, LANES - IDX_HEADS)))
            w_rope = jnp.concatenate([wq, wiq, _dup(wk), _dup(wik)], axis=1).astype(BF16)
            w_plain = jnp.concatenate([_dup(wv), wiw, wqm], axis=1).astype(BF16)
            iw_scale = IDX_HEADS ** -0.5 * IDX_DIM ** -0.5
            q, iq, kk, ikk, vv, iw, qm = _inproj(
                x2d, g_pre_mix[i], cos_t, sin_t, w_rope, w_plain,
                ((DSA_HEADS * HEAD_DIM, scale2, BF16), (IDX_HEADS * IDX_DIM, 1.0, BF16),
                 (LANES, 1.0, BF16), (LANES, 1.0, BF16)),
                ((LANES, 1.0, BF16), (LANES, iw_scale, F32), (MEM_WIDTH, scale, BF16)), tm_proj)
            o_seq = _dsa_attention(
                q, iq, iw, kk, vv, ikk, cq3.reshape(m // tq_dsa, tq_dsa, 1), ck3,
                vis_dsa[0], vis_dsa[1], b, s, tq_dsa, tk_dsa, top_k)
        x2d = _mix_mlp(x2d, o_seq, qm, kv, w_out[i].astype(BF16), g_post_mix[i], g_pre_mlp[i],
                       w_mlp_up[i].astype(BF16), w_mlp_down[i].astype(BF16), g_post_mlp[i],
                       s, n_mem, tm_mlp, tf_mlp)
    return x2d.reshape(b, s, d)
```

```python
import functools
import math

import jax
import jax.numpy as jnp
from jax import lax
from jax.experimental import pallas as pl
from jax.experimental.pallas import tpu as pltpu

D_MODEL = 1024
CHUNK = 64
HEAD_DIM = 64
LANES = 128
ROPE_THETA = 10000.0
EPS = 1e-6
MEM_HEADS = 4
MEM_WIDTH = MEM_HEADS * HEAD_DIM
SEQ_WIDTH = D_MODEL - MEM_WIDTH
DIFF_HEADS = SEQ_WIDTH // (2 * HEAD_DIM)
DSA_HEADS = SEQ_WIDTH // HEAD_DIM
IDX_HEADS = 8
IDX_DIM = 64
DSA_TOPK_MAX = 256
D_FF = 4 * D_MODEL

NEG = -1e30
INT_MIN = -(2 ** 31)
MASKED_KEY = -2139095041
TOP_PER_LANE = 12
CAND_ROWS = 32
CAND_GROUP = 2
assert TOP_PER_LANE % CAND_GROUP == 0
VMEM_LIMIT = 56 * 1024 * 1024

F32 = jnp.float32
BF16 = jnp.bfloat16
I32 = jnp.int32


def _dot(a, b):
    return jnp.dot(a, b, preferred_element_type=F32)


def _dot_nt(a, b):
    return lax.dot_general(a, b, (((1,), (1,)), ((), ())), preferred_element_type=F32)


def _lane_tile(x, width):
    return jnp.concatenate([x] * (width // LANES), axis=1)


def _row_total(counts):
    total = jnp.sum(counts.astype(F32), axis=1, keepdims=True)
    return jnp.broadcast_to(total, counts.shape).astype(I32)


def _key_to_f32(key):
    return lax.bitcast_convert_type(jnp.where(key < 0, key ^ 0x7FFFFFFF, key), F32)


def _rms(x, g):
    return x * lax.rsqrt(jnp.mean(x * x, axis=-1, keepdims=True) + EPS) * g


def _params(sem):
    return pltpu.CompilerParams(dimension_semantics=sem, vmem_limit_bytes=VMEM_LIMIT)


def _inproj_kernel(*refs, rope_outs, plain_outs):
    n_r, n_p = len(rope_outs), len(plain_outs)
    x_ref, g_ref = refs[0], refs[1]
    pos = 2
    if n_r:
        cos_ref, sin_ref, wr_ref = refs[2], refs[3], refs[4]
        pos = 5
    if n_p:
        wp_ref = refs[pos]
        pos += 1
    out_refs = refs[pos:]
    x = x_ref[...]
    h = _rms(x, g_ref[...]).astype(BF16)
    oi = 0
    if n_r:
        cos = cos_ref[...]
        sin = sin_ref[...]
        lane = lax.broadcasted_iota(I32, cos.shape, 1)
        low = (lane % HEAD_DIM) < (HEAD_DIM // 2)
        c0 = 0
        for width, scale in rope_outs:
            o_ref = out_refs[oi]
            oi += 1
            y_all = _dot(h, wr_ref[:, c0:c0 + width])
            for s in range(width // LANES):
                y = y_all[:, s * LANES:(s + 1) * LANES]
                swapped = jnp.where(low, pltpu.roll(y, LANES - HEAD_DIM // 2, 1),
                                    pltpu.roll(y, HEAD_DIM // 2, 1))
                r = y * cos + swapped * sin
                if scale != 1.0:
                    r = r * scale
                o_ref[:, s * LANES:(s + 1) * LANES] = r.astype(o_ref.dtype)
            c0 += width
    c0 = 0
    for width, scale in plain_outs:
        o_ref = out_refs[oi]
        oi += 1
        y = _dot(h, wp_ref[:, c0:c0 + width])
        if scale != 1.0:
            y = y * scale
        o_ref[...] = y.astype(o_ref.dtype)
        c0 += width


def _inproj(x2d, g, cos, sin, w_rope, w_plain, rope_outs, plain_outs, tm):
    m = x2d.shape[0]
    row = lambda i: (i, 0)
    fixed = lambda i: (0, 0)
    in_specs = [pl.BlockSpec((tm, D_MODEL), row), pl.BlockSpec((1, D_MODEL), fixed)]
    args = [x2d, g.reshape(1, D_MODEL)]
    if rope_outs:
        in_specs += [pl.BlockSpec((tm, LANES), row), pl.BlockSpec((tm, LANES), row),
                     pl.BlockSpec(w_rope.shape, fixed)]
        args += [cos, sin, w_rope]
    if plain_outs:
        in_specs += [pl.BlockSpec(w_plain.shape, fixed)]
        args += [w_plain]
    outs = tuple(rope_outs) + tuple(plain_outs)
    out_shape = [jax.ShapeDtypeStruct((m, w), dt) for w, _, dt in outs]
    out_specs = [pl.BlockSpec((tm, w), row) for w, _, _ in outs]
    kern = functools.partial(_inproj_kernel,
                             rope_outs=tuple((w, s) for w, s, _ in rope_outs),
                             plain_outs=tuple((w, s) for w, s, _ in plain_outs))
    return pl.pallas_call(
        kern, grid=(m // tm,), in_specs=in_specs, out_specs=out_specs, out_shape=out_shape,
        compiler_params=_params(("parallel",)), name="inproj")(*args)


def _diff_attn_kernel(status_ref, nhi_ref, q_ref, k_ref, v_ref, cq_ref, ck_ref, lam_ref, gsub_ref,
                      o_ref, m_s, acc_s, s0_s, s1_s, *, tq, tk, nq, nk, lam_init):
    b = pl.program_id(0)
    lane = lax.broadcasted_iota(I32, (tq, LANES), 1)
    lam_p = lam_ref[...]
    lam = (jnp.exp(jnp.sum(lam_p[0:1] * lam_p[1:2], axis=1, keepdims=True))
           - jnp.exp(jnp.sum(lam_p[2:3] * lam_p[3:4], axis=1, keepdims=True)) + lam_init)

    def logits(blk, t, dst_s):
        q = q_ref[pl.ds(pl.multiple_of(blk * tq, tq), tq), :]
        zero = jnp.zeros_like(q)
        kt = k_ref[pl.ds(pl.multiple_of(t * tk, tk), tk), :]
        dst_s[0] = _dot_nt(jnp.where(lane < HEAD_DIM, q, zero), kt)
        dst_s[1] = _dot_nt(jnp.where(lane >= HEAD_DIM, q, zero), kt)

    def tile(blk, t, masked, cur_s, nxt):
        if nxt is not None:
            logits(*nxt)
        off = pl.multiple_of(t * tk, tk)
        vt = jnp.concatenate([v_ref[pl.ds(off, tk), :], jnp.ones((tk, LANES), BF16)], axis=1)
        if masked:
            ck = ck_ref[0, :, pl.ds(off, tk)]
            bias = jnp.where(cq_ref[blk] >= ck, 0.0, NEG)
        for j in range(2):
            s = cur_s[j] + bias if masked else cur_s[j]
            m_old = m_s[j]
            m_new = jnp.maximum(m_old, jnp.max(s, axis=1, keepdims=True))
            p = jnp.exp2(s - _lane_tile(m_new, tk))
            alpha = jnp.exp2(m_old - m_new)
            acc_s[j] = _lane_tile(alpha, 2 * LANES) * acc_s[j] + _dot(p.astype(BF16), vt)
            m_s[j] = m_new

    def step(blk, n_hi, first, g, cur_s, nxt_s):
        t = n_hi - 1 - (g - first)
        live = jnp.logical_and(g >= first, t >= 0)
        st = status_ref[(b * nq + blk) * nk + jnp.clip(t, 0, nk - 1)]
        here = t >= 1
        n_next = nhi_ref[b * nq + jnp.minimum(blk + 1, nq - 1)]
        nxt = (jnp.where(here, blk, blk + 1), jnp.where(here, t - 1, n_next - 1), nxt_s)
        some = jnp.logical_or(here, blk + 1 < nq)

        @pl.when(jnp.logical_and(jnp.logical_and(live, st == 0), some))
        def _():
            logits(*nxt)

        for code, masked in ((1, False), (2, True)):
            hit = jnp.logical_and(live, st == code)

            @pl.when(jnp.logical_and(hit, some))
            def _():
                tile(blk, t, masked, cur_s, nxt)

            @pl.when(jnp.logical_and(hit, jnp.logical_not(some)))
            def _():
                tile(blk, t, masked, cur_s, None)

    def block(blk, first):
        n_hi = nhi_ref[b * nq + blk]
        m_s[...] = jnp.full(m_s.shape, NEG, F32)
        acc_s[...] = jnp.zeros(acc_s.shape, F32)

        def pair(u, carry):
            step(blk, n_hi, first, 2 * u, s0_s, s1_s)
            step(blk, n_hi, first, 2 * u + 1, s1_s, s0_s)
            return carry

        lax.fori_loop(first // 2, (first + n_hi + 1) // 2, pair, 0)
        o = (acc_s[0, :, :LANES] / acc_s[0, :, LANES:]
             - lam * (acc_s[1, :, :LANES] / acc_s[1, :, LANES:]))
        o = _rms(o, gsub_ref[...]) * (1.0 - lam_init)
        o_ref[pl.ds(pl.multiple_of(blk * tq, tq), tq), :] = o.astype(o_ref.dtype)
        return first + n_hi

    logits(0, nhi_ref[b * nq] - 1, s0_s)
    lax.fori_loop(0, nq, block, jnp.int32(0))


def _diff_attention(q, k, v, cq3, ck3, status, nhi, lam_p, gsub, b, s, tq, tk, lam_init):
    nq, nk = s // tq, s // tk
    kern = functools.partial(_diff_attn_kernel, tq=tq, tk=tk, nq=nq, nk=nk, lam_init=lam_init)
    per_head = lambda bb, h, *_: (bb, h)
    grid_spec = pltpu.PrefetchScalarGridSpec(
        num_scalar_prefetch=2,
        grid=(b, DIFF_HEADS),
        in_specs=[
            pl.BlockSpec((s, LANES), per_head),
            pl.BlockSpec((s, LANES), per_head),
            pl.BlockSpec((s, LANES), per_head),
            pl.BlockSpec((nq, tq, 1), lambda bb, h, *_: (bb, 0, 0)),
            pl.BlockSpec((1, 1, s), lambda bb, h, *_: (bb, 0, 0)),
            pl.BlockSpec((4, HEAD_DIM), lambda bb, h, *_: (0, 0)),
            pl.BlockSpec((1, LANES), lambda bb, h, *_: (0, 0)),
        ],
        out_specs=pl.BlockSpec((s, LANES), per_head),
        scratch_shapes=[pltpu.VMEM((2, tq, LANES), F32),
                        pltpu.VMEM((2, tq, 2 * LANES), F32),
                        pltpu.VMEM((2, tq, tk), F32),
                        pltpu.VMEM((2, tq, tk), F32)],
    )
    return pl.pallas_call(
        kern, grid_spec=grid_spec,
        out_shape=jax.ShapeDtypeStruct((b * s, SEQ_WIDTH), BF16),
        compiler_params=_params(("parallel", "parallel")), name="diff_attn",
    )(status, nhi, q, k, v, cq3, ck3, lam_p, gsub)


def _dsa_kernel(status_ref, nhi_ref, q_ref, iq_ref, iw_ref, kk_ref, vv_ref, ikk_ref, cq_ref, ck_ref,
                o_ref, keys_s, cand_s, qst_s, iqst_s, iwb_s, m_s, acc_s, thf_s, jst_s, floor_s, nall_s, ncand_s,
                *, tq, tk, nq, nk, s_len, top_k):
    b = pl.program_id(0)
    i = pl.program_id(1)
    base = (b * nq + i) * nk
    n_hi = nhi_ref[b * nq + i]
    cq = cq_ref[0]
    lane = lax.broadcasted_iota(I32, (tq, LANES), 1)
    first = lane < HEAD_DIM
    spt = tk // LANES

    for h in range(DSA_HEADS):
        slab = q_ref[:, (h // 2) * LANES:(h // 2 + 1) * LANES]
        keep = first if h % 2 == 0 else jnp.logical_not(first)
        qst_s[h * tq:(h + 1) * tq, :] = jnp.where(keep, slab, jnp.zeros_like(slab))
    for h in range(IDX_HEADS):
        slab = iq_ref[:, (h // 2) * LANES:(h // 2 + 1) * LANES]
        keep = first if h % 2 == 0 else jnp.logical_not(first)
        iqst_s[h * tq:(h + 1) * tq, :] = jnp.where(keep, slab, jnp.zeros_like(slab))
        iwb_s[h * tq:(h + 1) * tq, :] = jnp.broadcast_to(iw_ref[:, h:h + 1], (tq, LANES))

    cand_s[...] = jnp.full(cand_s.shape, -jnp.inf, F32)

    def fold_candidates(sc):
        for g in range(tq // CAND_ROWS):
            rows = slice(g * CAND_ROWS, (g + 1) * CAND_ROWS)
            tops = [cand_s[r, rows, :] for r in range(TOP_PER_LANE)]
            for c in range(tk // LANES):
                x = sc[rows, c * LANES:(c + 1) * LANES]
                for r in range(TOP_PER_LANE):
                    tops[r], x = jnp.maximum(tops[r], x), jnp.minimum(tops[r], x)
            for r in range(TOP_PER_LANE):
                cand_s[r, rows, :] = tops[r]

    def score_body(t, carry):
        st = status_ref[base + t]
        off = pl.multiple_of(t * tk, tk)

        @pl.when(st == 0)
        def _():
            for c in range(spt):
                keys_s[t * spt + c] = jnp.full((tq, LANES), -jnp.inf, F32)

        def scores(masked):
            ikt = ikk_ref[pl.ds(off, tk), :]
            logits = _dot_nt(iqst_s[...], ikt)
            r = jnp.maximum(logits, 0.0) * _lane_tile(iwb_s[...], tk)
            sc = r[0:tq]
            for h in range(1, IDX_HEADS):
                sc = sc + r[h * tq:(h + 1) * tq]
            sc = jnp.where(sc == 0.0, 0.0, sc)
            if masked:
                ck = ck_ref[0, :, pl.ds(off, tk)]
                sc = jnp.where(cq >= ck, sc, -jnp.inf)
            for c in range(spt):
                keys_s[t * spt + c] = sc[:, c * LANES:(c + 1) * LANES]
            fold_candidates(sc)

        @pl.when(st == 1)
        def _():
            scores(False)

        @pl.when(st == 2)
        def _():
            scores(True)

        return carry

    lax.fori_loop(0, n_hi, score_body, 0)

    rg = min(tq, LANES)
    lane_rg = lax.broadcasted_iota(I32, (rg, LANES), 1)

    def count_tiles(preds, *operands):
        zero = jnp.zeros((rg, LANES), I32)
        accs = []
        for r in range(tq // rg):
            rows = slice(r * rg, (r + 1) * rg)
            ops = [o[rows] for o in operands]

            def body(t, acc, rows=rows, ops=ops):
                off = pl.multiple_of(t * tk, tk)
                acc = list(acc)
                for c in range(spt):
                    slab = keys_s[t * spt + c, rows, :]
                    for n, pred in enumerate(preds):
                        acc[n] = acc[n] + jnp.where(pred(slab, off + c * LANES, *ops), 1, 0).astype(I32)
                return tuple(acc)
            accs.append(lax.fori_loop(0, n_hi, body, (zero,) * len(preds)))
        outs = []
        for n in range(len(preds)):
            acc = accs[0][n] if len(accs) == 1 else jnp.concatenate([a[n] for a in accs], axis=0)
            outs.append(_row_total(acc))
        return outs

    ge = lambda slab, col0, c: slab >= c
    gt = lambda slab, col0, c: slab > c
    eq = lambda slab, col0, c: slab == c

    def count_cand(rows, c):
        n_groups = TOP_PER_LANE // CAND_GROUP
        full = [cand_s[CAND_GROUP * i + CAND_GROUP - 1, rows, :] >= c for i in range(n_groups)]
        base = jnp.full(c.shape, TOP_PER_LANE, I32)
        rest = [jnp.full(c.shape, -jnp.inf, F32)] * (CAND_GROUP - 1)
        for i in reversed(range(n_groups)):
            base = jnp.where(full[i], base, CAND_GROUP * i)
            rest = [jnp.where(full[i], rest[k], cand_s[CAND_GROUP * i + k, rows, :])
                    for k in range(CAND_GROUP - 1)]
        for x in rest:
            base = base + jnp.where(x >= c, 1, 0).astype(I32)
        return base

    def bisect(count):
        def bit_body(j, thr):
            cand = thr + lax.shift_left(jnp.int32(1), 31 - j)
            cnt, = count([ge], _key_to_f32(cand))
            return jnp.where(cnt >= top_k, cand, thr)
        return lax.fori_loop(0, 32, bit_body, jnp.full((tq, LANES), INT_MIN, I32))

    def bisect_candidates():
        half = tq // 2
        halves = (slice(0, half), slice(half, tq))

        def raw_count(rows, key):
            return count_cand(rows, _key_to_f32(key))

        def decide(acc, cand, thr):
            return jnp.where(_row_total(acc) >= top_k, cand, thr)

        bit = lambda j: lax.shift_left(jnp.int32(1), 31 - j)
        start = jnp.full((half, LANES), INT_MIN, I32)

        def body(j, carry):
            thr_a, cand_a, acc_a, thr_b = carry
            cand_b = thr_b + bit(j)
            acc_b = raw_count(halves[1], cand_b)
            thr_a = decide(acc_a, cand_a, thr_a)
            cand_a = thr_a + bit(j + 1)
            acc_a = raw_count(halves[0], cand_a)
            thr_b = decide(acc_b, cand_b, thr_b)
            return thr_a, cand_a, acc_a, thr_b

        cand_a = start + bit(0)
        carry = (start, cand_a, raw_count(halves[0], cand_a), start)
        thr_a, cand_a, acc_a, thr_b = lax.fori_loop(0, 31, body, carry)
        thr_a = decide(acc_a, cand_a, thr_a)
        cand_b = thr_b + bit(31)
        thr_b = decide(raw_count(halves[1], cand_b), cand_b, thr_b)
        return jnp.concatenate([thr_a, thr_b], axis=0)

    lane_k = lax.broadcasted_iota(I32, (tq, tk), 1)
    lane_v = lax.broadcasted_iota(I32, (tk, LANES), 1)

    def prepare(thr_key, cnt_gt, cnt_eq):
        real = thr_key > MASKED_KEY
        thr = jnp.where(real, _key_to_f32(thr_key), -jnp.inf)
        thf_s[...] = thr
        need = top_k - cnt_gt
        excess = jnp.logical_and(real, cnt_eq > need)
        jst_s[...] = jnp.where(real, jnp.int32(s_len), jnp.int32(-1))

        @pl.when(jnp.max(excess.astype(I32)) > 0)
        def _():
            def idx_body(j, jcur):
                cand = jcur + lax.shift_left(jnp.int32(1), (s_len.bit_length() - 2) - j)
                cnt, = count_tiles(
                    [lambda slab, col0, th, c: jnp.logical_and(slab == th, (col0 + lane_rg) < c)], thr, cand)
                return jnp.where(cnt < need, cand, jcur)
            jbest = lax.fori_loop(0, s_len.bit_length() - 1, idx_body, jnp.zeros((tq, LANES), I32))
            jst_s[...] = jnp.where(excess, jbest, jst_s[...])

    def attend():
        m_s[...] = jnp.full(m_s.shape, NEG, F32)
        acc_s[...] = jnp.zeros(acc_s.shape, F32)
        nall_s[...] = jnp.zeros(nall_s.shape, I32)

        def attn_body(t, carry):
            st = status_ref[base + t]

            @pl.when(st != 0)
            def _():
                off = pl.multiple_of(t * tk, tk)
                slabs = [keys_s[t * spt + c] for c in range(spt)]
                seen = nall_s[...]
                for slab in slabs:
                    seen = seen + jnp.where(slab >= floor_s[...], 1, 0).astype(I32)
                nall_s[...] = seen
                key = jnp.concatenate(slabs, axis=1)
                th = _lane_tile(thf_s[...], tk)
                jst = _lane_tile(jst_s[...], tk)
                sel = jnp.logical_or(key > th, jnp.logical_and(key == th, (lane_k + off) <= jst))
                bias = jnp.where(sel, 0.0, NEG)
                kt = kk_ref[pl.ds(off, tk), :]
                vt = vv_ref[pl.ds(off, tk), :]
                vt = jnp.where(lane_v < HEAD_DIM, vt, jnp.ones_like(vt))
                s = _dot_nt(qst_s[...], kt)
                s = (s.reshape(DSA_HEADS, tq, tk) + bias[None]).reshape(DSA_HEADS * tq, tk)
                m_old = m_s[...]
                m_new = jnp.maximum(m_old, jnp.max(s, axis=1, keepdims=True))
                p = jnp.exp2(s - _lane_tile(m_new, tk))
                alpha = jnp.exp2(m_old - m_new)
                acc_s[...] = alpha * acc_s[...] + _dot(p.astype(BF16), vt)
                m_s[...] = m_new

            return carry

        lax.fori_loop(0, n_hi, attn_body, 0)

    thr_key = bisect_candidates()
    everything = slice(0, tq)
    cnt_gt = _row_total(count_cand(everything, _key_to_f32(thr_key + 1)))
    cnt_eq = _row_total(count_cand(everything, _key_to_f32(thr_key))) - cnt_gt
    floor = _key_to_f32(jnp.maximum(thr_key, MASKED_KEY + 1))
    floor_s[...] = floor
    ncand_s[...] = _row_total(count_cand(everything, floor))
    prepare(thr_key, cnt_gt, cnt_eq)
    attend()

    @pl.when(jnp.max((_row_total(nall_s[...]) != ncand_s[...]).astype(I32)) > 0)
    def _():
        thr_all = bisect(count_tiles)
        thr_f = jnp.where(thr_all > MASKED_KEY, _key_to_f32(thr_all), -jnp.inf)
        gt_all, eq_all = count_tiles([gt, eq], thr_f)
        prepare(thr_all, gt_all, eq_all)
        attend()

    for j in range(DSA_HEADS // 2):
        acc_a = acc_s[2 * j * tq:(2 * j + 1) * tq]
        acc_b = acc_s[(2 * j + 1) * tq:(2 * j + 2) * tq]
        swap_a = pltpu.roll(acc_a, HEAD_DIM, 1)
        swap_b = pltpu.roll(acc_b, HEAD_DIM, 1)
        o_ref[:, j * LANES:(j + 1) * LANES] = jnp.where(first, acc_a / swap_a,
                                                        swap_b / acc_b).astype(o_ref.dtype)


def _dsa_attention(q, iq, iw, kk, vv, ikk, cq3, ck3, status, nhi, b, s, tq, tk, top_k):
    nq, nk = s // tq, s // tk
    kern = functools.partial(_dsa_kernel, tq=tq, tk=tk, nq=nq, nk=nk, s_len=s, top_k=top_k)
    rowblk = lambda bb, i, *_: (bb * nq + i, 0)
    perb = lambda bb, i, *_: (bb, 0)
    grid_spec = pltpu.PrefetchScalarGridSpec(
        num_scalar_prefetch=2,
        grid=(b, nq),
        in_specs=[
            pl.BlockSpec((tq, SEQ_WIDTH), rowblk),
            pl.BlockSpec((tq, IDX_HEADS * IDX_DIM), rowblk),
            pl.BlockSpec((tq, LANES), rowblk),
            pl.BlockSpec((s, LANES), perb),
            pl.BlockSpec((s, LANES), perb),
            pl.BlockSpec((s, LANES), perb),
            pl.BlockSpec((1, tq, 1), lambda bb, i, *_: (bb * nq + i, 0, 0)),
            pl.BlockSpec((1, 1, s), lambda bb, i, *_: (bb, 0, 0)),
        ],
        out_specs=pl.BlockSpec((tq, SEQ_WIDTH), rowblk),
        scratch_shapes=[
            pltpu.VMEM((s // LANES, tq, LANES), F32),
            pltpu.VMEM((TOP_PER_LANE, tq, LANES), F32),
            pltpu.VMEM((DSA_HEADS * tq, LANES), BF16),
            pltpu.VMEM((IDX_HEADS * tq, LANES), BF16),
            pltpu.VMEM((IDX_HEADS * tq, LANES), F32),
            pltpu.VMEM((DSA_HEADS * tq, LANES), F32),
            pltpu.VMEM((DSA_HEADS * tq, LANES), F32),
            pltpu.VMEM((tq, LANES), F32),
            pltpu.VMEM((tq, LANES), I32),
            pltpu.VMEM((tq, LANES), F32),
            pltpu.VMEM((tq, LANES), I32),
            pltpu.VMEM((tq, LANES), I32),
        ],
    )
    return pl.pallas_call(
        kern, grid_spec=grid_spec,
        out_shape=jax.ShapeDtypeStruct((b * s, SEQ_WIDTH), BF16),
        compiler_params=_params(("parallel", "arbitrary")), name="dsa_attn",
    )(status, nhi, q, iq, iw, kk, vv, ikk, cq3, ck3)


def _mix_mlp_kernel(x_ref, oseq_ref, qm_ref, km_ref, vm_ref, wo_ref, gmix_ref, gpre_ref, wup_ref, wdn_ref,
                    gpost_ref, o_ref, h_s, acc_s):
    j = pl.program_id(1)

    @pl.when(j == 0)
    def _():
        qm = qm_ref[...]
        km = km_ref[...]
        vm = vm_ref[...]
        lane = lax.broadcasted_iota(I32, qm.shape, 1)
        o_mem = jnp.zeros(qm.shape, F32)
        for h in range(MEM_HEADS):
            mine = (lane // HEAD_DIM) == h
            s = _dot_nt(jnp.where(mine, qm, jnp.zeros_like(qm)), km)
            p = jnp.exp(s - jnp.max(s, axis=1, keepdims=True))
            p = p / jnp.sum(p, axis=1, keepdims=True)
            o_mem = o_mem + jnp.where(mine, _dot(p.astype(BF16), vm), 0.0)
        y = (_dot(oseq_ref[...], wo_ref[0:SEQ_WIDTH, :])
             + _dot(o_mem.astype(BF16), wo_ref[SEQ_WIDTH:, :]))
        x_mid = x_ref[...] + _rms(y, gmix_ref[...])
        o_ref[...] = x_mid
        h_s[...] = _rms(x_mid, gpre_ref[...]).astype(BF16)
        acc_s[...] = jnp.zeros(acc_s.shape, F32)

    u = jnp.maximum(_dot(h_s[...], wup_ref[...]), 0.0)
    acc_s[...] += _dot((u * u).astype(BF16), wdn_ref[...])

    @pl.when(j == pl.num_programs(1) - 1)
    def _():
        o_ref[...] = o_ref[...] + _rms(acc_s[...], gpost_ref[...])


def _mix_mlp(x2d, oseq, qm, kv, w_out, g_mix, g_pre, w_up, w_dn, g_post, s, n_mem, tm, tf):
    m = x2d.shape[0]
    nt = s // tm
    row = lambda i, j: (i, 0)
    fixed = lambda i, j: (0, 0)
    vec = lambda g: g.reshape(1, D_MODEL)
    return pl.pallas_call(
        _mix_mlp_kernel, grid=(m // tm, D_FF // tf),
        in_specs=[
            pl.BlockSpec((tm, D_MODEL), row),
            pl.BlockSpec((tm, SEQ_WIDTH), row),
            pl.BlockSpec((tm, MEM_WIDTH), row),
            pl.BlockSpec((n_mem, MEM_WIDTH), lambda i, j: (i // nt, 0)),
            pl.BlockSpec((n_mem, MEM_WIDTH), lambda i, j: (i // nt, 1)),
            pl.BlockSpec((D_MODEL, D_MODEL), fixed),
            pl.BlockSpec((1, D_MODEL), fixed),
            pl.BlockSpec((1, D_MODEL), fixed),
            pl.BlockSpec((D_MODEL, tf), lambda i, j: (0, j)),
            pl.BlockSpec((tf, D_MODEL), lambda i, j: (j, 0)),
            pl.BlockSpec((1, D_MODEL), fixed),
        ],
        out_specs=pl.BlockSpec((tm, D_MODEL), row),
        out_shape=jax.ShapeDtypeStruct(x2d.shape, F32),
        scratch_shapes=[pltpu.VMEM((tm, D_MODEL), BF16), pltpu.VMEM((tm, D_MODEL), F32)],
        compiler_params=_params(("parallel", "arbitrary")), name="mix_mlp",
    )(x2d, oseq, qm, kv, kv, w_out, vec(g_mix), vec(g_pre), w_up, w_dn, vec(g_post))


def _rope_tables(positions):
    half = HEAD_DIM // 2
    inv = ROPE_THETA ** (-jnp.arange(0, HEAD_DIM, 2, dtype=F32) / HEAD_DIM)
    ang = positions.astype(F32).reshape(-1, 1) * inv
    cos, sin = jnp.cos(ang), jnp.sin(ang)
    reps = LANES // HEAD_DIM
    cos_t = jnp.tile(cos, (1, 2 * reps))
    sin_t = jnp.tile(jnp.concatenate([-sin, sin], axis=1), (1, reps))
    assert cos_t.shape[1] == LANES and half * 2 == HEAD_DIM
    return cos_t, sin_t


def _visibility(chunk_id, tq, tk):
    b, s = chunk_id.shape
    cq = chunk_id.reshape(b, s // tq, tq)
    ck = chunk_id.reshape(b, s // tk, tk)
    qmin, qmax = cq.min(-1)[:, :, None], cq.max(-1)[:, :, None]
    kmin, kmax = ck.min(-1)[:, None, :], ck.max(-1)[:, None, :]
    status = jnp.where(kmin > qmax, 0, jnp.where(kmax <= qmin, 1, 2)).astype(I32)
    tiles = jnp.arange(s // tk, dtype=I32)[None, None, :]
    nhi = jnp.max(jnp.where(status != 0, tiles + 1, 0), axis=-1).astype(I32)
    return status.reshape(-1), nhi.reshape(-1)


def _dup(w):
    return jnp.concatenate([w, w], axis=1)


def kernel(x, mem, positions, g_pre_mix, g_post_mix, g_mem, w_mem_kv, w_out, g_pre_mlp, g_post_mlp,
           w_mlp_up, w_mlp_down, w_in_diff, lambda_q1, lambda_k1, lambda_q2, lambda_k2, g_diff_subln,
           w_in_dsa):
    b, s, d = x.shape
    n_mem = mem.shape[1]
    depth = g_pre_mix.shape[0]
    m = b * s
    scale = HEAD_DIM ** -0.5
    scale2 = scale * math.log2(math.e)
    tm_proj = min(512, s)
    tq_diff, tk_diff = min(512, s), min(1024, s)
    tq_dsa, tk_dsa = min(256, s), min(512, s)
    tm_mlp, tf_mlp = min(1024, s), 1024
    top_k = min(DSA_TOPK_MAX, s // 4)
    assert tk_dsa >= top_k, "the threshold search needs one key tile to hold top_k candidates"

    cos_t, sin_t = _rope_tables(positions)
    chunk_id = positions // CHUNK
    cq3 = chunk_id.reshape(-1, 1)
    ck3 = chunk_id.reshape(b, 1, s)
    vis_diff = _visibility(chunk_id, tq_diff, tk_diff)
    vis_dsa = _visibility(chunk_id, tq_dsa, tk_dsa)

    x2d = x.reshape(m, d)
    mem2d = mem.reshape(b * n_mem, d)
    for i in range(depth):
        j = i // 2
        kv = _inproj(mem2d, g_mem[i], None, None, None, w_mem_kv[i].astype(BF16), (),
                     ((2 * MEM_WIDTH, 1.0, BF16),), min(256, b * n_mem))[0]
        if i % 2 == 0:
            w = w_in_diff[j].astype(BF16)
            nqk = 2 * DIFF_HEADS * HEAD_DIM
            q, k, v, qm = _inproj(
                x2d, g_pre_mix[i], cos_t, sin_t, w[:, :2 * nqk], w[:, 2 * nqk:],
                ((nqk, scale2, BF16), (nqk, 1.0, BF16)),
                ((SEQ_WIDTH, 1.0, BF16), (MEM_WIDTH, scale, BF16)), tm_proj)
            lam_p = jnp.stack([lambda_q1[j], lambda_k1[j], lambda_q2[j], lambda_k2[j]]).astype(F32)
            lam_init = 0.8 - 0.6 * math.exp(-0.3 * i)
            o_seq = _diff_attention(
                q, k, v, cq3.reshape(m // tq_diff, tq_diff, 1), ck3, vis_diff[0], vis_diff[1],
                lam_p, g_diff_subln[j].reshape(1, LANES), b, s, tq_diff, tk_diff, lam_init)
        else:
            w = w_in_dsa[j]
            o0 = 0
            parts = []
            for width in (DSA_HEADS * HEAD_DIM, HEAD_DIM, HEAD_DIM, IDX_HEADS * IDX_DIM, IDX_DIM,
                          IDX_HEADS, MEM_WIDTH):
                parts.append(w[:, o0:o0 + width])
                o0 += width
            wq, wk, wv, wiq, wik, wiw, wqm = parts
            wiw = jnp.pad(wiw, ((0, 0), (0, LANES - IDX_HEADS)))
            w_rope = jnp.concatenate([wq, wiq, _dup(wk), _dup(wik)], axis=1).astype(BF16)
            w_plain = jnp.concatenate([_dup(wv), wiw, wqm], axis=1).astype(BF16)
            iw_scale = IDX_HEADS ** -0.5 * IDX_DIM ** -0.5
            q, iq, kk, ikk, vv, iw, qm = _inproj(
                x2d, g_pre_mix[i], cos_t, sin_t, w_rope, w_plain,
                ((DSA_HEADS * HEAD_DIM, scale2, BF16), (IDX_HEADS * IDX_DIM, 1.0, BF16),
                 (LANES, 1.0, BF16), (LANES, 1.0, BF16)),
                ((LANES, 1.0, BF16), (LANES, iw_scale, F32), (MEM_WIDTH, scale, BF16)), tm_proj)
            o_seq = _dsa_attention(
                q, iq, iw, kk, vv, ikk, cq3.reshape(m // tq_dsa, tq_dsa, 1), ck3,
                vis_dsa[0], vis_dsa[1], b, s, tq_dsa, tk_dsa, top_k)
        x2d = _mix_mlp(x2d, o_seq, qm, kv, w_out[i].astype(BF16), g_post_mix[i], g_pre_mlp[i],
                       w_mlp_up[i].astype(BF16), w_mlp_down[i].astype(BF16), g_post_mlp[i],
                       s, n_mem, tm_mlp, tf_mlp)
    return x2d.reshape(b, s, d)
```

```python
import functools
import math

import jax
import jax.numpy as jnp
from jax import lax
from jax.experimental import pallas as pl
from jax.experimental.pallas import tpu as pltpu

D_MODEL = 1024
CHUNK = 64
HEAD_DIM = 64
LANES = 128
ROPE_THETA = 10000.0
EPS = 1e-6
MEM_HEADS = 4
MEM_WIDTH = MEM_HEADS * HEAD_DIM
SEQ_WIDTH = D_MODEL - MEM_WIDTH
DIFF_HEADS = SEQ_WIDTH // (2 * HEAD_DIM)
DSA_HEADS = SEQ_WIDTH // HEAD_DIM
IDX_HEADS = 8
IDX_DIM = 64
DSA_TOPK_MAX = 256
D_FF = 4 * D_MODEL

NEG = -1e30
INT_MIN = -(2 ** 31)
MASKED_KEY = -2139095041
TOP_PER_LANE = 12
CAND_ROWS = 32
CAND_GROUP = 2
assert TOP_PER_LANE % CAND_GROUP == 0
VMEM_LIMIT = 56 * 1024 * 1024

F32 = jnp.float32
BF16 = jnp.bfloat16
I32 = jnp.int32


def _dot(a, b):
    return jnp.dot(a, b, preferred_element_type=F32)


def _dot_nt(a, b):
    return lax.dot_general(a, b, (((1,), (1,)), ((), ())), preferred_element_type=F32)


def _lane_tile(x, width):
    return jnp.concatenate([x] * (width // LANES), axis=1)


def _row_total(counts):
    total = jnp.sum(counts.astype(F32), axis=1, keepdims=True)
    return jnp.broadcast_to(total, counts.shape).astype(I32)


def _key_to_f32(key):
    return lax.bitcast_convert_type(jnp.where(key < 0, key ^ 0x7FFFFFFF, key), F32)


def _rms(x, g):
    return x * lax.rsqrt(jnp.mean(x * x, axis=-1, keepdims=True) + EPS) * g


def _params(sem):
    return pltpu.CompilerParams(dimension_semantics=sem, vmem_limit_bytes=VMEM_LIMIT)


def _inproj_kernel(*refs, rope_outs, plain_outs):
    n_r, n_p = len(rope_outs), len(plain_outs)
    x_ref, g_ref = refs[0], refs[1]
    pos = 2
    if n_r:
        cos_ref, sin_ref, wr_ref = refs[2], refs[3], refs[4]
        pos = 5
    if n_p:
        wp_ref = refs[pos]
        pos += 1
    out_refs = refs[pos:]
    x = x_ref[...]
    h = _rms(x, g_ref[...]).astype(BF16)
    oi = 0
    if n_r:
        cos = cos_ref[...]
        sin = sin_ref[...]
        lane = lax.broadcasted_iota(I32, cos.shape, 1)
        low = (lane % HEAD_DIM) < (HEAD_DIM // 2)
        c0 = 0
        for width, scale in rope_outs:
            o_ref = out_refs[oi]
            oi += 1
            y_all = _dot(h, wr_ref[:, c0:c0 + width])
            for s in range(width // LANES):
                y = y_all[:, s * LANES:(s + 1) * LANES]
                swapped = jnp.where(low, pltpu.roll(y, LANES - HEAD_DIM // 2, 1),
                                    pltpu.roll(y, HEAD_DIM // 2, 1))
                r = y * cos + swapped * sin
                if scale != 1.0:
                    r = r * scale
                o_ref[:, s * LANES:(s + 1) * LANES] = r.astype(o_ref.dtype)
            c0 += width
    c0 = 0
    for width, scale in plain_outs:
        o_ref = out_refs[oi]
        oi += 1
        y = _dot(h, wp_ref[:, c0:c0 + width])
        if scale != 1.0:
            y = y * scale
        o_ref[...] = y.astype(o_ref.dtype)
        c0 += width


def _inproj(x2d, g, cos, sin, w_rope, w_plain, rope_outs, plain_outs, tm):
    m = x2d.shape[0]
    row = lambda i: (i, 0)
    fixed = lambda i: (0, 0)
    in_specs = [pl.BlockSpec((tm, D_MODEL), row), pl.BlockSpec((1, D_MODEL), fixed)]
    args = [x2d, g.reshape(1, D_MODEL)]
    if rope_outs:
        in_specs += [pl.BlockSpec((tm, LANES), row), pl.BlockSpec((tm, LANES), row),
                     pl.BlockSpec(w_rope.shape, fixed)]
        args += [cos, sin, w_rope]
    if plain_outs:
        in_specs += [pl.BlockSpec(w_plain.shape, fixed)]
        args += [w_plain]
    outs = tuple(rope_outs) + tuple(plain_outs)
    out_shape = [jax.ShapeDtypeStruct((m, w), dt) for w, _, dt in outs]
    out_specs = [pl.BlockSpec((tm, w), row) for w, _, _ in outs]
    kern = functools.partial(_inproj_kernel,
                             rope_outs=tuple((w, s) for w, s, _ in rope_outs),
                             plain_outs=tuple((w, s) for w, s, _ in plain_outs))
    return pl.pallas_call(
        kern, grid=(m // tm,), in_specs=in_specs, out_specs=out_specs, out_shape=out_shape,
        compiler_params=_params(("parallel",)), name="inproj")(*args)


def _diff_attn_kernel(status_ref, nhi_ref, q_ref, k_ref, v_ref, cq_ref, ck_ref, lam_ref, gsub_ref,
                      o_ref, m_s, acc_s, s0_s, s1_s, *, tq, tk, nq, nk, lam_init):
    b = pl.program_id(0)
    lane = lax.broadcasted_iota(I32, (tq, LANES), 1)
    lam_p = lam_ref[...]
    lam = (jnp.exp(jnp.sum(lam_p[0:1] * lam_p[1:2], axis=1, keepdims=True))
           - jnp.exp(jnp.sum(lam_p[2:3] * lam_p[3:4], axis=1, keepdims=True)) + lam_init)

    def logits(blk, t, dst_s):
        q = q_ref[pl.ds(pl.multiple_of(blk * tq, tq), tq), :]
        zero = jnp.zeros_like(q)
        kt = k_ref[pl.ds(pl.multiple_of(t * tk, tk), tk), :]
        dst_s[0] = _dot_nt(jnp.where(lane < HEAD_DIM, q, zero), kt)
        dst_s[1] = _dot_nt(jnp.where(lane >= HEAD_DIM, q, zero), kt)

    def tile(blk, t, masked, cur_s, nxt):
        if nxt is not None:
            logits(*nxt)
        off = pl.multiple_of(t * tk, tk)
        vt = jnp.concatenate([v_ref[pl.ds(off, tk), :], jnp.ones((tk, LANES), BF16)], axis=1)
        if masked:
            ck = ck_ref[0, :, pl.ds(off, tk)]
            bias = jnp.where(cq_ref[blk] >= ck, 0.0, NEG)
        for j in range(2):
            s = cur_s[j] + bias if masked else cur_s[j]
            m_old = m_s[j]
            m_new = jnp.maximum(m_old, jnp.max(s, axis=1, keepdims=True))
            p = jnp.exp2(s - _lane_tile(m_new, tk))
            alpha = jnp.exp2(m_old - m_new)
            acc_s[j] = _lane_tile(alpha, 2 * LANES) * acc_s[j] + _dot(p.astype(BF16), vt)
            m_s[j] = m_new

    def step(blk, n_hi, first, g, cur_s, nxt_s):
        t = n_hi - 1 - (g - first)
        live = jnp.logical_and(g >= first, t >= 0)
        st = status_ref[(b * nq + blk) * nk + jnp.clip(t, 0, nk - 1)]
        here = t >= 1
        n_next = nhi_ref[b * nq + jnp.minimum(blk + 1, nq - 1)]
        nxt = (jnp.where(here, blk, blk + 1), jnp.where(here, t - 1, n_next - 1), nxt_s)
        some = jnp.logical_or(here, blk + 1 < nq)

        @pl.when(jnp.logical_and(jnp.logical_and(live, st == 0), some))
        def _():
            logits(*nxt)

        for code, masked in ((1, False), (2, True)):
            hit = jnp.logical_and(live, st == code)

            @pl.when(jnp.logical_and(hit, some))
            def _():
                tile(blk, t, masked, cur_s, nxt)

            @pl.when(jnp.logical_and(hit, jnp.logical_not(some)))
            def _():
                tile(blk, t, masked, cur_s, None)

    def block(blk, first):
        n_hi = nhi_ref[b * nq + blk]
        m_s[...] = jnp.full(m_s.shape, NEG, F32)
        acc_s[...] = jnp.zeros(acc_s.shape, F32)

        def pair(u, carry):
            step(blk, n_hi, first, 2 * u, s0_s, s1_s)
            step(blk, n_hi, first, 2 * u + 1, s1_s, s0_s)
            return carry

        lax.fori_loop(first // 2, (first + n_hi + 1) // 2, pair, 0)
        o = (acc_s[0, :, :LANES] / acc_s[0, :, LANES:]
             - lam * (acc_s[1, :, :LANES] / acc_s[1, :, LANES:]))
        o = _rms(o, gsub_ref[...]) * (1.0 - lam_init)
        o_ref[pl.ds(pl.multiple_of(blk * tq, tq), tq), :] = o.astype(o_ref.dtype)
        return first + n_hi

    logits(0, nhi_ref[b * nq] - 1, s0_s)
    lax.fori_loop(0, nq, block, jnp.int32(0))


def _diff_attention(q, k, v, cq3, ck3, status, nhi, lam_p, gsub, b, s, tq, tk, lam_init):
    nq, nk = s // tq, s // tk
    kern = functools.partial(_diff_attn_kernel, tq=tq, tk=tk, nq=nq, nk=nk, lam_init=lam_init)
    per_head = lambda bb, h, *_: (bb, h)
    grid_spec = pltpu.PrefetchScalarGridSpec(
        num_scalar_prefetch=2,
        grid=(b, DIFF_HEADS),
        in_specs=[
            pl.BlockSpec((s, LANES), per_head),
            pl.BlockSpec((s, LANES), per_head),
            pl.BlockSpec((s, LANES), per_head),
            pl.BlockSpec((nq, tq, 1), lambda bb, h, *_: (bb, 0, 0)),
            pl.BlockSpec((1, 1, s), lambda bb, h, *_: (bb, 0, 0)),
            pl.BlockSpec((4, HEAD_DIM), lambda bb, h, *_: (0, 0)),
            pl.BlockSpec((1, LANES), lambda bb, h, *_: (0, 0)),
        ],
        out_specs=pl.BlockSpec((s, LANES), per_head),
        scratch_shapes=[pltpu.VMEM((2, tq, LANES), F32),
                        pltpu.VMEM((2, tq, 2 * LANES), F32),
                        pltpu.VMEM((2, tq, tk), F32),
                        pltpu.VMEM((2, tq, tk), F32)],
    )
    return pl.pallas_call(
        kern, grid_spec=grid_spec,
        out_shape=jax.ShapeDtypeStruct((b * s, SEQ_WIDTH), BF16),
        compiler_params=_params(("parallel", "parallel")), name="diff_attn",
    )(status, nhi, q, k, v, cq3, ck3, lam_p, gsub)


def _dsa_kernel(status_ref, nhi_ref, q_ref, iq_ref, iw_ref, kk_ref, vv_ref, ikk_ref, cq_ref, ck_ref,
                o_ref, keys_s, cand_s, qst_s, iqst_s, iwb_s, m_s, acc_s, thf_s, jst_s, floor_s, nall_s, ncand_s,
                *, tq, tk, nq, nk, s_len, top_k):
    b = pl.program_id(0)
    i = pl.program_id(1)
    base = (b * nq + i) * nk
    n_hi = nhi_ref[b * nq + i]
    cq = cq_ref[0]
    lane = lax.broadcasted_iota(I32, (tq, LANES), 1)
    first = lane < HEAD_DIM
    spt = tk // LANES

    for h in range(DSA_HEADS):
        slab = q_ref[:, (h // 2) * LANES:(h // 2 + 1) * LANES]
        keep = first if h % 2 == 0 else jnp.logical_not(first)
        qst_s[h * tq:(h + 1) * tq, :] = jnp.where(keep, slab, jnp.zeros_like(slab))
    for h in range(IDX_HEADS):
        slab = iq_ref[:, (h // 2) * LANES:(h // 2 + 1) * LANES]
        keep = first if h % 2 == 0 else jnp.logical_not(first)
        iqst_s[h * tq:(h + 1) * tq, :] = jnp.where(keep, slab, jnp.zeros_like(slab))
        iwb_s[h * tq:(h + 1) * tq, :] = jnp.broadcast_to(iw_ref[:, h:h + 1], (tq, LANES))

    cand_s[...] = jnp.full(cand_s.shape, -jnp.inf, F32)

    def fold_candidates(sc):
        for g in range(tq // CAND_ROWS):
            rows = slice(g * CAND_ROWS, (g + 1) * CAND_ROWS)
            tops = [cand_s[r, rows, :] for r in range(TOP_PER_LANE)]
            for c in range(tk // LANES):
                x = sc[rows, c * LANES:(c + 1) * LANES]
                for r in range(TOP_PER_LANE):
                    tops[r], x = jnp.maximum(tops[r], x), jnp.minimum(tops[r], x)
            for r in range(TOP_PER_LANE):
                cand_s[r, rows, :] = tops[r]

    def score_body(t, carry):
        st = status_ref[base + t]
        off = pl.multiple_of(t * tk, tk)

        @pl.when(st == 0)
        def _():
            for c in range(spt):
                keys_s[t * spt + c] = jnp.full((tq, LANES), -jnp.inf, F32)

        def scores(masked):
            ikt = ikk_ref[pl.ds(off, tk), :]
            logits = _dot_nt(iqst_s[...], ikt)
            r = jnp.maximum(logits, 0.0) * _lane_tile(iwb_s[...], tk)
            sc = r[0:tq]
            for h in range(1, IDX_HEADS):
                sc = sc + r[h * tq:(h + 1) * tq]
            sc = jnp.where(sc == 0.0, 0.0, sc)
            if masked:
                ck = ck_ref[0, :, pl.ds(off, tk)]
                sc = jnp.where(cq >= ck, sc, -jnp.inf)
            for c in range(spt):
                keys_s[t * spt + c] = sc[:, c * LANES:(c + 1) * LANES]
            fold_candidates(sc)

        @pl.when(st == 1)
        def _():
            scores(False)

        @pl.when(st == 2)
        def _():
            scores(True)

        return carry

    lax.fori_loop(0, n_hi, score_body, 0)

    rg = min(tq, LANES)
    lane_rg = lax.broadcasted_iota(I32, (rg, LANES), 1)

    def count_tiles(preds, *operands):
        zero = jnp.zeros((rg, LANES), I32)
        accs = []
        for r in range(tq // rg):
            rows = slice(r * rg, (r + 1) * rg)
            ops = [o[rows] for o in operands]

            def body(t, acc, rows=rows, ops=ops):
                off = pl.multiple_of(t * tk, tk)
                acc = list(acc)
                for c in range(spt):
                    slab = keys_s[t * spt + c, rows, :]
                    for n, pred in enumerate(preds):
                        acc[n] = acc[n] + jnp.where(pred(slab, off + c * LANES, *ops), 1, 0).astype(I32)
                return tuple(acc)
            accs.append(lax.fori_loop(0, n_hi, body, (zero,) * len(preds)))
        outs = []
        for n in range(len(preds)):
            acc = accs[0][n] if len(accs) == 1 else jnp.concatenate([a[n] for a in accs], axis=0)
            outs.append(_row_total(acc))
        return outs

    ge = lambda slab, col0, c: slab >= c
    gt = lambda slab, col0, c: slab > c
    eq = lambda slab, col0, c: slab == c

    def count_cand(rows, c):
        n_groups = TOP_PER_LANE // CAND_GROUP
        full = [cand_s[CAND_GROUP * i + CAND_GROUP - 1, rows, :] >= c for i in range(n_groups)]
        base = jnp.full(c.shape, TOP_PER_LANE, I32)
        rest = [jnp.full(c.shape, -jnp.inf, F32)] * (CAND_GROUP - 1)
        for i in reversed(range(n_groups)):
            base = jnp.where(full[i], base, CAND_GROUP * i)
            rest = [jnp.where(full[i], rest[k], cand_s[CAND_GROUP * i + k, rows, :])
                    for k in range(CAND_GROUP - 1)]
        for x in rest:
            base = base + jnp.where(x >= c, 1, 0).astype(I32)
        return base

    def bisect(count):
        def bit_body(j, thr):
            cand = thr + lax.shift_left(jnp.int32(1), 31 - j)
            cnt, = count([ge], _key_to_f32(cand))
            return jnp.where(cnt >= top_k, cand, thr)
        return lax.fori_loop(0, 32, bit_body, jnp.full((tq, LANES), INT_MIN, I32))

    def bisect_candidates():
        half = tq // 2
        halves = (slice(0, half), slice(half, tq))

        def raw_count(rows, key):
            return count_cand(rows, _key_to_f32(key))

        def decide(acc, cand, thr):
            return jnp.where(_row_total(acc) >= top_k, cand, thr)

        bit = lambda j: lax.shift_left(jnp.int32(1), 31 - j)
        start = jnp.full((half, LANES), INT_MIN, I32)

        def body(j, carry):
            thr_a, cand_a, acc_a, thr_b = carry
            cand_b = thr_b + bit(j)
            acc_b = raw_count(halves[1], cand_b)
            thr_a = decide(acc_a, cand_a, thr_a)
            cand_a = thr_a + bit(j + 1)
            acc_a = raw_count(halves[0], cand_a)
            thr_b = decide(acc_b, cand_b, thr_b)
            return thr_a, cand_a, acc_a, thr_b

        cand_a = start + bit(0)
        carry = (start, cand_a, raw_count(halves[0], cand_a), start)
        thr_a, cand_a, acc_a, thr_b = lax.fori_loop(0, 31, body, carry)
        thr_a = decide(acc_a, cand_a, thr_a)
        cand_b = thr_b + bit(31)
        thr_b = decide(raw_count(halves[1], cand_b), cand_b, thr_b)
        return jnp.concatenate([thr_a, thr_b], axis=0)

    lane_k = lax.broadcasted_iota(I32, (tq, tk), 1)
    lane_v = lax.broadcasted_iota(I32, (tk, LANES), 1)

    def prepare(thr_key, cnt_gt, cnt_eq):
        real = thr_key > MASKED_KEY
        thr = jnp.where(real, _key_to_f32(thr_key), -jnp.inf)
        thf_s[...] = thr
        need = top_k - cnt_gt
        excess = jnp.logical_and(real, cnt_eq > need)
        jst_s[...] = jnp.where(real, jnp.int32(s_len), jnp.int32(-1))

        @pl.when(jnp.max(excess.astype(I32)) > 0)
        def _():
            def idx_body(j, jcur):
                cand = jcur + lax.shift_left(jnp.int32(1), (s_len.bit_length() - 2) - j)
                cnt, = count_tiles(
                    [lambda slab, col0, th, c: jnp.logical_and(slab == th, (col0 + lane_rg) < c)], thr, cand)
                return jnp.where(cnt < need, cand, jcur)
            jbest = lax.fori_loop(0, s_len.bit_length() - 1, idx_body, jnp.zeros((tq, LANES), I32))
            jst_s[...] = jnp.where(excess, jbest, jst_s[...])

    def attend():
        m_s[...] = jnp.full(m_s.shape, NEG, F32)
        acc_s[...] = jnp.zeros(acc_s.shape, F32)
        nall_s[...] = jnp.zeros(nall_s.shape, I32)

        def attn_body(t, carry):
            st = status_ref[base + t]

            @pl.when(st != 0)
            def _():
                off = pl.multiple_of(t * tk, tk)
                slabs = [keys_s[t * spt + c] for c in range(spt)]
                seen = nall_s[...]
                for slab in slabs:
                    seen = seen + jnp.where(slab >= floor_s[...], 1, 0).astype(I32)
                nall_s[...] = seen
                key = jnp.concatenate(slabs, axis=1)
                th = _lane_tile(thf_s[...], tk)
                jst = _lane_tile(jst_s[...], tk)
                sel = jnp.logical_or(key > th, jnp.logical_and(key == th, (lane_k + off) <= jst))
                bias = jnp.where(sel, 0.0, NEG)
                kt = kk_ref[pl.ds(off, tk), :]
                vt = vv_ref[pl.ds(off, tk), :]
                vt = jnp.where(lane_v < HEAD_DIM, vt, jnp.ones_like(vt))
                s = _dot_nt(qst_s[...], kt)
                s = (s.reshape(DSA_HEADS, tq, tk) + bias[None]).reshape(DSA_HEADS * tq, tk)
                m_old = m_s[...]
                m_new = jnp.maximum(m_old, jnp.max(s, axis=1, keepdims=True))
                p = jnp.exp2(s - _lane_tile(m_new, tk))
                alpha = jnp.exp2(m_old - m_new)
                acc_s[...] = alpha * acc_s[...] + _dot(p.astype(BF16), vt)
                m_s[...] = m_new

            return carry

        lax.fori_loop(0, n_hi, attn_body, 0)

    thr_key = bisect_candidates()
    everything = slice(0, tq)
    cnt_gt = _row_total(count_cand(everything, _key_to_f32(thr_key + 1)))
    cnt_eq = _row_total(count_cand(everything, _key_to_f32(thr_key))) - cnt_gt
    floor = _key_to_f32(jnp.maximum(thr_key, MASKED_KEY + 1))
    floor_s[...] = floor
    ncand_s[...] = _row_total(count_cand(everything, floor))
    prepare(thr_key, cnt_gt, cnt_eq)
    attend()

    @pl.when(jnp.max((_row_total(nall_s[...]) != ncand_s[...]).astype(I32)) > 0)
    def _():
        thr_all = bisect(count_tiles)
        thr_f = jnp.where(thr_all > MASKED_KEY, _key_to_f32(thr_all), -jnp.inf)
        gt_all, eq_all = count_tiles([gt, eq], thr_f)
        prepare(thr_all, gt_all, eq_all)
        attend()

    for j in range(DSA_HEADS // 2):
        acc_a = acc_s[2 * j * tq:(2 * j + 1) * tq]
        acc_b = acc_s[(2 * j + 1) * tq:(2 * j + 2) * tq]
        swap_a = pltpu.roll(acc_a, HEAD_DIM, 1)
        swap_b = pltpu.roll(acc_b, HEAD_DIM, 1)
        o_ref[:, j * LANES:(j + 1) * LANES] = jnp.where(first, acc_a / swap_a,
                                                        swap_b / acc_b).astype(o_ref.dtype)


def _dsa_attention(q, iq, iw, kk, vv, ikk, cq3, ck3, status, nhi, b, s, tq, tk, top_k):
    nq, nk = s // tq, s // tk
    kern = functools.partial(_dsa_kernel, tq=tq, tk=tk, nq=nq, nk=nk, s_len=s, top_k=top_k)
    rowblk = lambda bb, i, *_: (bb * nq + i, 0)
    perb = lambda bb, i, *_: (bb, 0)
    grid_spec = pltpu.PrefetchScalarGridSpec(
        num_scalar_prefetch=2,
        grid=(b, nq),
        in_specs=[
            pl.BlockSpec((tq, SEQ_WIDTH), rowblk),
            pl.BlockSpec((tq, IDX_HEADS * IDX_DIM), rowblk),
            pl.BlockSpec((tq, LANES), rowblk),
            pl.BlockSpec((s, LANES), perb),
            pl.BlockSpec((s, LANES), perb),
            pl.BlockSpec((s, LANES), perb),
            pl.BlockSpec((1, tq, 1), lambda bb, i, *_: (bb * nq + i, 0, 0)),
            pl.BlockSpec((1, 1, s), lambda bb, i, *_: (bb, 0, 0)),
        ],
        out_specs=pl.BlockSpec((tq, SEQ_WIDTH), rowblk),
        scratch_shapes=[
            pltpu.VMEM((s // LANES, tq, LANES), F32),
            pltpu.VMEM((TOP_PER_LANE, tq, LANES), F32),
            pltpu.VMEM((DSA_HEADS * tq, LANES), BF16),
            pltpu.VMEM((IDX_HEADS * tq, LANES), BF16),
            pltpu.VMEM((IDX_HEADS * tq, LANES), F32),
            pltpu.VMEM((DSA_HEADS * tq, LANES), F32),
            pltpu.VMEM((DSA_HEADS * tq, LANES), F32),
            pltpu.VMEM((tq, LANES), F32),
            pltpu.VMEM((tq, LANES), I32),
            pltpu.VMEM((tq, LANES), F32),
            pltpu.VMEM((tq, LANES), I32),
            pltpu.VMEM((tq, LANES), I32),
        ],
    )
    return pl.pallas_call(
        kern, grid_spec=grid_spec,
        out_shape=jax.ShapeDtypeStruct((b * s, SEQ_WIDTH), BF16),
        compiler_params=_params(("parallel", "arbitrary")), name="dsa_attn",
    )(status, nhi, q, iq, iw, kk, vv, ikk, cq3, ck3)


def _mix_mlp_kernel(x_ref, oseq_ref, qm_ref, km_ref, vm_ref, wo_ref, gmix_ref, gpre_ref, wup_ref, wdn_ref,
                    gpost_ref, o_ref, h_s, acc_s):
    j = pl.program_id(1)

    @pl.when(j == 0)
    def _():
        qm = qm_ref[...]
        km = km_ref[...]
        vm = vm_ref[...]
        lane = lax.broadcasted_iota(I32, qm.shape, 1)
        mine = [(lane // HEAD_DIM) == h for h in range(MEM_HEADS)]
        q_st = jnp.concatenate([jnp.where(m_h, qm, jnp.zeros_like(qm)) for m_h in mine], axis=0)
        s = _dot_nt(q_st, km)
        p = jnp.exp(s - jnp.max(s, axis=1, keepdims=True))
        p = p / jnp.sum(p, axis=1, keepdims=True)
        o_st = _dot(p.astype(BF16), vm)
        tm = qm.shape[0]
        o_mem = jnp.zeros(qm.shape, F32)
        for h in range(MEM_HEADS):
            o_mem = o_mem + jnp.where(mine[h], o_st[h * tm:(h + 1) * tm], 0.0)
        y = (_dot(oseq_ref[...], wo_ref[0:SEQ_WIDTH, :])
             + _dot(o_mem.astype(BF16), wo_ref[SEQ_WIDTH:, :]))
        x_mid = x_ref[...] + _rms(y, gmix_ref[...])
        o_ref[...] = x_mid
        h_s[...] = _rms(x_mid, gpre_ref[...]).astype(BF16)
        acc_s[...] = jnp.zeros(acc_s.shape, F32)

    u = jnp.maximum(_dot(h_s[...], wup_ref[...]), 0.0)
    acc_s[...] += _dot((u * u).astype(BF16), wdn_ref[...])

    @pl.when(j == pl.num_programs(1) - 1)
    def _():
        o_ref[...] = o_ref[...] + _rms(acc_s[...], gpost_ref[...])


def _mix_mlp(x2d, oseq, qm, kv, w_out, g_mix, g_pre, w_up, w_dn, g_post, s, n_mem, tm, tf):
    m = x2d.shape[0]
    nt = s // tm
    row = lambda i, j: (i, 0)
    fixed = lambda i, j: (0, 0)
    vec = lambda g: g.reshape(1, D_MODEL)
    return pl.pallas_call(
        _mix_mlp_kernel, grid=(m // tm, D_FF // tf),
        in_specs=[
            pl.BlockSpec((tm, D_MODEL), row),
            pl.BlockSpec((tm, SEQ_WIDTH), row),
            pl.BlockSpec((tm, MEM_WIDTH), row),
            pl.BlockSpec((n_mem, MEM_WIDTH), lambda i, j: (i // nt, 0)),
            pl.BlockSpec((n_mem, MEM_WIDTH), lambda i, j: (i // nt, 1)),
            pl.BlockSpec((D_MODEL, D_MODEL), fixed),
            pl.BlockSpec((1, D_MODEL), fixed),
            pl.BlockSpec((1, D_MODEL), fixed),
            pl.BlockSpec((D_MODEL, tf), lambda i, j: (0, j)),
            pl.BlockSpec((tf, D_MODEL), lambda i, j: (j, 0)),
            pl.BlockSpec((1, D_MODEL), fixed),
        ],
        out_specs=pl.BlockSpec((tm, D_MODEL), row),
        out_shape=jax.ShapeDtypeStruct(x2d.shape, F32),
        scratch_shapes=[pltpu.VMEM((tm, D_MODEL), BF16), pltpu.VMEM((tm, D_MODEL), F32)],
        compiler_params=_params(("parallel", "arbitrary")), name="mix_mlp",
    )(x2d, oseq, qm, kv, kv, w_out, vec(g_mix), vec(g_pre), w_up, w_dn, vec(g_post))


def _rope_tables(positions):
    half = HEAD_DIM // 2
    inv = ROPE_THETA ** (-jnp.arange(0, HEAD_DIM, 2, dtype=F32) / HEAD_DIM)
    ang = positions.astype(F32).reshape(-1, 1) * inv
    cos, sin = jnp.cos(ang), jnp.sin(ang)
    reps = LANES // HEAD_DIM
    cos_t = jnp.tile(cos, (1, 2 * reps))
    sin_t = jnp.tile(jnp.concatenate([-sin, sin], axis=1), (1, reps))
    assert cos_t.shape[1] == LANES and half * 2 == HEAD_DIM
    return cos_t, sin_t


def _visibility(chunk_id, tq, tk):
    b, s = chunk_id.shape
    cq = chunk_id.reshape(b, s // tq, tq)
    ck = chunk_id.reshape(b, s // tk, tk)
    qmin, qmax = cq.min(-1)[:, :, None], cq.max(-1)[:, :, None]
    kmin, kmax = ck.min(-1)[:, None, :], ck.max(-1)[:, None, :]
    status = jnp.where(kmin > qmax, 0, jnp.where(kmax <= qmin, 1, 2)).astype(I32)
    tiles = jnp.arange(s // tk, dtype=I32)[None, None, :]
    nhi = jnp.max(jnp.where(status != 0, tiles + 1, 0), axis=-1).astype(I32)
    return status.reshape(-1), nhi.reshape(-1)


def _dup(w):
    return jnp.concatenate([w, w], axis=1)


def kernel(x, mem, positions, g_pre_mix, g_post_mix, g_mem, w_mem_kv, w_out, g_pre_mlp, g_post_mlp,
           w_mlp_up, w_mlp_down, w_in_diff, lambda_q1, lambda_k1, lambda_q2, lambda_k2, g_diff_subln,
           w_in_dsa):
    b, s, d = x.shape
    n_mem = mem.shape[1]
    depth = g_pre_mix.shape[0]
    m = b * s
    scale = HEAD_DIM ** -0.5
    scale2 = scale * math.log2(math.e)
    tm_proj = min(1024, s)
    tq_diff, tk_diff = min(512, s), min(1024, s)
    tq_dsa, tk_dsa = min(256, s), min(512, s)
    tm_mlp, tf_mlp = min(1024, s), 1024
    top_k = min(DSA_TOPK_MAX, s // 4)
    assert tk_dsa >= top_k, "the threshold search needs one key tile to hold top_k candidates"

    cos_t, sin_t = _rope_tables(positions)
    chunk_id = positions // CHUNK
    cq3 = chunk_id.reshape(-1, 1)
    ck3 = chunk_id.reshape(b, 1, s)
    vis_diff = _visibility(chunk_id, tq_diff, tk_diff)
    vis_dsa = _visibility(chunk_id, tq_dsa, tk_dsa)

    x2d = x.reshape(m, d)
    mem2d = mem.reshape(b * n_mem, d)
    for i in range(depth):
        j = i // 2
        kv = _inproj(mem2d, g_mem[i], None, None, None, w_mem_kv[i].astype(BF16), (),
                     ((2 * MEM_WIDTH, 1.0, BF16),), min(256, b * n_mem))[0]
        if i % 2 == 0:
            w = w_in_diff[j].astype(BF16)
            nqk = 2 * DIFF_HEADS * HEAD_DIM
            q, k, v, qm = _inproj(
                x2d, g_pre_mix[i], cos_t, sin_t, w[:, :2 * nqk], w[:, 2 * nqk:],
                ((nqk, scale2, BF16), (nqk, 1.0, BF16)),
                ((SEQ_WIDTH, 1.0, BF16), (MEM_WIDTH, scale, BF16)), tm_proj)
            lam_p = jnp.stack([lambda_q1[j], lambda_k1[j], lambda_q2[j], lambda_k2[j]]).astype(F32)
            lam_init = 0.8 - 0.6 * math.exp(-0.3 * i)
            o_seq = _diff_attention(
                q, k, v, cq3.reshape(m // tq_diff, tq_diff, 1), ck3, vis_diff[0], vis_diff[1],
                lam_p, g_diff_subln[j].reshape(1, LANES), b, s, tq_diff, tk_diff, lam_init)
        else:
            w = w_in_dsa[j]
            o0 = 0
            parts = []
            for width in (DSA_HEADS * HEAD_DIM, HEAD_DIM, HEAD_DIM, IDX_HEADS * IDX_DIM, IDX_DIM,
                          IDX_HEADS, MEM_WIDTH):
                parts.append(w[:, o0:o0 + width])
                o0 += width
            wq, wk, wv, wiq, wik, wiw, wqm = parts
            wiw = jnp.pad(wiw, ((0, 0), (0, LANES - IDX_HEADS)))
            w_rope = jnp.concatenate([wq, wiq, _dup(wk), _dup(wik)], axis=1).astype(BF16)
            w_plain = jnp.concatenate([_dup(wv), wiw, wqm], axis=1).astype(BF16)
            iw_scale = IDX_HEADS ** -0.5 * IDX_DIM ** -0.5
            q, iq, kk, ikk, vv, iw, qm = _inproj(
                x2d, g_pre_mix[i], cos_t, sin_t, w_rope, w_plain,
                ((DSA_HEADS * HEAD_DIM, scale2, BF16), (IDX_HEADS * IDX_DIM, 1.0, BF16),
                 (LANES, 1.0, BF16), (LANES, 1.0, BF16)),
                ((LANES, 1.0, BF16), (LANES, iw_scale, F32), (MEM_WIDTH, scale, BF16)), tm_proj)
            o_seq = _dsa_attention(
                q, iq, iw, kk, vv, ikk, cq3.reshape(m // tq_dsa, tq_dsa, 1), ck3,
                vis_dsa[0], vis_dsa[1], b, s, tq_dsa, tk_dsa, top_k)
        x2d = _mix_mlp(x2d, o_seq, qm, kv, w_out[i].astype(BF16), g_post_mix[i], g_pre_mlp[i],
                       w_mlp_up[i].astype(BF16), w_mlp_down[i].astype(BF16), g_post_mlp[i],
                       s, n_mem, tm_mlp, tf_mlp)
    return x2d.reshape(b, s, d)
```

```python
import functools
import math

import jax
import jax.numpy as jnp
from jax import lax
from jax.experimental import pallas as pl
from jax.experimental.pallas import tpu as pltpu

D_MODEL = 1024
CHUNK = 64
HEAD_DIM = 64
LANES = 128
ROPE_THETA = 10000.0
EPS = 1e-6
MEM_HEADS = 4
MEM_WIDTH = MEM_HEADS * HEAD_DIM
SEQ_WIDTH = D_MODEL - MEM_WIDTH
DIFF_HEADS = SEQ_WIDTH // (2 * HEAD_DIM)
DSA_HEADS = SEQ_WIDTH // HEAD_DIM
IDX_HEADS = 8
IDX_DIM = 64
DSA_TOPK_MAX = 256
D_FF = 4 * D_MODEL

NEG = -1e30
INT_MIN = -(2 ** 31)
MASKED_KEY = -2139095041
TOP_PER_LANE = 12
CAND_ROWS = 32
CAND_GROUP = 2
assert TOP_PER_LANE % CAND_GROUP == 0
VMEM_LIMIT = 56 * 1024 * 1024

F32 = jnp.float32
BF16 = jnp.bfloat16
I32 = jnp.int32


def _dot(a, b):
    return jnp.dot(a, b, preferred_element_type=F32)


def _dot_nt(a, b):
    return lax.dot_general(a, b, (((1,), (1,)), ((), ())), preferred_element_type=F32)


def _lane_tile(x, width):
    return jnp.concatenate([x] * (width // LANES), axis=1)


def _row_total(counts):
    total = jnp.sum(counts.astype(F32), axis=1, keepdims=True)
    return jnp.broadcast_to(total, counts.shape).astype(I32)


def _key_to_f32(key):
    return lax.bitcast_convert_type(jnp.where(key < 0, key ^ 0x7FFFFFFF, key), F32)


def _rms(x, g):
    return x * lax.rsqrt(jnp.mean(x * x, axis=-1, keepdims=True) + EPS) * g


def _params(sem):
    return pltpu.CompilerParams(dimension_semantics=sem, vmem_limit_bytes=VMEM_LIMIT)


def _inproj_kernel(*refs, rope_outs, plain_outs):
    n_r, n_p = len(rope_outs), len(plain_outs)
    x_ref, g_ref = refs[0], refs[1]
    pos = 2
    if n_r:
        cos_ref, sin_ref, wr_ref = refs[2], refs[3], refs[4]
        pos = 5
    if n_p:
        wp_ref = refs[pos]
        pos += 1
    out_refs = refs[pos:]
    x = x_ref[...]
    h = _rms(x, g_ref[...]).astype(BF16)
    oi = 0
    if n_r:
        cos = cos_ref[...]
        sin = sin_ref[...]
        lane = lax.broadcasted_iota(I32, cos.shape, 1)
        low = (lane % HEAD_DIM) < (HEAD_DIM // 2)
        c0 = 0
        for width, scale in rope_outs:
            o_ref = out_refs[oi]
            oi += 1
            y_all = _dot(h, wr_ref[:, c0:c0 + width])
            for s in range(width // LANES):
                y = y_all[:, s * LANES:(s + 1) * LANES]
                swapped = jnp.where(low, pltpu.roll(y, LANES - HEAD_DIM // 2, 1),
                                    pltpu.roll(y, HEAD_DIM // 2, 1))
                r = y * cos + swapped * sin
                if scale != 1.0:
                    r = r * scale
                o_ref[:, s * LANES:(s + 1) * LANES] = r.astype(o_ref.dtype)
            c0 += width
    c0 = 0
    for width, scale in plain_outs:
        o_ref = out_refs[oi]
        oi += 1
        y = _dot(h, wp_ref[:, c0:c0 + width])
        if scale != 1.0:
            y = y * scale
        o_ref[...] = y.astype(o_ref.dtype)
        c0 += width


def _inproj(x2d, g, cos, sin, w_rope, w_plain, rope_outs, plain_outs, tm):
    m = x2d.shape[0]
    row = lambda i: (i, 0)
    fixed = lambda i: (0, 0)
    in_specs = [pl.BlockSpec((tm, D_MODEL), row), pl.BlockSpec((1, D_MODEL), fixed)]
    args = [x2d, g.reshape(1, D_MODEL)]
    if rope_outs:
        in_specs += [pl.BlockSpec((tm, LANES), row), pl.BlockSpec((tm, LANES), row),
                     pl.BlockSpec(w_rope.shape, fixed)]
        args += [cos, sin, w_rope]
    if plain_outs:
        in_specs += [pl.BlockSpec(w_plain.shape, fixed)]
        args += [w_plain]
    outs = tuple(rope_outs) + tuple(plain_outs)
    out_shape = [jax.ShapeDtypeStruct((m, w), dt) for w, _, dt in outs]
    out_specs = [pl.BlockSpec((tm, w), row) for w, _, _ in outs]
    kern = functools.partial(_inproj_kernel,
                             rope_outs=tuple((w, s) for w, s, _ in rope_outs),
                             plain_outs=tuple((w, s) for w, s, _ in plain_outs))
    return pl.pallas_call(
        kern, grid=(m // tm,), in_specs=in_specs, out_specs=out_specs, out_shape=out_shape,
        compiler_params=_params(("parallel",)), name="inproj")(*args)


def _diff_attn_kernel(status_ref, nhi_ref, q_ref, k_ref, v_ref, cq_ref, ck_ref, lam_ref, gsub_ref,
                      o_ref, m_s, acc_s, s0_s, s1_s, *, tq, tk, nq, nk, lam_init):
    b = pl.program_id(0)
    lane = lax.broadcasted_iota(I32, (tq, LANES), 1)
    lam_p = lam_ref[...]
    lam = (jnp.exp(jnp.sum(lam_p[0:1] * lam_p[1:2], axis=1, keepdims=True))
           - jnp.exp(jnp.sum(lam_p[2:3] * lam_p[3:4], axis=1, keepdims=True)) + lam_init)

    def logits(blk, t, dst_s):
        q = q_ref[pl.ds(pl.multiple_of(blk * tq, tq), tq), :]
        zero = jnp.zeros_like(q)
        kt = k_ref[pl.ds(pl.multiple_of(t * tk, tk), tk), :]
        dst_s[0] = _dot_nt(jnp.where(lane < HEAD_DIM, q, zero), kt)
        dst_s[1] = _dot_nt(jnp.where(lane >= HEAD_DIM, q, zero), kt)

    def tile(blk, t, masked, cur_s, nxt):
        if nxt is not None:
            logits(*nxt)
        off = pl.multiple_of(t * tk, tk)
        vt = jnp.concatenate([v_ref[pl.ds(off, tk), :], jnp.ones((tk, LANES), BF16)], axis=1)
        if masked:
            ck = ck_ref[0, :, pl.ds(off, tk)]
            pieces = tq // cq_ref.shape[1]
            cq = jnp.concatenate([cq_ref[blk * pieces + r] for r in range(pieces)], axis=0)
            bias = jnp.where(cq >= ck, 0.0, NEG)
        for j in range(2):
            s = cur_s[j] + bias if masked else cur_s[j]
            m_old = m_s[j]
            m_new = jnp.maximum(m_old, jnp.max(s, axis=1, keepdims=True))
            p = jnp.exp2(s - _lane_tile(m_new, tk))
            alpha = jnp.exp2(m_old - m_new)
            acc_s[j] = _lane_tile(alpha, 2 * LANES) * acc_s[j] + _dot(p.astype(BF16), vt)
            m_s[j] = m_new

    def step(blk, n_hi, first, g, cur_s, nxt_s):
        t = n_hi - 1 - (g - first)
        live = jnp.logical_and(g >= first, t >= 0)
        st = status_ref[(b * nq + blk) * nk + jnp.clip(t, 0, nk - 1)]
        here = t >= 1
        n_next = nhi_ref[b * nq + jnp.minimum(blk + 1, nq - 1)]
        nxt = (jnp.where(here, blk, blk + 1), jnp.where(here, t - 1, n_next - 1), nxt_s)
        some = jnp.logical_or(here, blk + 1 < nq)

        @pl.when(jnp.logical_and(jnp.logical_and(live, st == 0), some))
        def _():
            logits(*nxt)

        for code, masked in ((1, False), (2, True)):
            hit = jnp.logical_and(live, st == code)

            @pl.when(jnp.logical_and(hit, some))
            def _():
                tile(blk, t, masked, cur_s, nxt)

            @pl.when(jnp.logical_and(hit, jnp.logical_not(some)))
            def _():
                tile(blk, t, masked, cur_s, None)

    def block(blk, first):
        n_hi = nhi_ref[b * nq + blk]
        m_s[...] = jnp.full(m_s.shape, NEG, F32)
        acc_s[...] = jnp.zeros(acc_s.shape, F32)

        def pair(u, carry):
            step(blk, n_hi, first, 2 * u, s0_s, s1_s)
            step(blk, n_hi, first, 2 * u + 1, s1_s, s0_s)
            return carry

        lax.fori_loop(first // 2, (first + n_hi + 1) // 2, pair, 0)
        o = (acc_s[0, :, :LANES] / acc_s[0, :, LANES:]
             - lam * (acc_s[1, :, :LANES] / acc_s[1, :, LANES:]))
        o = _rms(o, gsub_ref[...]) * (1.0 - lam_init)
        o_ref[pl.ds(pl.multiple_of(blk * tq, tq), tq), :] = o.astype(o_ref.dtype)
        return first + n_hi

    logits(0, nhi_ref[b * nq] - 1, s0_s)
    lax.fori_loop(0, nq, block, jnp.int32(0))


def _diff_attention(q, k, v, cq3, ck3, status, nhi, lam_p, gsub, b, s, tq, tk, lam_init):
    nq, nk = s // tq, s // tk
    kern = functools.partial(_diff_attn_kernel, tq=tq, tk=tk, nq=nq, nk=nk, lam_init=lam_init)
    per_head = lambda bb, h, *_: (bb, h)
    grid_spec = pltpu.PrefetchScalarGridSpec(
        num_scalar_prefetch=2,
        grid=(b, DIFF_HEADS),
        in_specs=[
            pl.BlockSpec((s, LANES), per_head),
            pl.BlockSpec((s, LANES), per_head),
            pl.BlockSpec((s, LANES), per_head),
            pl.BlockSpec((s // cq3.shape[1], cq3.shape[1], 1), lambda bb, h, *_: (bb, 0, 0)),
            pl.BlockSpec((1, 1, s), lambda bb, h, *_: (bb, 0, 0)),
            pl.BlockSpec((4, HEAD_DIM), lambda bb, h, *_: (0, 0)),
            pl.BlockSpec((1, LANES), lambda bb, h, *_: (0, 0)),
        ],
        out_specs=pl.BlockSpec((s, LANES), per_head),
        scratch_shapes=[pltpu.VMEM((2, tq, LANES), F32),
                        pltpu.VMEM((2, tq, 2 * LANES), F32),
                        pltpu.VMEM((2, tq, tk), F32),
                        pltpu.VMEM((2, tq, tk), F32)],
    )
    return pl.pallas_call(
        kern, grid_spec=grid_spec,
        out_shape=jax.ShapeDtypeStruct((b * s, SEQ_WIDTH), BF16),
        compiler_params=_params(("parallel", "parallel")), name="diff_attn",
    )(status, nhi, q, k, v, cq3, ck3, lam_p, gsub)


def _dsa_kernel(status_ref, nhi_ref, q_ref, iq_ref, iw_ref, kk_ref, vv_ref, ikk_ref, cq_ref, ck_ref,
                o_ref, keys_s, cand_s, qst_s, iqst_s, iwb_s, m_s, acc_s, thf_s, jst_s, floor_s, nall_s, ncand_s,
                *, tq, tk, nq, nk, s_len, top_k):
    b = pl.program_id(0)
    i = pl.program_id(1)
    base = (b * nq + i) * nk
    n_hi = nhi_ref[b * nq + i]
    cq = cq_ref[0]
    lane = lax.broadcasted_iota(I32, (tq, LANES), 1)
    first = lane < HEAD_DIM
    spt = tk // LANES

    for h in range(DSA_HEADS):
        slab = q_ref[:, (h // 2) * LANES:(h // 2 + 1) * LANES]
        keep = first if h % 2 == 0 else jnp.logical_not(first)
        qst_s[h * tq:(h + 1) * tq, :] = jnp.where(keep, slab, jnp.zeros_like(slab))
    for h in range(IDX_HEADS):
        slab = iq_ref[:, (h // 2) * LANES:(h // 2 + 1) * LANES]
        keep = first if h % 2 == 0 else jnp.logical_not(first)
        iqst_s[h * tq:(h + 1) * tq, :] = jnp.where(keep, slab, jnp.zeros_like(slab))
        iwb_s[h * tq:(h + 1) * tq, :] = jnp.broadcast_to(iw_ref[:, h:h + 1], (tq, LANES))

    cand_s[...] = jnp.full(cand_s.shape, -jnp.inf, F32)

    def fold_candidates(sc):
        for g in range(tq // CAND_ROWS):
            rows = slice(g * CAND_ROWS, (g + 1) * CAND_ROWS)
            tops = [cand_s[r, rows, :] for r in range(TOP_PER_LANE)]
            for c in range(tk // LANES):
                x = sc[rows, c * LANES:(c + 1) * LANES]
                for r in range(TOP_PER_LANE):
                    tops[r], x = jnp.maximum(tops[r], x), jnp.minimum(tops[r], x)
            for r in range(TOP_PER_LANE):
                cand_s[r, rows, :] = tops[r]

    def score_body(t, carry):
        st = status_ref[base + t]
        off = pl.multiple_of(t * tk, tk)

        @pl.when(st == 0)
        def _():
            for c in range(spt):
                keys_s[t * spt + c] = jnp.full((tq, LANES), -jnp.inf, F32)

        def scores(masked):
            ikt = ikk_ref[pl.ds(off, tk), :]
            logits = _dot_nt(iqst_s[...], ikt)
            r = jnp.maximum(logits, 0.0) * _lane_tile(iwb_s[...], tk)
            sc = r[0:tq]
            for h in range(1, IDX_HEADS):
                sc = sc + r[h * tq:(h + 1) * tq]
            sc = jnp.where(sc == 0.0, 0.0, sc)
            if masked:
                ck = ck_ref[0, :, pl.ds(off, tk)]
                sc = jnp.where(cq >= ck, sc, -jnp.inf)
            for c in range(spt):
                keys_s[t * spt + c] = sc[:, c * LANES:(c + 1) * LANES]
            fold_candidates(sc)

        @pl.when(st == 1)
        def _():
            scores(False)

        @pl.when(st == 2)
        def _():
            scores(True)

        return carry

    lax.fori_loop(0, n_hi, score_body, 0)

    rg = min(tq, LANES)
    lane_rg = lax.broadcasted_iota(I32, (rg, LANES), 1)

    def count_tiles(preds, *operands):
        zero = jnp.zeros((rg, LANES), I32)
        accs = []
        for r in range(tq // rg):
            rows = slice(r * rg, (r + 1) * rg)
            ops = [o[rows] for o in operands]

            def body(t, acc, rows=rows, ops=ops):
                off = pl.multiple_of(t * tk, tk)
                acc = list(acc)
                for c in range(spt):
                    slab = keys_s[t * spt + c, rows, :]
                    for n, pred in enumerate(preds):
                        acc[n] = acc[n] + jnp.where(pred(slab, off + c * LANES, *ops), 1, 0).astype(I32)
                return tuple(acc)
            accs.append(lax.fori_loop(0, n_hi, body, (zero,) * len(preds)))
        outs = []
        for n in range(len(preds)):
            acc = accs[0][n] if len(accs) == 1 else jnp.concatenate([a[n] for a in accs], axis=0)
            outs.append(_row_total(acc))
        return outs

    ge = lambda slab, col0, c: slab >= c
    gt = lambda slab, col0, c: slab > c
    eq = lambda slab, col0, c: slab == c

    def count_cand(rows, c):
        n_groups = TOP_PER_LANE // CAND_GROUP
        full = [cand_s[CAND_GROUP * i + CAND_GROUP - 1, rows, :] >= c for i in range(n_groups)]
        base = jnp.full(c.shape, TOP_PER_LANE, I32)
        rest = [jnp.full(c.shape, -jnp.inf, F32)] * (CAND_GROUP - 1)
        for i in reversed(range(n_groups)):
            base = jnp.where(full[i], base, CAND_GROUP * i)
            rest = [jnp.where(full[i], rest[k], cand_s[CAND_GROUP * i + k, rows, :])
                    for k in range(CAND_GROUP - 1)]
        for x in rest:
            base = base + jnp.where(x >= c, 1, 0).astype(I32)
        return base

    def bisect(count):
        def bit_body(j, thr):
            cand = thr + lax.shift_left(jnp.int32(1), 31 - j)
            cnt, = count([ge], _key_to_f32(cand))
            return jnp.where(cnt >= top_k, cand, thr)
        return lax.fori_loop(0, 32, bit_body, jnp.full((tq, LANES), INT_MIN, I32))

    def bisect_candidates():
        half = tq // 2
        halves = (slice(0, half), slice(half, tq))

        def raw_count(rows, key):
            return count_cand(rows, _key_to_f32(key))

        def decide(acc, cand, thr):
            return jnp.where(_row_total(acc) >= top_k, cand, thr)

        bit = lambda j: lax.shift_left(jnp.int32(1), 31 - j)
        start = jnp.full((half, LANES), INT_MIN, I32)

        def body(j, carry):
            thr_a, cand_a, acc_a, thr_b = carry
            cand_b = thr_b + bit(j)
            acc_b = raw_count(halves[1], cand_b)
            thr_a = decide(acc_a, cand_a, thr_a)
            cand_a = thr_a + bit(j + 1)
            acc_a = raw_count(halves[0], cand_a)
            thr_b = decide(acc_b, cand_b, thr_b)
            return thr_a, cand_a, acc_a, thr_b

        cand_a = start + bit(0)
        carry = (start, cand_a, raw_count(halves[0], cand_a), start)
        thr_a, cand_a, acc_a, thr_b = lax.fori_loop(0, 31, body, carry)
        thr_a = decide(acc_a, cand_a, thr_a)
        cand_b = thr_b + bit(31)
        thr_b = decide(raw_count(halves[1], cand_b), cand_b, thr_b)
        return jnp.concatenate([thr_a, thr_b], axis=0)

    lane_k = lax.broadcasted_iota(I32, (tq, tk), 1)
    lane_v = lax.broadcasted_iota(I32, (tk, LANES), 1)

    def prepare(thr_key, cnt_gt, cnt_eq):
        real = thr_key > MASKED_KEY
        thr = jnp.where(real, _key_to_f32(thr_key), -jnp.inf)
        thf_s[...] = thr
        need = top_k - cnt_gt
        excess = jnp.logical_and(real, cnt_eq > need)
        jst_s[...] = jnp.where(real, jnp.int32(s_len), jnp.int32(-1))

        @pl.when(jnp.max(excess.astype(I32)) > 0)
        def _():
            def idx_body(j, jcur):
                cand = jcur + lax.shift_left(jnp.int32(1), (s_len.bit_length() - 2) - j)
                cnt, = count_tiles(
                    [lambda slab, col0, th, c: jnp.logical_and(slab == th, (col0 + lane_rg) < c)], thr, cand)
                return jnp.where(cnt < need, cand, jcur)
            jbest = lax.fori_loop(0, s_len.bit_length() - 1, idx_body, jnp.zeros((tq, LANES), I32))
            jst_s[...] = jnp.where(excess, jbest, jst_s[...])

    def attend():
        m_s[...] = jnp.full(m_s.shape, NEG, F32)
        acc_s[...] = jnp.zeros(acc_s.shape, F32)
        nall_s[...] = jnp.zeros(nall_s.shape, I32)

        def attn_body(t, carry):
            st = status_ref[base + t]

            @pl.when(st != 0)
            def _():
                off = pl.multiple_of(t * tk, tk)
                slabs = [keys_s[t * spt + c] for c in range(spt)]
                seen = nall_s[...]
                for slab in slabs:
                    seen = seen + jnp.where(slab >= floor_s[...], 1, 0).astype(I32)
                nall_s[...] = seen
                key = jnp.concatenate(slabs, axis=1)
                th = _lane_tile(thf_s[...], tk)
                jst = _lane_tile(jst_s[...], tk)
                sel = jnp.logical_or(key > th, jnp.logical_and(key == th, (lane_k + off) <= jst))
                bias = jnp.where(sel, 0.0, NEG)
                kt = kk_ref[pl.ds(off, tk), :]
                vt = vv_ref[pl.ds(off, tk), :]
                vt = jnp.where(lane_v < HEAD_DIM, vt, jnp.ones_like(vt))
                s = _dot_nt(qst_s[...], kt)
                s = (s.reshape(DSA_HEADS, tq, tk) + bias[None]).reshape(DSA_HEADS * tq, tk)
                m_old = m_s[...]
                m_new = jnp.maximum(m_old, jnp.max(s, axis=1, keepdims=True))
                p = jnp.exp2(s - _lane_tile(m_new, tk))
                alpha = jnp.exp2(m_old - m_new)
                acc_s[...] = alpha * acc_s[...] + _dot(p.astype(BF16), vt)
                m_s[...] = m_new

            return carry

        lax.fori_loop(0, n_hi, attn_body, 0)

    thr_key = bisect_candidates()
    everything = slice(0, tq)
    cnt_gt = _row_total(count_cand(everything, _key_to_f32(thr_key + 1)))
    cnt_eq = _row_total(count_cand(everything, _key_to_f32(thr_key))) - cnt_gt
    floor = _key_to_f32(jnp.maximum(thr_key, MASKED_KEY + 1))
    floor_s[...] = floor
    ncand_s[...] = _row_total(count_cand(everything, floor))
    prepare(thr_key, cnt_gt, cnt_eq)
    attend()

    @pl.when(jnp.max((_row_total(nall_s[...]) != ncand_s[...]).astype(I32)) > 0)
    def _():
        thr_all = bisect(count_tiles)
        thr_f = jnp.where(thr_all > MASKED_KEY, _key_to_f32(thr_all), -jnp.inf)
        gt_all, eq_all = count_tiles([gt, eq], thr_f)
        prepare(thr_all, gt_all, eq_all)
        attend()

    for j in range(DSA_HEADS // 2):
        acc_a = acc_s[2 * j * tq:(2 * j + 1) * tq]
        acc_b = acc_s[(2 * j + 1) * tq:(2 * j + 2) * tq]
        swap_a = pltpu.roll(acc_a, HEAD_DIM, 1)
        swap_b = pltpu.roll(acc_b, HEAD_DIM, 1)
        o_ref[:, j * LANES:(j + 1) * LANES] = jnp.where(first, acc_a / swap_a,
                                                        swap_b / acc_b).astype(o_ref.dtype)


def _dsa_attention(q, iq, iw, kk, vv, ikk, cq3, ck3, status, nhi, b, s, tq, tk, top_k):
    nq, nk = s // tq, s // tk
    kern = functools.partial(_dsa_kernel, tq=tq, tk=tk, nq=nq, nk=nk, s_len=s, top_k=top_k)
    rowblk = lambda bb, i, *_: (bb * nq + i, 0)
    perb = lambda bb, i, *_: (bb, 0)
    grid_spec = pltpu.PrefetchScalarGridSpec(
        num_scalar_prefetch=2,
        grid=(b, nq),
        in_specs=[
            pl.BlockSpec((tq, SEQ_WIDTH), rowblk),
            pl.BlockSpec((tq, IDX_HEADS * IDX_DIM), rowblk),
            pl.BlockSpec((tq, LANES), rowblk),
            pl.BlockSpec((s, LANES), perb),
            pl.BlockSpec((s, LANES), perb),
            pl.BlockSpec((s, LANES), perb),
            pl.BlockSpec((1, tq, 1), lambda bb, i, *_: (bb * nq + i, 0, 0)),
            pl.BlockSpec((1, 1, s), lambda bb, i, *_: (bb, 0, 0)),
        ],
        out_specs=pl.BlockSpec((tq, SEQ_WIDTH), rowblk),
        scratch_shapes=[
            pltpu.VMEM((s // LANES, tq, LANES), F32),
            pltpu.VMEM((TOP_PER_LANE, tq, LANES), F32),
            pltpu.VMEM((DSA_HEADS * tq, LANES), BF16),
            pltpu.VMEM((IDX_HEADS * tq, LANES), BF16),
            pltpu.VMEM((IDX_HEADS * tq, LANES), F32),
            pltpu.VMEM((DSA_HEADS * tq, LANES), F32),
            pltpu.VMEM((DSA_HEADS * tq, LANES), F32),
            pltpu.VMEM((tq, LANES), F32),
            pltpu.VMEM((tq, LANES), I32),
            pltpu.VMEM((tq, LANES), F32),
            pltpu.VMEM((tq, LANES), I32),
            pltpu.VMEM((tq, LANES), I32),
        ],
    )
    return pl.pallas_call(
        kern, grid_spec=grid_spec,
        out_shape=jax.ShapeDtypeStruct((b * s, SEQ_WIDTH), BF16),
        compiler_params=_params(("parallel", "arbitrary")), name="dsa_attn",
    )(status, nhi, q, iq, iw, kk, vv, ikk, cq3, ck3)


def _mix_mlp_kernel(x_ref, oseq_ref, qm_ref, km_ref, vm_ref, wo_ref, gmix_ref, gpre_ref, wup_ref, wdn_ref,
                    gpost_ref, o_ref, h_s, acc_s):
    j = pl.program_id(1)

    @pl.when(j == 0)
    def _():
        qm = qm_ref[...]
        km = km_ref[...]
        vm = vm_ref[...]
        lane = lax.broadcasted_iota(I32, qm.shape, 1)
        mine = [(lane // HEAD_DIM) == h for h in range(MEM_HEADS)]
        q_st = jnp.concatenate([jnp.where(m_h, qm, jnp.zeros_like(qm)) for m_h in mine], axis=0)
        s = _dot_nt(q_st, km)
        p = jnp.exp(s - jnp.max(s, axis=1, keepdims=True))
        p = p / jnp.sum(p, axis=1, keepdims=True)
        o_st = _dot(p.astype(BF16), vm)
        tm = qm.shape[0]
        o_mem = jnp.zeros(qm.shape, F32)
        for h in range(MEM_HEADS):
            o_mem = o_mem + jnp.where(mine[h], o_st[h * tm:(h + 1) * tm], 0.0)
        y = (_dot(oseq_ref[...], wo_ref[0:SEQ_WIDTH, :])
             + _dot(o_mem.astype(BF16), wo_ref[SEQ_WIDTH:, :]))
        x_mid = x_ref[...] + _rms(y, gmix_ref[...])
        o_ref[...] = x_mid
        h_s[...] = _rms(x_mid, gpre_ref[...]).astype(BF16)
        acc_s[...] = jnp.zeros(acc_s.shape, F32)

    u = jnp.maximum(_dot(h_s[...], wup_ref[...]), 0.0)
    acc_s[...] += _dot((u * u).astype(BF16), wdn_ref[...])

    @pl.when(j == pl.num_programs(1) - 1)
    def _():
        o_ref[...] = o_ref[...] + _rms(acc_s[...], gpost_ref[...])


def _mix_mlp(x2d, oseq, qm, kv, w_out, g_mix, g_pre, w_up, w_dn, g_post, s, n_mem, tm, tf):
    m = x2d.shape[0]
    nt = s // tm
    row = lambda i, j: (i, 0)
    fixed = lambda i, j: (0, 0)
    vec = lambda g: g.reshape(1, D_MODEL)
    return pl.pallas_call(
        _mix_mlp_kernel, grid=(m // tm, D_FF // tf),
        in_specs=[
            pl.BlockSpec((tm, D_MODEL), row),
            pl.BlockSpec((tm, SEQ_WIDTH), row),
            pl.BlockSpec((tm, MEM_WIDTH), row),
            pl.BlockSpec((n_mem, MEM_WIDTH), lambda i, j: (i // nt, 0)),
            pl.BlockSpec((n_mem, MEM_WIDTH), lambda i, j: (i // nt, 1)),
            pl.BlockSpec((D_MODEL, D_MODEL), fixed),
            pl.BlockSpec((1, D_MODEL), fixed),
            pl.BlockSpec((1, D_MODEL), fixed),
            pl.BlockSpec((D_MODEL, tf), lambda i, j: (0, j)),
            pl.BlockSpec((tf, D_MODEL), lambda i, j: (j, 0)),
            pl.BlockSpec((1, D_MODEL), fixed),
        ],
        out_specs=pl.BlockSpec((tm, D_MODEL), row),
        out_shape=jax.ShapeDtypeStruct(x2d.shape, F32),
        scratch_shapes=[pltpu.VMEM((tm, D_MODEL), BF16), pltpu.VMEM((tm, D_MODEL), F32)],
        compiler_params=_params(("parallel", "arbitrary")), name="mix_mlp",
    )(x2d, oseq, qm, kv, kv, w_out, vec(g_mix), vec(g_pre), w_up, w_dn, vec(g_post))


def _rope_tables(positions):
    half = HEAD_DIM // 2
    inv = ROPE_THETA ** (-jnp.arange(0, HEAD_DIM, 2, dtype=F32) / HEAD_DIM)
    ang = positions.astype(F32).reshape(-1, 1) * inv
    cos, sin = jnp.cos(ang), jnp.sin(ang)
    reps = LANES // HEAD_DIM
    cos_t = jnp.tile(cos, (1, 2 * reps))
    sin_t = jnp.tile(jnp.concatenate([-sin, sin], axis=1), (1, reps))
    assert cos_t.shape[1] == LANES and half * 2 == HEAD_DIM
    return cos_t, sin_t


def _visibility(chunk_id, tq, tk):
    b, s = chunk_id.shape
    cq = chunk_id.reshape(b, s // tq, tq)
    ck = chunk_id.reshape(b, s // tk, tk)
    qmin, qmax = cq.min(-1)[:, :, None], cq.max(-1)[:, :, None]
    kmin, kmax = ck.min(-1)[:, None, :], ck.max(-1)[:, None, :]
    status = jnp.where(kmin > qmax, 0, jnp.where(kmax <= qmin, 1, 2)).astype(I32)
    tiles = jnp.arange(s // tk, dtype=I32)[None, None, :]
    nhi = jnp.max(jnp.where(status != 0, tiles + 1, 0), axis=-1).astype(I32)
    return status.reshape(-1), nhi.reshape(-1)


def _dup(w):
    return jnp.concatenate([w, w], axis=1)


def kernel(x, mem, positions, g_pre_mix, g_post_mix, g_mem, w_mem_kv, w_out, g_pre_mlp, g_post_mlp,
           w_mlp_up, w_mlp_down, w_in_diff, lambda_q1, lambda_k1, lambda_q2, lambda_k2, g_diff_subln,
           w_in_dsa):
    b, s, d = x.shape
    n_mem = mem.shape[1]
    depth = g_pre_mix.shape[0]
    m = b * s
    scale = HEAD_DIM ** -0.5
    scale2 = scale * math.log2(math.e)
    tm_proj = min(1024, s)
    tq_diff, tk_diff = min(512, s), min(1024, s)
    tq_dsa, tk_dsa = min(256, s), min(512, s)
    tm_mlp, tf_mlp = min(1024, s), 1024
    top_k = min(DSA_TOPK_MAX, s // 4)
    assert tk_dsa >= top_k, "the threshold search needs one key tile to hold top_k candidates"

    cos_t, sin_t = _rope_tables(positions)
    chunk_id = positions // CHUNK
    assert tq_diff % tq_dsa == 0
    cq3 = chunk_id.reshape(m // tq_dsa, tq_dsa, 1)
    ck3 = chunk_id.reshape(b, 1, s)
    vis_diff = _visibility(chunk_id, tq_diff, tk_diff)
    vis_dsa = _visibility(chunk_id, tq_dsa, tk_dsa)

    x2d = x.reshape(m, d)
    mem2d = mem.reshape(b * n_mem, d)
    for i in range(depth):
        j = i // 2
        kv = _inproj(mem2d, g_mem[i], None, None, None, w_mem_kv[i].astype(BF16), (),
                     ((2 * MEM_WIDTH, 1.0, BF16),), min(256, b * n_mem))[0]
        if i % 2 == 0:
            w = w_in_diff[j].astype(BF16)
            nqk = 2 * DIFF_HEADS * HEAD_DIM
            q, k, v, qm = _inproj(
                x2d, g_pre_mix[i], cos_t, sin_t, w[:, :2 * nqk], w[:, 2 * nqk:],
                ((nqk, scale2, BF16), (nqk, 1.0, BF16)),
                ((SEQ_WIDTH, 1.0, BF16), (MEM_WIDTH, scale, BF16)), tm_proj)
            lam_p = jnp.stack([lambda_q1[j], lambda_k1[j], lambda_q2[j], lambda_k2[j]]).astype(F32)
            lam_init = 0.8 - 0.6 * math.exp(-0.3 * i)
            o_seq = _diff_attention(
                q, k, v, cq3, ck3, vis_diff[0], vis_diff[1],
                lam_p, g_diff_subln[j].reshape(1, LANES), b, s, tq_diff, tk_diff, lam_init)
        else:
            w = w_in_dsa[j]
            o0 = 0
            parts = []
            for width in (DSA_HEADS * HEAD_DIM, HEAD_DIM, HEAD_DIM, IDX_HEADS * IDX_DIM, IDX_DIM,
                          IDX_HEADS, MEM_WIDTH):
                parts.append(w[:, o0:o0 + width])
                o0 += width
            wq, wk, wv, wiq, wik, wiw, wqm = parts
            wiw = jnp.pad(wiw, ((0, 0), (0, LANES - IDX_HEADS)))
            w_rope = jnp.concatenate([wq, wiq, _dup(wk), _dup(wik)], axis=1).astype(BF16)
            w_plain = jnp.concatenate([_dup(wv), wiw, wqm], axis=1).astype(BF16)
            iw_scale = IDX_HEADS ** -0.5 * IDX_DIM ** -0.5
            q, iq, kk, ikk, vv, iw, qm = _inproj(
                x2d, g_pre_mix[i], cos_t, sin_t, w_rope, w_plain,
                ((DSA_HEADS * HEAD_DIM, scale2, BF16), (IDX_HEADS * IDX_DIM, 1.0, BF16),
                 (LANES, 1.0, BF16), (LANES, 1.0, BF16)),
                ((LANES, 1.0, BF16), (LANES, iw_scale, F32), (MEM_WIDTH, scale, BF16)), tm_proj)
            o_seq = _dsa_attention(
                q, iq, iw, kk, vv, ikk, cq3, ck3,
                vis_dsa[0], vis_dsa[1], b, s, tq_dsa, tk_dsa, top_k)
        x2d = _mix_mlp(x2d, o_seq, qm, kv, w_out[i].astype(BF16), g_post_mix[i], g_pre_mlp[i],
                       w_mlp_up[i].astype(BF16), w_mlp_down[i].astype(BF16), g_post_mlp[i],
                       s, n_mem, tm_mlp, tf_mlp)
    return x2d.reshape(b, s, d)
```
